```python
import math
import jax, jax.numpy as jnp
from jax import lax
import numpy as np

D_MODEL = 1024
BATCH = 16
SEQ = 4096
DEPTH = 1

GMLP_WIDTH = 512
GMLP_GROUPS = 8
GMLP_HEAD = GMLP_WIDTH // GMLP_GROUPS
CHUNK = 128
S5_WIDTH = 256
S5_GROUP_CH = 16
S5_GROUPS = S5_WIDTH // S5_GROUP_CH
S5_STATE = 64
P_TOTAL = 2 * GMLP_WIDTH + S5_WIDTH + 2 * D_MODEL
N_EXPERTS = 16
CAPACITY_FACTOR = 2
EXPERT_FF = 2048
EPS = 1e-6

kernel_name = "hybrid_gmlp_s5_expert_choice_encoder_block"


def rms_norm(x, g):
    xf = x.astype(jnp.float32)
    r = lax.rsqrt(jnp.mean(xf * xf, axis=-1, keepdims=True) + EPS)
    return (xf * r * g.astype(jnp.float32)).astype(x.dtype)


def gmlp_spatial_gating(u, v, ln_g, ln_b, w_s, b_s):
    bn, seq, _ = u.shape
    vf = v.astype(jnp.float32)
    mu = jnp.mean(vf, axis=-1, keepdims=True)
    var = jnp.mean(jnp.square(vf - mu), axis=-1, keepdims=True)
    vn = (vf - mu) * lax.rsqrt(var + EPS) * ln_g.astype(jnp.float32) + ln_b.astype(jnp.float32)
    vc = vn.reshape(bn, seq // CHUNK, CHUNK, GMLP_GROUPS, GMLP_HEAD)
    z = jnp.einsum('gts,bnsgc->bntgc', w_s.astype(jnp.float32), vc) \
        + jnp.transpose(b_s.astype(jnp.float32))[:, :, None]
    return u * z.reshape(bn, seq, GMLP_WIDTH).astype(u.dtype)


def _ssm_combine(e1, e2):
    a1, b1 = e1
    a2, b2 = e2
    return a1 * a2, a2 * b1 + b2


def s5_direction(u, lam_re, lam_im, log_dt, b_re, b_im, c_re, c_im, reverse):
    f32 = jnp.float32
    lam = lax.complex(lam_re.astype(f32), lam_im.astype(f32))
    dt = jnp.exp(log_dt.astype(f32))[:, None]
    lam_bar = jnp.exp(lam * dt)
    bmat = lax.complex(b_re.astype(f32), b_im.astype(f32))
    b_bar = ((lam_bar - 1.0) / lam)[:, :, None] * bmat
    bu = jnp.einsum('gph,blgh->blgp', b_bar, u.astype(jnp.complex64))
    a = jnp.broadcast_to(lam_bar, bu.shape)
    _, states = lax.associative_scan(_ssm_combine, (a, bu), axis=1, reverse=reverse)
    cmat = lax.complex(c_re.astype(f32), c_im.astype(f32))
    return jnp.real(jnp.einsum('ghp,blgp->blgh', cmat, states))


def s5_mixer(s, lam_re, lam_im, log_dt, b_re, b_im, c_re, c_im, d_skip, w_glu, b_glu):
    bn, seq, _ = s.shape
    sf = s.astype(jnp.float32).reshape(bn, seq, S5_GROUPS, S5_GROUP_CH)
    y = (s5_direction(sf, lam_re[0], lam_im[0], log_dt[0], b_re[0], b_im[0], c_re[0], c_im[0], False)
         + s5_direction(sf, lam_re[1], lam_im[1], log_dt[1], b_re[1], b_im[1], c_re[1], c_im[1], True)
         + d_skip.astype(jnp.float32) * sf)
    y = jax.nn.gelu(y.reshape(bn, seq, S5_WIDTH), approximate=False)
    y = y * jax.nn.sigmoid(y @ w_glu.astype(jnp.float32) + b_glu.astype(jnp.float32))
    return y.astype(s.dtype)


def expert_choice_ffn(h, w_router, w_gate, w_up, w_down):
    bn, seq, dm = h.shape
    cap = CAPACITY_FACTOR * seq // N_EXPERTS
    probs = jax.nn.softmax(jnp.einsum('bld,de->ble', h, w_router).astype(jnp.float32), axis=-1)
    scores = jnp.transpose(probs, (0, 2, 1))
    gate, idx = lax.top_k(scores, cap)
    xs = jax.vmap(lambda hb, ib: hb[ib])(h, idx)
    a = jnp.einsum('becd,edf->becf', xs, w_gate)
    u = jnp.einsum('becd,edf->becf', xs, w_up)
    y = jnp.einsum('becf,efd->becd', jax.nn.silu(a) * u, w_down)
    y = (y * gate[..., None]).astype(h.dtype)
    seg = (idx + (jnp.arange(bn, dtype=jnp.int32) * seq)[:, None, None]).reshape(-1)
    out = jax.ops.segment_sum(y.reshape(-1, dm), seg, num_segments=bn * seq)
    return out.reshape(bn, seq, dm)


def setup_inputs(seed: int = 0) -> dict:
    key = jax.random.key(seed)
    ks = jax.random.split(key, 32)
    f32 = jnp.float32
    nrm = lambda k, shape, scale: (jax.random.normal(k, shape, f32) * scale)
    L_ = DEPTH
    x = jax.random.normal(ks[0], (BATCH, SEQ, D_MODEL), f32)
    norm1_g = 1.0 + nrm(ks[1], (L_, D_MODEL), 0.02)
    w_in = nrm(ks[2], (L_, D_MODEL, P_TOTAL), D_MODEL ** -0.5)
    b_gate = nrm(ks[3], (L_, 2 * D_MODEL), 0.02)
    gmlp_ln_g = 1.0 + nrm(ks[4], (L_, GMLP_WIDTH), 0.02)
    gmlp_ln_b = nrm(ks[5], (L_, GMLP_WIDTH), 0.02)
    gmlp_w_s = nrm(ks[6], (L_, GMLP_GROUPS, CHUNK, CHUNK), CHUNK ** -0.5)
    gmlp_b_s = 1.0 + nrm(ks[7], (L_, GMLP_GROUPS, CHUNK), 0.1)
    dshape = (L_, 2, S5_GROUPS, S5_STATE)
    s5_lam_re = -0.5 + nrm(ks[8], dshape, 0.01)
    n_idx = jnp.arange(S5_STATE, dtype=f32)
    s5_lam_im = jnp.pi * n_idx + nrm(ks[9], dshape, 0.01)
    s5_log_dt = jax.random.uniform(ks[10], (L_, 2, S5_GROUPS), f32,
                                   minval=math.log(0.001), maxval=math.log(0.1))
    bscale = (2.0 * S5_GROUP_CH) ** -0.5
    s5_b_re = nrm(ks[11], (L_, 2, S5_GROUPS, S5_STATE, S5_GROUP_CH), bscale)
    s5_b_im = nrm(ks[12], (L_, 2, S5_GROUPS, S5_STATE, S5_GROUP_CH), bscale)
    cscale = (2.0 * S5_STATE) ** -0.5
    s5_c_re = nrm(ks[13], (L_, 2, S5_GROUPS, S5_GROUP_CH, S5_STATE), cscale)
    s5_c_im = nrm(ks[14], (L_, 2, S5_GROUPS, S5_GROUP_CH, S5_STATE), cscale)
    s5_d = nrm(ks[15], (L_, S5_GROUPS, S5_GROUP_CH), 1.0)
    s5_w_glu = nrm(ks[16], (L_, S5_WIDTH, S5_WIDTH), S5_WIDTH ** -0.5)
    s5_b_glu = nrm(ks[17], (L_, S5_WIDTH), 0.02)
    w_up_a = nrm(ks[18], (L_, GMLP_WIDTH, D_MODEL), GMLP_WIDTH ** -0.5)
    w_up_b = nrm(ks[19], (L_, S5_WIDTH, D_MODEL), S5_WIDTH ** -0.5)
    w_out = nrm(ks[20], (L_, D_MODEL, D_MODEL), D_MODEL ** -0.5)
    norm2_g = 1.0 + nrm(ks[21], (L_, D_MODEL), 0.02)
    w_router = nrm(ks[22], (L_, D_MODEL, N_EXPERTS), D_MODEL ** -0.5)
    w_gate = nrm(ks[23], (L_, N_EXPERTS, D_MODEL, EXPERT_FF), D_MODEL ** -0.5)
    w_up = nrm(ks[24], (L_, N_EXPERTS, D_MODEL, EXPERT_FF), D_MODEL ** -0.5)
    w_down = nrm(ks[25], (L_, N_EXPERTS, EXPERT_FF, D_MODEL), EXPERT_FF ** -0.5)
    final_g = 1.0 + nrm(ks[26], (D_MODEL,), 0.02)
    return {"x": x, "norm1_g": norm1_g, "w_in": w_in, "b_gate": b_gate,
            "gmlp_ln_g": gmlp_ln_g, "gmlp_ln_b": gmlp_ln_b, "gmlp_w_s": gmlp_w_s, "gmlp_b_s": gmlp_b_s,
            "s5_lam_re": s5_lam_re, "s5_lam_im": s5_lam_im, "s5_log_dt": s5_log_dt,
            "s5_b_re": s5_b_re, "s5_b_im": s5_b_im, "s5_c_re": s5_c_re, "s5_c_im": s5_c_im,
            "s5_d": s5_d, "s5_w_glu": s5_w_glu, "s5_b_glu": s5_b_glu,
            "w_up_a": w_up_a, "w_up_b": w_up_b, "w_out": w_out, "norm2_g": norm2_g,
            "w_router": w_router, "w_gate": w_gate, "w_up": w_up, "w_down": w_down,
            "final_g": final_g}


def reference(x, norm1_g, w_in, b_gate, gmlp_ln_g, gmlp_ln_b, gmlp_w_s, gmlp_b_s,
              s5_lam_re, s5_lam_im, s5_log_dt, s5_b_re, s5_b_im, s5_c_re, s5_c_im,
              s5_d, s5_w_glu, s5_b_glu, w_up_a, w_up_b, w_out, norm2_g,
              w_router, w_gate, w_up, w_down, final_g):
    o_s5 = 2 * GMLP_WIDTH
    o_gate = o_s5 + S5_WIDTH
    for l in range(DEPTH):
        h = rms_norm(x, norm1_g[l])
        p = h @ w_in[l]
        uv = jax.nn.gelu(p[..., :o_s5], approximate=False)
        s_in = p[..., o_s5:o_gate]
        gates = jax.nn.sigmoid(p[..., o_gate:] + b_gate[l])
        g_a, g_b = gates[..., :D_MODEL], gates[..., D_MODEL:]
        br_a = gmlp_spatial_gating(uv[..., :GMLP_WIDTH], uv[..., GMLP_WIDTH:],
                                   gmlp_ln_g[l], gmlp_ln_b[l], gmlp_w_s[l], gmlp_b_s[l])
        br_b = s5_mixer(s_in, s5_lam_re[l], s5_lam_im[l], s5_log_dt[l], s5_b_re[l], s5_b_im[l],
                        s5_c_re[l], s5_c_im[l], s5_d[l], s5_w_glu[l], s5_b_glu[l])
        merged = g_a * (br_a @ w_up_a[l]) + g_b * (br_b @ w_up_b[l])
        x = x + (merged @ w_out[l]).astype(x.dtype)
        h2 = rms_norm(x, norm2_g[l])
        x = x + expert_choice_ffn(h2, w_router[l], w_gate[l], w_up[l], w_down[l]).astype(x.dtype)
    return rms_norm(x, final_g)
```

```python
import functools

import jax
import jax.numpy as jnp
from jax import lax
from jax.experimental import pallas as pl
from jax.experimental.pallas import tpu as pltpu

F32 = jnp.float32
BF16 = jnp.bfloat16
I32 = jnp.int32
U32 = jnp.uint32

D_MODEL = 1024
BATCH = 16
SEQ = 4096
GMLP_WIDTH = 512
GMLP_GROUPS = 8
GMLP_HEAD = GMLP_WIDTH // GMLP_GROUPS
CHUNK = 128
S5_WIDTH = 256
S5_GROUP_CH = 16
S5_GROUPS = S5_WIDTH // S5_GROUP_CH
S5_STATE = 64
N_EXPERTS = 16
CAPACITY = 2 * SEQ // N_EXPERTS
EXPERT_FF = 2048
EPS = 1e-6

LANES = 128
S5_TC = 16
S5_NCH = SEQ // S5_TC
TILE_A = 1024
TILE_M = 512
TILE_F = 512
TILE_O = 1024
SCAN_LANES = 256
VMEM_LIMIT = 56 * 1024 * 1024

_NT = (((1,), (1,)), ((), ()))


def _gelu(x):
    return 0.5 * x * (1.0 + lax.erf(x * (2.0 ** -0.5)))


def _rms(x, g):
    return x * lax.rsqrt(jnp.mean(x * x, axis=-1, keepdims=True) + EPS) * g


def _lane_block(shape):
    return lax.broadcasted_iota(I32, shape, 1) // S5_GROUP_CH


def _s5_tables(lam_re, lam_im, log_dt, b_re, b_im, c_re, c_im, d_skip):
    hp = lax.Precision.HIGHEST
    tc, ng, nh, npst = S5_TC, S5_GROUPS, S5_GROUP_CH, S5_STATE
    lam = lax.complex(lam_re.astype(F32), lam_im.astype(F32))
    dt = jnp.exp(log_dt.astype(F32))[..., None]
    ldt = lam * dt
    lam_bar = jnp.exp(ldt)
    bbar = ((lam_bar - 1.0) / lam)[..., None] * lax.complex(b_re.astype(F32), b_im.astype(F32))
    cm = lax.complex(c_re.astype(F32), c_im.astype(F32))
    k = jnp.arange(tc + 1, dtype=F32)
    pw = jnp.exp(ldt[None] * k[:, None, None, None])
    taps = jnp.real(jnp.einsum('dghp,kdgp,dgpi->kdghi', cm, pw[:tc], bbar, precision=hp))
    s_i = jnp.arange(tc)[:, None]
    t_i = jnp.arange(tc)[None, :]
    diff = t_i - s_i
    mf = jnp.where((diff >= 0)[:, :, None, None, None], taps[:, 0][jnp.clip(diff, 0, tc - 1)], 0.0)
    mr = jnp.where((diff <= 0)[:, :, None, None, None], taps[:, 1][jnp.clip(-diff, 0, tc - 1)], 0.0)
    m = jnp.transpose(mf + mr, (2, 0, 4, 1, 3))
    eye_t = jnp.eye(tc, dtype=F32)
    eye_h = jnp.eye(nh, dtype=F32)
    m = m + (eye_t[None, :, None, :, None] * eye_h[None, None, :, None, :]
             * d_skip.astype(F32)[:, None, None, None, :])
    perm = (jnp.arange(tc)[None, :] - jnp.arange(ng)[:, None]) % tc
    m = jax.vmap(lambda a, p: a[p][:, :, p])(m, perm).reshape(ng, tc * nh, tc * nh)

    cf = pw[tc - 1 - jnp.arange(tc), 0][..., None] * bbar[0][None]
    cr = pw[jnp.arange(tc), 1][..., None] * bbar[1][None]
    cf = jnp.transpose(cf, (1, 0, 3, 2))
    cr = jnp.transpose(cr, (1, 0, 3, 2))
    wsum = jnp.concatenate([cf.real, cr.real, cf.imag, cr.imag], axis=-1)
    wsum = jax.vmap(lambda a, p: a[p])(wsum, perm).reshape(ng, tc * nh, 4 * npst)

    ctf = cm[0][None] * pw[1 + jnp.arange(tc), 0][:, :, None, :]
    ctr = cm[1][None] * pw[tc - jnp.arange(tc), 1][:, :, None, :]
    ctf = jnp.transpose(ctf, (1, 3, 0, 2))
    ctr = jnp.transpose(ctr, (1, 3, 0, 2))
    wc = jnp.concatenate([ctf.real, ctr.real, -ctf.imag, -ctr.imag], axis=1)
    wc = jax.vmap(lambda a, p: a[:, p])(wc, perm).reshape(ng, 4 * npst, tc * nh)

    a_re = jnp.concatenate([pw[tc, 0].real, pw[tc, 1].real], axis=-1)
    a_im = jnp.concatenate([pw[tc, 0].imag, pw[tc, 1].imag], axis=-1)
    return m.astype(BF16), wsum.astype(BF16), wc.astype(BF16), a_re, a_im


def _s5_in_kernel(x_ref, g1_ref, ws5_ref, wsum_ref, z_ref, sre_ref, sim_ref, s_scr):
    n = TILE_A // S5_TC
    hn = _rms(x_ref[0], g1_ref[...]).astype(BF16)
    s = jnp.dot(hn, ws5_ref[...], preferred_element_type=F32)
    s_scr[0] = s[:, :LANES]
    s_scr[1] = s[:, LANES:]
    rows = []
    for t in range(S5_TC):
        a = jnp.concatenate([s_scr[0, pl.ds(t, n, stride=S5_TC), :],
                             s_scr[1, pl.ds(t, n, stride=S5_TC), :]], axis=1)
        rows.append(pltpu.roll(a, S5_GROUP_CH * t, 1) if t else a)
    blk = _lane_block((n, S5_WIDTH))
    for g in range(S5_GROUPS):
        tsel = (blk - g) & (S5_TC - 1)
        z = rows[0]
        for t in range(1, S5_TC):
            z = jnp.where(tsel == t, rows[t], z)
        zb = z.astype(BF16)
        z_ref[g] = zb
        sm = jnp.dot(zb, wsum_ref[g], preferred_element_type=F32)
        sre_ref[pl.ds(g, n, stride=S5_GROUPS), :] = sm[:, :LANES]
        sim_ref[pl.ds(g, n, stride=S5_GROUPS), :] = sm[:, LANES:]


def _s5_in(x, g1, ws5, wsum):
    nt = SEQ // TILE_A
    n = TILE_A // S5_TC
    return pl.pallas_call(
        _s5_in_kernel,
        grid=(BATCH, nt),
        in_specs=[
            pl.BlockSpec((1, TILE_A, D_MODEL), lambda b, i: (b, i, 0)),
            pl.BlockSpec((1, D_MODEL), lambda b, i: (0, 0)),
            pl.BlockSpec((D_MODEL, S5_WIDTH), lambda b, i: (0, 0)),
            pl.BlockSpec((S5_GROUPS, S5_WIDTH, S5_WIDTH), lambda b, i: (0, 0, 0)),
        ],
        out_specs=[
            pl.BlockSpec((S5_GROUPS, n, S5_WIDTH), lambda b, i: (0, b * nt + i, 0)),
            pl.BlockSpec((n * S5_GROUPS, LANES), lambda b, i: (i, b)),
            pl.BlockSpec((n * S5_GROUPS, LANES), lambda b, i: (i, b)),
        ],
        out_shape=[
            jax.ShapeDtypeStruct((S5_GROUPS, BATCH * S5_NCH, S5_WIDTH), BF16),
            jax.ShapeDtypeStruct((S5_NCH * S5_GROUPS, BATCH * LANES), F32),
            jax.ShapeDtypeStruct((S5_NCH * S5_GROUPS, BATCH * LANES), F32),
        ],
        scratch_shapes=[pltpu.VMEM((2, TILE_A, LANES), F32)],
        compiler_params=pltpu.CompilerParams(
            dimension_semantics=("parallel", "parallel"), vmem_limit_bytes=VMEM_LIMIT),
        name="s5_in",
    )(x, g1, ws5, wsum)


def _s5_scan_kernel(sre_ref, sim_ref, are_ref, aim_ref, xre_ref, xim_ref):
    ar = are_ref[...]
    ai = aim_ref[...]
    fwd = (lax.broadcasted_iota(I32, ar.shape, 1) % LANES) < S5_STATE
    zero = jnp.zeros_like(ar)

    def step(c, xr, xi):
        sr = sre_ref[c]
        si = sim_ref[c]
        return ar * xr - ai * xi + sr, ar * xi + ai * xr + si

    def fwd_body(c, carry):
        xr, xi = carry
        xre_ref[c] = xr
        xim_ref[c] = xi
        return step(c, xr, xi)

    lax.fori_loop(0, S5_NCH, fwd_body, (zero, zero))

    def rev_body(i, carry):
        xr, xi = carry
        c = S5_NCH - 1 - i
        xre_ref[c] = jnp.where(fwd, xre_ref[c], xr)
        xim_ref[c] = jnp.where(fwd, xim_ref[c], xi)
        return step(c, xr, xi)

    lax.fori_loop(0, S5_NCH, rev_body, (zero, zero))


def _s5_scan(s_re, s_im, a_re, a_im):
    nl = BATCH * LANES
    blk3 = pl.BlockSpec((S5_NCH, S5_GROUPS, SCAN_LANES), lambda j: (0, 0, j))
    blk2 = pl.BlockSpec((S5_GROUPS, SCAN_LANES), lambda j: (0, j))
    return pl.pallas_call(
        _s5_scan_kernel,
        grid=(nl // SCAN_LANES,),
        in_specs=[blk3, blk3, blk2, blk2],
        out_specs=[blk3, blk3],
        out_shape=[jax.ShapeDtypeStruct((S5_NCH, S5_GROUPS, nl), F32)] * 2,
        compiler_params=pltpu.CompilerParams(
            dimension_semantics=("parallel",), vmem_limit_bytes=VMEM_LIMIT),
        name="s5_scan",
    )(s_re, s_im, a_re, a_im)


def _s5_out_kernel(z_ref, xre_ref, xim_ref, m_ref, wc_ref, y_ref, yg_scr):
    n = S5_NCH
    for g in range(S5_GROUPS):
        xc = jnp.concatenate([xre_ref[pl.ds(g, n, stride=S5_GROUPS), :],
                              xim_ref[pl.ds(g, n, stride=S5_GROUPS), :]], axis=1).astype(BF16)
        yg_scr[g] = (jnp.dot(z_ref[g], m_ref[g], preferred_element_type=F32)
                     + jnp.dot(xc, wc_ref[g], preferred_element_type=F32))
    blk = _lane_block((n, S5_WIDTH))
    for t in range(S5_TC):
        gsel = (blk - t) & (S5_GROUPS - 1)
        acc = yg_scr[0]
        for g in range(1, S5_GROUPS):
            acc = jnp.where(gsel == g, yg_scr[g], acc)
        if t:
            acc = pltpu.roll(acc, S5_WIDTH - S5_GROUP_CH * t, 1)
        y_ref[0, 0, pl.ds(t, n, stride=S5_TC), :] = acc[:, :LANES]
        y_ref[0, 1, pl.ds(t, n, stride=S5_TC), :] = acc[:, LANES:]


def _s5_out(z, xc_re, xc_im, m, wc):
    wspec = pl.BlockSpec((S5_GROUPS, S5_WIDTH, S5_WIDTH), lambda b: (0, 0, 0))
    xspec = pl.BlockSpec((S5_NCH * S5_GROUPS, LANES), lambda b: (0, b))
    return pl.pallas_call(
        _s5_out_kernel,
        grid=(BATCH,),
        in_specs=[pl.BlockSpec((S5_GROUPS, S5_NCH, S5_WIDTH), lambda b: (0, b, 0)), xspec, xspec, wspec, wspec],
        out_specs=pl.BlockSpec((1, 2, SEQ, LANES), lambda b: (b, 0, 0, 0)),
        out_shape=jax.ShapeDtypeStruct((BATCH, 2, SEQ, LANES), F32),
        scratch_shapes=[pltpu.VMEM((S5_GROUPS, S5_NCH, S5_WIDTH), F32)],
        compiler_params=pltpu.CompilerParams(
            dimension_semantics=("parallel",), vmem_limit_bytes=VMEM_LIMIT),
        name="s5_out",
    )(z, xc_re, xc_im, m, wc)


def _mixer_kernel(x_ref, y_ref, g1_ref, wuv_ref, wga_ref, wgb_ref, bga_ref, bgb_ref, lng_ref, lnb_ref,
                  ws_ref, bs_ref, wglu_ref, bglu_ref, wupa_ref, wupb_ref, wout_ref, g2_ref, wr_ref,
                  x1_ref, h2p_ref, pr_ref, bra_scr):
    x = x_ref[0]
    hn = _rms(x, g1_ref[...]).astype(BF16)

    uv = _gelu(jnp.dot(hn, wuv_ref[...], preferred_element_type=F32))
    u = uv[:, :GMLP_WIDTH]
    v = uv[:, GMLP_WIDTH:]
    mu = jnp.mean(v, axis=-1, keepdims=True)
    vc = v - mu
    var = jnp.mean(vc * vc, axis=-1, keepdims=True)
    vn = (vc * lax.rsqrt(var + EPS) * lng_ref[...] + lnb_ref[...]).astype(BF16)
    low = lax.broadcasted_iota(I32, (CHUNK, LANES), 1) < GMLP_HEAD
    for ch in range(TILE_M // CHUNK):
        r0 = ch * CHUNK
        for q in range(GMLP_WIDTH // LANES):
            c0 = q * LANES
            vb = vn[r0:r0 + CHUNK, c0:c0 + LANES]
            z0 = jnp.dot(ws_ref[2 * q], vb, preferred_element_type=F32)
            z1 = jnp.dot(ws_ref[2 * q + 1], vb, preferred_element_type=F32)
            zz = jnp.where(low, z0, z1) + bs_ref[:, c0:c0 + LANES]
            bra_scr[r0:r0 + CHUNK, c0:c0 + LANES] = (u[r0:r0 + CHUNK, c0:c0 + LANES] * zz).astype(BF16)

    yb = _gelu(jnp.concatenate([y_ref[0, 0], y_ref[0, 1]], axis=1))
    gl = jnp.dot(yb.astype(BF16), wglu_ref[...], preferred_element_type=F32) + bglu_ref[...]
    brb = (yb * jax.nn.sigmoid(gl)).astype(BF16)

    ga = jax.nn.sigmoid(jnp.dot(hn, wga_ref[...], preferred_element_type=F32) + bga_ref[...])
    merged = ga * jnp.dot(bra_scr[...], wupa_ref[...], preferred_element_type=F32)
    gb = jax.nn.sigmoid(jnp.dot(hn, wgb_ref[...], preferred_element_type=F32) + bgb_ref[...])
    merged = merged + gb * jnp.dot(brb, wupb_ref[...], preferred_element_type=F32)
    x1 = x + jnp.dot(merged.astype(BF16), wout_ref[...], preferred_element_type=F32)
    x1_ref[0] = x1

    h2 = _rms(x1, g2_ref[...])
    half = D_MODEL // 2
    lo = pltpu.bitcast(h2[:, :half].astype(BF16).astype(F32), U32)
    hi = pltpu.bitcast(h2[:, half:].astype(BF16).astype(F32), U32)
    h2p_ref[0] = (lo >> 16) | hi

    logits = lax.dot_general(wr_ref[...], h2, _NT, precision=lax.Precision.HIGHEST,
                             preferred_element_type=F32)
    mx = jnp.max(logits, axis=0, keepdims=True)
    ex = jnp.exp(logits - mx)
    pr_ref[0] = ex / jnp.sum(ex, axis=0, keepdims=True)


def _mixer(x, y, consts):
    nt = SEQ // TILE_M

    def cspec(a):
        nd = a.ndim
        return pl.BlockSpec(a.shape, lambda b, i: (0,) * nd)

    return pl.pallas_call(
        _mixer_kernel,
        grid=(BATCH, nt),
        in_specs=[pl.BlockSpec((1, TILE_M, D_MODEL), lambda b, i: (b, i, 0)),
                  pl.BlockSpec((1, 2, TILE_M, LANES), lambda b, i: (b, 0, i, 0))] + [cspec(a) for a in consts],
        out_specs=[pl.BlockSpec((1, TILE_M, D_MODEL), lambda b, i: (b, i, 0)),
                   pl.BlockSpec((1, TILE_M, D_MODEL // 2), lambda b, i: (b, i, 0)),
                   pl.BlockSpec((1, N_EXPERTS, TILE_M), lambda b, i: (b, 0, i))],
        out_shape=[jax.ShapeDtypeStruct((BATCH, SEQ, D_MODEL), F32),
                   jax.ShapeDtypeStruct((BATCH, SEQ, D_MODEL // 2), U32),
                   jax.ShapeDtypeStruct((BATCH, N_EXPERTS, SEQ), F32)],
        scratch_shapes=[pltpu.VMEM((TILE_M, GMLP_WIDTH), BF16)],
        compiler_params=pltpu.CompilerParams(
            dimension_semantics=("parallel", "parallel"), vmem_limit_bytes=VMEM_LIMIT),
        name="mixer",
    )(x, y, *consts)


TOPK_TBLK = 1024


def _cumsum_lanes(mask_bf, tri, out_scr):
    carry = jnp.zeros((N_EXPERTS, 1), F32)
    for j in range(SEQ // LANES):
        cs = jnp.dot(mask_bf[:, j * LANES:(j + 1) * LANES], tri, preferred_element_type=F32) + carry
        out_scr[:, j * LANES:(j + 1) * LANES] = cs
        carry = cs[:, LANES - 1:LANES]


def _topk_kernel(p_ref, tri_ref, idx_ref, gate_ref, cum_scr, r_scr):
    p = p_ref[0]
    bits = pltpu.bitcast(p, I32)

    def search(i, v):
        cand = v | (jnp.int32(1) << (30 - i))
        cnt = jnp.sum(jnp.where(bits >= cand, 1.0, 0.0), axis=1, keepdims=True)
        return jnp.where(cnt >= CAPACITY, cand, v)

    thr = lax.fori_loop(0, 31, search, jnp.zeros((N_EXPERTS, 1), I32))
    gt = bits > thr
    eq = bits == thr
    need = CAPACITY - jnp.sum(jnp.where(gt, 1.0, 0.0), axis=1, keepdims=True)
    tri = tri_ref[...]
    eqf = jnp.where(eq, 1.0, 0.0)
    _cumsum_lanes(eqf.astype(BF16), tri, cum_scr)
    sel = gt | (eq & ((cum_scr[...] - eqf) < need))
    _cumsum_lanes(jnp.where(sel, 1.0, 0.0).astype(BF16), tri, cum_scr)
    cum_scr[...] = jnp.where(sel, cum_scr[...], 0.0)

    tok = lax.broadcasted_iota(I32, (1, SEQ), 1)
    r_scr[...] = jnp.zeros_like(r_scr)
    r_scr[0:1, :] = (tok // 64).astype(F32)
    r_scr[1:2, :] = (tok % 64).astype(F32)
    slot = (lax.broadcasted_iota(I32, (CAPACITY, 1), 0) + 1).astype(F32)

    def per_expert(e, _):
        pe = p_ref[0, pl.ds(e, 1), :]
        hi = pe.astype(BF16).astype(F32)
        mid = (pe - hi).astype(BF16).astype(F32)
        r_scr[2:3, :] = hi
        r_scr[3:4, :] = mid
        r_scr[4:5, :] = pe - hi - mid
        ce = cum_scr[pl.ds(e, 1), :]
        res = jnp.zeros((CAPACITY, LANES), F32)
        for c in range(SEQ // TOPK_TBLK):
            sl = slice(c * TOPK_TBLK, (c + 1) * TOPK_TBLK)
            onehot = jnp.where(ce[:, sl] == slot, 1.0, 0.0).astype(BF16)
            res = res + lax.dot_general(onehot, r_scr[:, sl].astype(BF16), _NT, preferred_element_type=F32)
        rt = res.T
        idx_ref[0, pl.ds(e, 1), :] = (rt[0:1] * 64.0 + rt[1:2]).astype(I32)
        gate_ref[0, pl.ds(e, 1), :] = rt[2:3] + rt[3:4] + rt[4:5]
        return 0

    lax.fori_loop(0, N_EXPERTS, per_expert, 0)


def _topk(probs, tri):
    return pl.pallas_call(
        _topk_kernel,
        grid=(BATCH,),
        in_specs=[pl.BlockSpec((1, N_EXPERTS, SEQ), lambda b: (b, 0, 0)),
                  pl.BlockSpec((LANES, LANES), lambda b: (0, 0))],
        out_specs=[pl.BlockSpec((1, N_EXPERTS, CAPACITY), lambda b: (b, 0, 0)),
                   pl.BlockSpec((1, N_EXPERTS, CAPACITY), lambda b: (b, 0, 0))],
        out_shape=[jax.ShapeDtypeStruct((BATCH, N_EXPERTS, CAPACITY), I32),
                   jax.ShapeDtypeStruct((BATCH, N_EXPERTS, CAPACITY), F32)],
        scratch_shapes=[pltpu.VMEM((N_EXPERTS, SEQ), F32), pltpu.VMEM((LANES, SEQ), F32)],
        compiler_params=pltpu.CompilerParams(
            dimension_semantics=("parallel",), vmem_limit_bytes=VMEM_LIMIT),
        name="topk",
    )(probs, tri)


def _moe_kernel(idx_ref, gate_ref, h2p_ref, wg_ref, wu_ref, wd_ref, out_ref, xg_scr, xs_scr, y_scr):
    e = pl.program_id(1)
    f = pl.program_id(2)
    nf = pl.num_programs(2)
    half = D_MODEL // 2

    @pl.when((e == 0) & (f == 0))
    def _():
        out_ref[...] = jnp.zeros_like(out_ref)

    @pl.when(f == 0)
    def _():
        def gather(c, _):
            r = idx_ref[0, e, c]
            xg_scr[pl.ds(c, 1), :] = h2p_ref[0, pl.ds(r, 1), :]
            return 0

        lax.fori_loop(0, CAPACITY, gather, 0, unroll=8)
        w = xg_scr[...]
        xs_scr[:, :half] = pltpu.bitcast(w << 16, F32).astype(BF16)
        xs_scr[:, half:] = pltpu.bitcast(w & jnp.uint32(0xFFFF0000), F32).astype(BF16)

    xs = xs_scr[...]
    a = jnp.dot(xs, wg_ref[0], preferred_element_type=F32)
    u = jnp.dot(xs, wu_ref[0], preferred_element_type=F32)
    hm = (a * jax.nn.sigmoid(a) * u).astype(BF16)
    y = jnp.dot(hm, wd_ref[0], preferred_element_type=F32)

    @pl.when(f == 0)
    def _():
        y_scr[...] = y

    @pl.when(f > 0)
    def _():
        y_scr[...] += y

    @pl.when(f == nf - 1)
    def _():
        def scatter(c, _):
            r = idx_ref[0, e, c]
            g = gate_ref[0, e, c]
            out_ref[0, pl.ds(r, 1), :] = out_ref[0, pl.ds(r, 1), :] + g * y_scr[pl.ds(c, 1), :]
            return 0

        lax.fori_loop(0, CAPACITY, scatter, 0, unroll=4)


def _moe(idx, gate, h2p, wg, wu, wd):
    nf = EXPERT_FF // TILE_F
    sspec = pl.BlockSpec((1, N_EXPERTS, CAPACITY), lambda b, e, f: (b, 0, 0), memory_space=pltpu.SMEM)
    return pl.pallas_call(
        _moe_kernel,
        grid=(BATCH, N_EXPERTS, nf),
        in_specs=[sspec, sspec,
                  pl.BlockSpec((1, SEQ, D_MODEL // 2), lambda b, e, f: (b, 0, 0), pipeline_mode=pl.Buffered(1)),
                  pl.BlockSpec((1, D_MODEL, TILE_F), lambda b, e, f: (e, 0, f)),
                  pl.BlockSpec((1, D_MODEL, TILE_F), lambda b, e, f: (e, 0, f)),
                  pl.BlockSpec((1, TILE_F, D_MODEL), lambda b, e, f: (e, f, 0))],
        out_specs=pl.BlockSpec((1, SEQ, D_MODEL), lambda b, e, f: (b, 0, 0), pipeline_mode=pl.Buffered(1)),
        out_shape=jax.ShapeDtypeStruct((BATCH, SEQ, D_MODEL), F32),
        scratch_shapes=[pltpu.VMEM((CAPACITY, D_MODEL // 2), U32),
                        pltpu.VMEM((CAPACITY, D_MODEL), BF16),
                        pltpu.VMEM((CAPACITY, D_MODEL), F32)],
        compiler_params=pltpu.CompilerParams(
            dimension_semantics=("parallel", "arbitrary", "arbitrary"), vmem_limit_bytes=VMEM_LIMIT),
        name="moe",
    )(idx, gate, h2p, wg, wu, wd)


def _final_kernel(x1_ref, moe_ref, g_ref, o_ref):
    o_ref[0] = _rms(x1_ref[0] + moe_ref[0], g_ref[...])


def _final(x1, moe, g):
    spec = pl.BlockSpec((1, TILE_O, D_MODEL), lambda b, i: (b, i, 0))
    return pl.pallas_call(
        _final_kernel,
        grid=(BATCH, SEQ // TILE_O),
        in_specs=[spec, spec, pl.BlockSpec((1, D_MODEL), lambda b, i: (0, 0))],
        out_specs=spec,
        out_shape=jax.ShapeDtypeStruct((BATCH, SEQ, D_MODEL), F32),
        compiler_params=pltpu.CompilerParams(
            dimension_semantics=("parallel", "parallel"), vmem_limit_bytes=VMEM_LIMIT),
        name="final",
    )(x1, moe, g)


def kernel(x, norm1_g, w_in, b_gate, gmlp_ln_g, gmlp_ln_b, gmlp_w_s, gmlp_b_s, s5_lam_re, s5_lam_im, s5_log_dt,
           s5_b_re, s5_b_im, s5_c_re, s5_c_im, s5_d, s5_w_glu, s5_b_glu, w_up_a, w_up_b, w_out, norm2_g,
           w_router, w_gate, w_up, w_down, final_g):
    assert x.shape == (BATCH, SEQ, D_MODEL) and w_in.shape[0] == 1
    l = 0
    o_s5 = 2 * GMLP_WIDTH
    o_gate = o_s5 + S5_WIDTH
    row = lambda a: a.astype(F32).reshape(1, -1)

    w_in_b = w_in[l].astype(BF16)
    m_mat, wsum, wc, a_re, a_im = _s5_tables(s5_lam_re[l], s5_lam_im[l], s5_log_dt[l], s5_b_re[l], s5_b_im[l],
                                             s5_c_re[l], s5_c_im[l], s5_d[l])
    g1 = row(norm1_g[l])

    z, s_re, s_im = _s5_in(x, g1, w_in_b[:, o_s5:o_gate], wsum)
    shape3 = (S5_NCH, S5_GROUPS, BATCH * LANES)
    xc_re, xc_im = _s5_scan(s_re.reshape(shape3), s_im.reshape(shape3),
                            jnp.tile(a_re, (1, BATCH)), jnp.tile(a_im, (1, BATCH)))
    shape2 = (S5_NCH * S5_GROUPS, BATCH * LANES)
    y = _s5_out(z, xc_re.reshape(shape2), xc_im.reshape(shape2), m_mat, wc)

    consts = (
        g1, w_in_b[:, :o_s5], w_in_b[:, o_gate:o_gate + D_MODEL], w_in_b[:, o_gate + D_MODEL:],
        row(b_gate[l][:D_MODEL]), row(b_gate[l][D_MODEL:]), row(gmlp_ln_g[l]), row(gmlp_ln_b[l]),
        gmlp_w_s[l].astype(BF16), jnp.repeat(jnp.transpose(gmlp_b_s[l].astype(F32)), GMLP_HEAD, axis=1),
        s5_w_glu[l].astype(BF16), row(s5_b_glu[l]), w_up_a[l].astype(BF16), w_up_b[l].astype(BF16),
        w_out[l].astype(BF16), row(norm2_g[l]), jnp.transpose(w_router[l].astype(F32)),
    )
    x1, h2p, probs = _mixer(x, y, consts)

    tri = (jnp.arange(LANES)[:, None] <= jnp.arange(LANES)[None, :]).astype(BF16)
    idx, gate = _topk(probs, tri)
    moe = _moe(idx, gate, h2p, w_gate[l].astype(BF16), w_up[l].astype(BF16), w_down[l].astype(BF16))
    return _final(x1, moe, row(final_g))
```

```python
import jax
import jax.numpy as jnp
import numpy as np
from jax import lax
from jax.experimental import pallas as pl
from jax.experimental.pallas import tpu as pltpu

F32 = jnp.float32
BF16 = jnp.bfloat16
I32 = jnp.int32
U32 = jnp.uint32

D_MODEL = 1024
BATCH = 16
SEQ = 4096
GMLP_WIDTH = 512
GMLP_GROUPS = 8
GMLP_HEAD = GMLP_WIDTH // GMLP_GROUPS
CHUNK = 128
S5_WIDTH = 256
S5_GROUP_CH = 16
S5_GROUPS = S5_WIDTH // S5_GROUP_CH
S5_STATE = 64
N_EXPERTS = 16
CAPACITY = 2 * SEQ // N_EXPERTS
EXPERT_FF = 2048
EPS = 1e-6
O_S5 = 2 * GMLP_WIDTH
O_GATE = O_S5 + S5_WIDTH

LANES = 128
S5_TC = 16
S5_NCH = SEQ // S5_TC
TILE_A = 1024
TILE_M = 512
TILE_F = 1024
TILE_O = 1024
SCAN_LANES = 256
VMEM_LIMIT = 56 * 1024 * 1024

_NT = (((1,), (1,)), ((), ()))


def _gelu(x):
    return 0.5 * x * (1.0 + lax.erf(x * (2.0 ** -0.5)))


def _rms(x, g):
    return x * lax.rsqrt(jnp.mean(x * x, axis=-1, keepdims=True) + EPS) * g


def _lane_block(shape):
    return lax.broadcasted_iota(I32, shape, 1) // S5_GROUP_CH


def _cpow(er, ei, k):
    mag = jnp.exp(k * er)
    return mag * jnp.cos(k * ei), mag * jnp.sin(k * ei)


def _cmul(a, b):
    return a[0] * b[0] - a[1] * b[1], a[0] * b[1] + a[1] * b[0]


def _s5_tables(lam_re, lam_im, log_dt, b_re, b_im, c_re, c_im, d_skip):
    hp = lax.Precision.HIGHEST
    tc, ng, nh = S5_TC, S5_GROUPS, S5_GROUP_CH
    lr, li = lam_re.astype(F32), lam_im.astype(F32)
    dt = jnp.exp(log_dt.astype(F32))[..., None]
    er, ei = lr * dt, li * dt
    lb = _cpow(er, ei, 1.0)
    den = lr * lr + li * li
    q = (((lb[0] - 1.0) * lr + lb[1] * li) / den, (lb[1] * lr - (lb[0] - 1.0) * li) / den)
    bbar = _cmul((q[0][..., None], q[1][..., None]), (b_re.astype(F32), b_im.astype(F32)))
    cm = (c_re.astype(F32), c_im.astype(F32))

    kk = np.arange(tc, dtype=np.float32)[:, None, None, None]
    pw = _cpow(er[None], ei[None], kk)
    t = _cmul((cm[0][None], cm[1][None]), (pw[0][:, :, :, None, :], pw[1][:, :, :, None, :]))
    taps = (jnp.einsum('kdghp,dgpi->kdghi', t[0], bbar[0], precision=hp)
            - jnp.einsum('kdghp,dgpi->kdghi', t[1], bbar[1], precision=hp))

    tau = (np.arange(tc)[None, :] - np.arange(ng)[:, None]) % tc
    diff = tau[:, None, :] - tau[:, :, None]
    oh_f = (diff[..., None] == np.arange(tc)).astype(np.float32)
    oh_r = (-diff[..., None] == np.arange(tc)).astype(np.float32)
    m = (jnp.einsum('gabk,kghi->gaibh', oh_f, taps[:, 0], precision=hp)
         + jnp.einsum('gabk,kghi->gaibh', oh_r, taps[:, 1], precision=hp))
    eye_j = np.eye(tc, dtype=np.float32)[None, :, None, :, None]
    eye_h = np.eye(nh, dtype=np.float32)[None, None, :, None, :]
    m = (m + eye_j * eye_h * d_skip.astype(F32)[:, None, None, None, :]).reshape(ng, tc * nh, tc * nh)

    tauf = tau.astype(np.float32)
    bt = (jnp.transpose(bbar[0], (0, 1, 3, 2)), jnp.transpose(bbar[1], (0, 1, 3, 2)))
    parts = []
    for d, kin in ((0, tc - 1.0 - tauf), (1, tauf)):
        p_in = _cpow(er[d][:, None, None, :], ei[d][:, None, None, :], kin[:, :, None, None])
        parts.append(_cmul(p_in, (bt[0][d][:, None], bt[1][d][:, None])))
    wsum = jnp.concatenate([parts[0][0], parts[1][0], parts[0][1], parts[1][1]], axis=-1)
    wsum = wsum.reshape(ng, tc * nh, 4 * S5_STATE)

    ct = (jnp.transpose(cm[0], (0, 1, 3, 2)), jnp.transpose(cm[1], (0, 1, 3, 2)))
    parts = []
    for d, kout in ((0, tauf + 1.0), (1, tc - tauf)):
        p_out = _cpow(er[d][:, :, None, None], ei[d][:, :, None, None], kout[:, None, :, None])
        parts.append(_cmul(p_out, (ct[0][d][:, :, None, :], ct[1][d][:, :, None, :])))
    wc = jnp.concatenate([parts[0][0], parts[1][0], -parts[0][1], -parts[1][1]], axis=1)
    wc = wc.reshape(ng, 4 * S5_STATE, tc * nh)

    a16 = _cpow(er, ei, float(tc))
    a_re = jnp.concatenate([a16[0][0], a16[0][1]], axis=-1)
    a_im = jnp.concatenate([a16[1][0], a16[1][1]], axis=-1)
    return m.astype(BF16), wsum.astype(BF16), wc.astype(BF16), a_re, a_im


def _s5_in_kernel(x_ref, g1_ref, ws5_ref, wsum_ref, z_ref, sre_ref, sim_ref, s_scr):
    n = TILE_A // S5_TC
    hn = _rms(x_ref[0], g1_ref[...]).astype(BF16)
    s = jnp.dot(hn, ws5_ref[...], preferred_element_type=F32)
    s_scr[0] = s[:, :LANES]
    s_scr[1] = s[:, LANES:]
    rows = []
    for t in range(S5_TC):
        a = jnp.concatenate([s_scr[0, pl.ds(t, n, stride=S5_TC), :],
                             s_scr[1, pl.ds(t, n, stride=S5_TC), :]], axis=1)
        rows.append(pltpu.roll(a, S5_GROUP_CH * t, 1) if t else a)
    blk = _lane_block((n, S5_WIDTH))
    for g in range(S5_GROUPS):
        tsel = (blk - g) & (S5_TC - 1)
        z = rows[0]
        for t in range(1, S5_TC):
            z = jnp.where(tsel == t, rows[t], z)
        zb = z.astype(BF16)
        z_ref[g] = zb
        sm = jnp.dot(zb, wsum_ref[g], preferred_element_type=F32)
        sre_ref[pl.ds(g, n, stride=S5_GROUPS), :] = sm[:, :LANES]
        sim_ref[pl.ds(g, n, stride=S5_GROUPS), :] = sm[:, LANES:]


def _s5_in(x, g1, ws5, wsum):
    nt = SEQ // TILE_A
    n = TILE_A // S5_TC
    return pl.pallas_call(
        _s5_in_kernel,
        grid=(BATCH, nt),
        in_specs=[
            pl.BlockSpec((1, TILE_A, D_MODEL), lambda b, i: (b, i, 0)),
            pl.BlockSpec((1, D_MODEL), lambda b, i: (0, 0)),
            pl.BlockSpec((D_MODEL, S5_WIDTH), lambda b, i: (0, O_S5 // S5_WIDTH)),
            pl.BlockSpec((S5_GROUPS, S5_WIDTH, S5_WIDTH), lambda b, i: (0, 0, 0)),
        ],
        out_specs=[
            pl.BlockSpec((S5_GROUPS, n, S5_WIDTH), lambda b, i: (0, b * nt + i, 0)),
            pl.BlockSpec((n * S5_GROUPS, LANES), lambda b, i: (i, b)),
            pl.BlockSpec((n * S5_GROUPS, LANES), lambda b, i: (i, b)),
        ],
        out_shape=[
            jax.ShapeDtypeStruct((S5_GROUPS, BATCH * S5_NCH, S5_WIDTH), BF16),
            jax.ShapeDtypeStruct((S5_NCH * S5_GROUPS, BATCH * LANES), F32),
            jax.ShapeDtypeStruct((S5_NCH * S5_GROUPS, BATCH * LANES), F32),
        ],
        scratch_shapes=[pltpu.VMEM((2, TILE_A, LANES), F32)],
        compiler_params=pltpu.CompilerParams(
            dimension_semantics=("parallel", "parallel"), vmem_limit_bytes=VMEM_LIMIT),
        name="s5_in",
    )(x, g1, ws5, wsum)


def _s5_scan_kernel(sre_ref, sim_ref, are_ref, aim_ref, xre_ref, xim_ref):
    ar = are_ref[...]
    ai = aim_ref[...]
    fwd = (lax.broadcasted_iota(I32, ar.shape, 1) % LANES) < S5_STATE
    zero = jnp.zeros_like(ar)

    def step(c, xr, xi):
        sr = sre_ref[c]
        si = sim_ref[c]
        return ar * xr - ai * xi + sr, ar * xi + ai * xr + si

    def fwd_body(c, carry):
        xr, xi = carry
        xre_ref[c] = xr
        xim_ref[c] = xi
        return step(c, xr, xi)

    lax.fori_loop(0, S5_NCH, fwd_body, (zero, zero))

    def rev_body(i, carry):
        xr, xi = carry
        c = S5_NCH - 1 - i
        xre_ref[c] = jnp.where(fwd, xre_ref[c], xr)
        xim_ref[c] = jnp.where(fwd, xim_ref[c], xi)
        return step(c, xr, xi)

    lax.fori_loop(0, S5_NCH, rev_body, (zero, zero))


def _s5_scan(s_re, s_im, a_re, a_im):
    nl = BATCH * LANES
    blk3 = pl.BlockSpec((S5_NCH, S5_GROUPS, SCAN_LANES), lambda j: (0, 0, j))
    blk2 = pl.BlockSpec((S5_GROUPS, SCAN_LANES), lambda j: (0, j))
    return pl.pallas_call(
        _s5_scan_kernel,
        grid=(nl // SCAN_LANES,),
        in_specs=[blk3, blk3, blk2, blk2],
        out_specs=[blk3, blk3],
        out_shape=[jax.ShapeDtypeStruct((S5_NCH, S5_GROUPS, nl), F32)] * 2,
        compiler_params=pltpu.CompilerParams(
            dimension_semantics=("parallel",), vmem_limit_bytes=VMEM_LIMIT),
        name="s5_scan",
    )(s_re, s_im, a_re, a_im)


def _s5_out_kernel(z_ref, xre_ref, xim_ref, m_ref, wc_ref, y_ref, yg_scr):
    n = S5_NCH
    for g in range(S5_GROUPS):
        xc = jnp.concatenate([xre_ref[pl.ds(g, n, stride=S5_GROUPS), :],
                              xim_ref[pl.ds(g, n, stride=S5_GROUPS), :]], axis=1).astype(BF16)
        yg_scr[g] = (jnp.dot(z_ref[g], m_ref[g], preferred_element_type=F32)
                     + jnp.dot(xc, wc_ref[g], preferred_element_type=F32))
    blk = _lane_block((n, S5_WIDTH))
    for t in range(S5_TC):
        gsel = (blk - t) & (S5_GROUPS - 1)
        acc = yg_scr[0]
        for g in range(1, S5_GROUPS):
            acc = jnp.where(gsel == g, yg_scr[g], acc)
        if t:
            acc = pltpu.roll(acc, S5_WIDTH - S5_GROUP_CH * t, 1)
        y_ref[0, 0, pl.ds(t, n, stride=S5_TC), :] = acc[:, :LANES]
        y_ref[0, 1, pl.ds(t, n, stride=S5_TC), :] = acc[:, LANES:]


def _s5_out(z, xc_re, xc_im, m, wc):
    wspec = pl.BlockSpec((S5_GROUPS, S5_WIDTH, S5_WIDTH), lambda b: (0, 0, 0))
    xspec = pl.BlockSpec((S5_NCH * S5_GROUPS, LANES), lambda b: (0, b))
    return pl.pallas_call(
        _s5_out_kernel,
        grid=(BATCH,),
        in_specs=[pl.BlockSpec((S5_GROUPS, S5_NCH, S5_WIDTH), lambda b: (0, b, 0)), xspec, xspec, wspec, wspec],
        out_specs=pl.BlockSpec((1, 2, SEQ, LANES), lambda b: (b, 0, 0, 0)),
        out_shape=jax.ShapeDtypeStruct((BATCH, 2, SEQ, LANES), F32),
        scratch_shapes=[pltpu.VMEM((S5_GROUPS, S5_NCH, S5_WIDTH), F32)],
        compiler_params=pltpu.CompilerParams(
            dimension_semantics=("parallel",), vmem_limit_bytes=VMEM_LIMIT),
        name="s5_out",
    )(z, xc_re, xc_im, m, wc)


def _mixer_kernel(x_ref, y_ref, g1_ref, win_ref, bga_ref, bgb_ref, lng_ref, lnb_ref,
                  ws_ref, bs_ref, wglu_ref, bglu_ref, wupa_ref, wupb_ref, wout_ref, g2_ref, wr_ref,
                  x1_ref, h2p_ref, pr_ref, bra_scr):
    x = x_ref[0]
    hn = _rms(x, g1_ref[...]).astype(BF16)
    wuv_ref = win_ref.at[:, :O_S5]
    wga_ref = win_ref.at[:, O_GATE:O_GATE + D_MODEL]
    wgb_ref = win_ref.at[:, O_GATE + D_MODEL:]

    uv = _gelu(jnp.dot(hn, wuv_ref[...], preferred_element_type=F32))
    u = uv[:, :GMLP_WIDTH]
    v = uv[:, GMLP_WIDTH:]
    mu = jnp.mean(v, axis=-1, keepdims=True)
    vc = v - mu
    var = jnp.mean(vc * vc, axis=-1, keepdims=True)
    vn = (vc * lax.rsqrt(var + EPS) * lng_ref[...] + lnb_ref[...]).astype(BF16)
    low = lax.broadcasted_iota(I32, (CHUNK, LANES), 1) < GMLP_HEAD
    for ch in range(TILE_M // CHUNK):
        r0 = ch * CHUNK
        for q in range(GMLP_WIDTH // LANES):
            c0 = q * LANES
            vb = vn[r0:r0 + CHUNK, c0:c0 + LANES]
            z0 = jnp.dot(ws_ref[2 * q], vb, preferred_element_type=F32)
            z1 = jnp.dot(ws_ref[2 * q + 1], vb, preferred_element_type=F32)
            zz = jnp.where(low, z0, z1) + bs_ref[:, c0:c0 + LANES]
            bra_scr[r0:r0 + CHUNK, c0:c0 + LANES] = (u[r0:r0 + CHUNK, c0:c0 + LANES] * zz).astype(BF16)

    yb = _gelu(jnp.concatenate([y_ref[0, 0], y_ref[0, 1]], axis=1))
    gl = jnp.dot(yb.astype(BF16), wglu_ref[...], preferred_element_type=F32) + bglu_ref[...]
    brb = (yb * jax.nn.sigmoid(gl)).astype(BF16)

    ga = jax.nn.sigmoid(jnp.dot(hn, wga_ref[...], preferred_element_type=F32) + bga_ref[...])
    merged = ga * jnp.dot(bra_scr[...], wupa_ref[...], preferred_element_type=F32)
    gb = jax.nn.sigmoid(jnp.dot(hn, wgb_ref[...], preferred_element_type=F32) + bgb_ref[...])
    merged = merged + gb * jnp.dot(brb, wupb_ref[...], preferred_element_type=F32)
    x1 = x + jnp.dot(merged.astype(BF16), wout_ref[...], preferred_element_type=F32)
    x1_ref[0] = x1

    h2 = _rms(x1, g2_ref[...])
    half = D_MODEL // 2
    h2p_ref[0] = pltpu.pack_elementwise([h2[:, :half], h2[:, half:]], packed_dtype=BF16)

    logits = lax.dot_general(wr_ref[...], h2, _NT, precision=lax.Precision.HIGHEST,
                             preferred_element_type=F32)
    mx = jnp.max(logits, axis=0, keepdims=True)
    ex = jnp.exp(logits - mx)
    pr_ref[0] = ex / jnp.sum(ex, axis=0, keepdims=True)


def _mixer(x, y, consts):
    nt = SEQ // TILE_M

    def cspec(a):
        nd = a.ndim
        return pl.BlockSpec(a.shape, lambda b, i: (0,) * nd)

    return pl.pallas_call(
        _mixer_kernel,
        grid=(BATCH, nt),
        in_specs=[pl.BlockSpec((1, TILE_M, D_MODEL), lambda b, i: (b, i, 0)),
                  pl.BlockSpec((1, 2, TILE_M, LANES), lambda b, i: (b, 0, i, 0))] + [cspec(a) for a in consts],
        out_specs=[pl.BlockSpec((1, TILE_M, D_MODEL), lambda b, i: (b, i, 0)),
                   pl.BlockSpec((1, TILE_M, D_MODEL // 2), lambda b, i: (b, i, 0)),
                   pl.BlockSpec((1, N_EXPERTS, TILE_M), lambda b, i: (b, 0, i))],
        out_shape=[jax.ShapeDtypeStruct((BATCH, SEQ, D_MODEL), F32),
                   jax.ShapeDtypeStruct((BATCH, SEQ, D_MODEL // 2), U32),
                   jax.ShapeDtypeStruct((BATCH, N_EXPERTS, SEQ), F32)],
        scratch_shapes=[pltpu.VMEM((TILE_M, GMLP_WIDTH), BF16)],
        compiler_params=pltpu.CompilerParams(
            dimension_semantics=("parallel", "parallel"), vmem_limit_bytes=VMEM_LIMIT),
        name="mixer",
    )(x, y, *consts)


TOPK_NBLK = SEQ // LANES
TOPK_ROWS = 4 * N_EXPERTS
TOPK_SEARCH_STEPS = 28


def _lane_sum(x):
    return jnp.sum(x, axis=1, keepdims=True)


def _topk_kernel(p_ref, tri_ref, idx_ref, gate_ref, eqc_scr, loc_scr, pj_scr, cb_scr, cbp_scr):
    p = p_ref[...]

    def count_ge(th):
        return _lane_sum(jnp.where(p >= th, 1.0, 0.0))

    def search(_, c):
        lo, hi = c
        w = hi - lo
        m1, m2, m3 = lo + 0.25 * w, lo + 0.5 * w, lo + 0.75 * w
        c1, c2, c3 = count_ge(m1) >= CAPACITY, count_ge(m2) >= CAPACITY, count_ge(m3) >= CAPACITY
        return (jnp.where(c3, m3, jnp.where(c2, m2, jnp.where(c1, m1, lo))),
                jnp.where(c1, jnp.where(c2, jnp.where(c3, hi, m3), m2), m1))

    lo, hi = lax.fori_loop(0, TOPK_SEARCH_STEPS, search,
                           (jnp.zeros((TOPK_ROWS, 1), F32), jnp.full((TOPK_ROWS, 1), 2.0, F32)))
    gt = p >= hi
    eq = (p >= lo) & (p < hi)
    need = CAPACITY - _lane_sum(jnp.where(gt, 1.0, 0.0))
    tri = tri_ref[...]
    eqf = jnp.where(eq, 1.0, 0.0)
    carry = jnp.zeros((TOPK_ROWS, 1), F32)
    for j in range(TOPK_NBLK):
        sl = slice(j * LANES, (j + 1) * LANES)
        cs = jnp.dot(eqf[:, sl].astype(BF16), tri, preferred_element_type=F32) + carry
        eqc_scr[:, sl] = cs
        carry = cs[:, LANES - 1:LANES]
    sel = gt | (eq & ((eqc_scr[...] - eqf) < need))
    self = jnp.where(sel, 1.0, 0.0)

    lane = lax.broadcasted_iota(I32, (TOPK_ROWS, LANES), 1)
    cb = jnp.full((TOPK_ROWS, LANES), float(2 * SEQ), F32)
    cbp = cb
    carry = jnp.zeros((TOPK_ROWS, 1), F32)
    for j in range(TOPK_NBLK):
        sl = slice(j * LANES, (j + 1) * LANES)
        loc = jnp.dot(self[:, sl].astype(BF16), tri, preferred_element_type=F32)
        loc_scr[j * TOPK_ROWS:(j + 1) * TOPK_ROWS, :] = loc
        pj_scr[j * TOPK_ROWS:(j + 1) * TOPK_ROWS, :] = p[:, sl]
        cbp = jnp.where(lane == j, carry, cbp)
        carry = carry + loc[:, LANES - 1:LANES]
        cb = jnp.where(lane == j, carry, cb)
    cb_scr[...] = cb
    cbp_scr[...] = cbp

    slot = (lax.broadcasted_iota(I32, (CAPACITY, 1), 0) + 1).astype(F32)
    lane_k = lax.broadcasted_iota(I32, (CAPACITY, LANES), 1)
    lane_kf = lane_k.astype(F32)
    pad = jnp.zeros((LANES - TOPK_NBLK, LANES), F32)

    def block_rows(ref, e):
        return jnp.concatenate([ref[pl.ds(e, TOPK_NBLK, stride=TOPK_ROWS), :], pad], axis=0)

    def per_expert(e, _):
        cb_e = cb_scr[pl.ds(e, 1), :]
        cbp_e = cbp_scr[pl.ds(e, 1), :]
        hit = (cbp_e < slot) & (slot <= cb_e)
        hitb = jnp.where(hit, 1.0, 0.0).astype(BF16)
        both = _lane_sum(jnp.where(hit, cbp_e * float(LANES) + lane_kf, 0.0))
        before = jnp.floor(both * (1.0 / LANES))
        jblk = both - before * float(LANES)
        rank = slot - before
        loc = jnp.dot(hitb, block_rows(loc_scr, e).astype(BF16), preferred_element_type=F32)
        off = _lane_sum(jnp.where(loc < rank, 1.0, 0.0))
        pe = block_rows(pj_scr, e)
        hi = pe.astype(BF16)
        r1 = pe - hi.astype(F32)
        mid = r1.astype(BF16)
        lo = (r1 - mid.astype(F32)).astype(BF16)
        pg = (jnp.dot(hitb, hi, preferred_element_type=F32) + jnp.dot(hitb, mid, preferred_element_type=F32)
              + jnp.dot(hitb, lo, preferred_element_type=F32))
        gate = _lane_sum(jnp.where(lane_kf == off, pg, 0.0))
        tok = jblk * float(LANES) + off
        rt = jnp.where(lane_k == 0, tok, jnp.where(lane_k == 1, gate, 0.0)).T
        idx_ref[pl.ds(e, 1), :] = rt[0:1].astype(I32)
        gate_ref[pl.ds(e, 1), :] = rt[1:2]
        return 0

    lax.fori_loop(0, TOPK_ROWS, per_expert, 0)


def _topk(probs, tri):
    nrow = BATCH * N_EXPERTS
    return pl.pallas_call(
        _topk_kernel,
        grid=(nrow // TOPK_ROWS,),
        in_specs=[pl.BlockSpec((TOPK_ROWS, SEQ), lambda i: (i, 0)),
                  pl.BlockSpec((LANES, LANES), lambda i: (0, 0))],
        out_specs=[pl.BlockSpec((TOPK_ROWS, CAPACITY), lambda i: (i, 0)),
                   pl.BlockSpec((TOPK_ROWS, CAPACITY), lambda i: (i, 0))],
        out_shape=[jax.ShapeDtypeStruct((nrow, CAPACITY), I32),
                   jax.ShapeDtypeStruct((nrow, CAPACITY), F32)],
        scratch_shapes=[pltpu.VMEM((TOPK_ROWS, SEQ), F32),
                        pltpu.VMEM((TOPK_NBLK * TOPK_ROWS, LANES), F32),
                        pltpu.VMEM((TOPK_NBLK * TOPK_ROWS, LANES), F32),
                        pltpu.VMEM((TOPK_ROWS, LANES), F32),
                        pltpu.VMEM((TOPK_ROWS, LANES), F32)],
        compiler_params=pltpu.CompilerParams(
            dimension_semantics=("parallel",), vmem_limit_bytes=VMEM_LIMIT),
        name="topk",
    )(probs, tri)


MOE_GROUP = 8


def _gather_rows(idx_ref, ibase, h2p_ref, dst_ref, dbase, nrows):
    for k in range(nrows):
        r = idx_ref[0, 0, ibase + k]
        dst_ref[pl.ds(dbase + k, 1), :] = h2p_ref[0, pl.ds(r, 1), :]


def _scatter_rows(idx_ref, gate_ref, ibase, scale, y_ref, ybase, out_ref, nrows):
    for k0 in range(0, nrows, MOE_GROUP):
        rs, vals = [], []
        for k in range(k0, k0 + MOE_GROUP):
            r = idx_ref[0, 0, ibase + k]
            g = gate_ref[0, 0, ibase + k] * scale
            rs.append(r)
            vals.append(out_ref[0, pl.ds(r, 1), :] + g * y_ref[pl.ds(ybase + k, 1), :])
        for r, v in zip(rs, vals):
            out_ref[0, pl.ds(r, 1), :] = v


def _moe_kernel(idx_ref, gate_ref, h2p_ref, wg_ref, wu_ref, wd_ref, out_ref, xg_scr, xs_scr, y_scr):
    e = pl.program_id(1)
    f = pl.program_id(2)
    nf = EXPERT_FF // TILE_F
    rq = CAPACITY // nf
    half = D_MODEL // 2
    cur = e % 2
    nxt = 1 - cur
    xg_cur, xg_nxt = xg_scr.at[cur], xg_scr.at[nxt]
    y_cur, y_nxt = y_scr.at[cur], y_scr.at[nxt]

    @pl.when((e == 0) & (f == 0))
    def _():
        out_ref[...] = jnp.zeros_like(out_ref)
        y_scr[...] = jnp.zeros_like(y_scr)

        def gather(i, _):
            base = pl.multiple_of(i * 32, 32)
            _gather_rows(idx_ref, base, h2p_ref, xg_cur, base, 32)
            return 0

        lax.fori_loop(0, CAPACITY // 32, gather, 0)

    @pl.when(f == 0)
    def _():
        w = xg_cur[...]
        for i in range(2):
            xs_scr[:, i * half:(i + 1) * half] = pltpu.unpack_elementwise(
                w, index=i, packed_dtype=BF16, unpacked_dtype=F32).astype(BF16)

    qbase = pl.multiple_of(f * rq, rq)
    e_prev = (e + N_EXPERTS - 1) % N_EXPERTS
    e_next = (e + 1) % N_EXPERTS
    scale = jnp.where(e >= 1, 1.0, 0.0).astype(F32)
    _scatter_rows(idx_ref, gate_ref, e_prev * CAPACITY + qbase, scale, y_nxt, qbase, out_ref, rq)
    _gather_rows(idx_ref, e_next * CAPACITY + qbase, h2p_ref, xg_nxt, qbase, rq)

    xs = xs_scr[...]
    a = jnp.dot(xs, wg_ref[0], preferred_element_type=F32)
    u = jnp.dot(xs, wu_ref[0], preferred_element_type=F32)
    hm = (a * jax.nn.sigmoid(a) * u).astype(BF16)
    y = jnp.dot(hm, wd_ref[0], preferred_element_type=F32)
    y_cur[...] = jnp.where(f > 0, y_cur[...], 0.0) + y

    @pl.when((e == N_EXPERTS - 1) & (f == nf - 1))
    def _():
        def scatter(i, _):
            base = pl.multiple_of(i * 32, 32)
            _scatter_rows(idx_ref, gate_ref, e * CAPACITY + base, jnp.float32(1.0), y_cur, base, out_ref, 32)
            return 0

        lax.fori_loop(0, CAPACITY // 32, scatter, 0)


def _moe(idx, gate, h2p, wg, wu, wd):
    nf = EXPERT_FF // TILE_F
    sspec = pl.BlockSpec((1, 1, N_EXPERTS * CAPACITY), lambda b, e, f: (b, 0, 0), memory_space=pltpu.SMEM)
    return pl.pallas_call(
        _moe_kernel,
        grid=(BATCH, N_EXPERTS, nf),
        in_specs=[sspec, sspec,
                  pl.BlockSpec((1, SEQ, D_MODEL // 2), lambda b, e, f: (b, 0, 0), pipeline_mode=pl.Buffered(1)),
                  pl.BlockSpec((1, D_MODEL, TILE_F), lambda b, e, f: (e, 0, f)),
                  pl.BlockSpec((1, D_MODEL, TILE_F), lambda b, e, f: (e, 0, f)),
                  pl.BlockSpec((1, TILE_F, D_MODEL), lambda b, e, f: (e, f, 0))],
        out_specs=pl.BlockSpec((1, SEQ, D_MODEL), lambda b, e, f: (b, 0, 0), pipeline_mode=pl.Buffered(1)),
        out_shape=jax.ShapeDtypeStruct((BATCH, SEQ, D_MODEL), F32),
        scratch_shapes=[pltpu.VMEM((2, CAPACITY, D_MODEL // 2), U32),
                        pltpu.VMEM((CAPACITY, D_MODEL), BF16),
                        pltpu.VMEM((2, CAPACITY, D_MODEL), F32)],
        compiler_params=pltpu.CompilerParams(
            dimension_semantics=("parallel", "arbitrary", "arbitrary"), vmem_limit_bytes=VMEM_LIMIT),
        name="moe",
    )(idx.reshape(BATCH, 1, N_EXPERTS * CAPACITY), gate.reshape(BATCH, 1, N_EXPERTS * CAPACITY), h2p, wg, wu, wd)


def _final_kernel(x1_ref, moe_ref, g_ref, o_ref):
    o_ref[0] = _rms(x1_ref[0] + moe_ref[0], g_ref[...])


def _final(x1, moe, g):
    spec = pl.BlockSpec((1, TILE_O, D_MODEL), lambda b, i: (b, i, 0))
    return pl.pallas_call(
        _final_kernel,
        grid=(BATCH, SEQ // TILE_O),
        in_specs=[spec, spec, pl.BlockSpec((1, D_MODEL), lambda b, i: (0, 0))],
        out_specs=spec,
        out_shape=jax.ShapeDtypeStruct((BATCH, SEQ, D_MODEL), F32),
        compiler_params=pltpu.CompilerParams(
            dimension_semantics=("parallel", "parallel"), vmem_limit_bytes=VMEM_LIMIT),
        name="final",
    )(x1, moe, g)


def kernel(x, norm1_g, w_in, b_gate, gmlp_ln_g, gmlp_ln_b, gmlp_w_s, gmlp_b_s, s5_lam_re, s5_lam_im, s5_log_dt,
           s5_b_re, s5_b_im, s5_c_re, s5_c_im, s5_d, s5_w_glu, s5_b_glu, w_up_a, w_up_b, w_out, norm2_g,
           w_router, w_gate, w_up, w_down, final_g):
    assert x.shape == (BATCH, SEQ, D_MODEL) and w_in.shape[0] == 1
    l = 0
    row = lambda a: a.astype(F32).reshape(1, -1)

    w_in_b = w_in[l].astype(BF16)
    m_mat, wsum, wc, a_re, a_im = _s5_tables(s5_lam_re[l], s5_lam_im[l], s5_log_dt[l], s5_b_re[l], s5_b_im[l],
                                             s5_c_re[l], s5_c_im[l], s5_d[l])
    g1 = row(norm1_g[l])

    z, s_re, s_im = _s5_in(x, g1, w_in_b, wsum)
    shape3 = (S5_NCH, S5_GROUPS, BATCH * LANES)
    xc_re, xc_im = _s5_scan(s_re.reshape(shape3), s_im.reshape(shape3),
                            jnp.tile(a_re, (1, BATCH)), jnp.tile(a_im, (1, BATCH)))
    shape2 = (S5_NCH * S5_GROUPS, BATCH * LANES)
    y = _s5_out(z, xc_re.reshape(shape2), xc_im.reshape(shape2), m_mat, wc)

    consts = (
        g1, w_in_b, row(b_gate[l][:D_MODEL]), row(b_gate[l][D_MODEL:]), row(gmlp_ln_g[l]), row(gmlp_ln_b[l]),
        gmlp_w_s[l].astype(BF16), jnp.repeat(jnp.transpose(gmlp_b_s[l].astype(F32)), GMLP_HEAD, axis=1),
        s5_w_glu[l].astype(BF16), row(s5_b_glu[l]), w_up_a[l].astype(BF16), w_up_b[l].astype(BF16),
        w_out[l].astype(BF16), row(norm2_g[l]), jnp.transpose(w_router[l].astype(F32)),
    )
    x1, h2p, probs = _mixer(x, y, consts)

    tri = (jnp.arange(LANES)[:, None] <= jnp.arange(LANES)[None, :]).astype(BF16)
    idx, gate = _topk(probs.reshape(BATCH * N_EXPERTS, SEQ), tri)
    moe = _moe(idx, gate, h2p, w_gate[l].astype(BF16), w_up[l].astype(BF16), w_down[l].astype(BF16))
    return _final(x1, moe, row(final_g))
```

```python
import jax
import jax.numpy as jnp
import numpy as np
from jax import lax
from jax.experimental import pallas as pl
from jax.experimental.pallas import tpu as pltpu

F32 = jnp.float32
BF16 = jnp.bfloat16
I32 = jnp.int32
U32 = jnp.uint32

D_MODEL = 1024
BATCH = 16
SEQ = 4096
GMLP_WIDTH = 512
GMLP_GROUPS = 8
GMLP_HEAD = GMLP_WIDTH // GMLP_GROUPS
CHUNK = 128
S5_WIDTH = 256
S5_GROUP_CH = 16
S5_GROUPS = S5_WIDTH // S5_GROUP_CH
S5_STATE = 64
N_EXPERTS = 16
CAPACITY = 2 * SEQ // N_EXPERTS
EXPERT_FF = 2048
EPS = 1e-6
O_S5 = 2 * GMLP_WIDTH
O_GATE = O_S5 + S5_WIDTH

LANES = 128
S5_TC = 16
S5_NCH = SEQ // S5_TC
TILE_A = 1024
TILE_M = 1024
MIX_COLS = 256
TILE_F = 1024
TILE_X = 256
SCAN_LANES = 256
VMEM_LIMIT = 56 * 1024 * 1024

_NT = (((1,), (1,)), ((), ()))


def _gelu(x):
    return 0.5 * x * (1.0 + lax.erf(x * (2.0 ** -0.5)))


def _rms(x, g):
    return x * lax.rsqrt(jnp.mean(x * x, axis=-1, keepdims=True) + EPS) * g


def _lane_block(shape):
    return lax.broadcasted_iota(I32, shape, 1) // S5_GROUP_CH


def _cpow(er, ei, k):
    mag = jnp.exp(k * er)
    return mag * jnp.cos(k * ei), mag * jnp.sin(k * ei)


def _cmul(a, b):
    return a[0] * b[0] - a[1] * b[1], a[0] * b[1] + a[1] * b[0]


def _s5_tables(lam_re, lam_im, log_dt, b_re, b_im, c_re, c_im, d_skip):
    hp = lax.Precision.HIGHEST
    tc, ng, nh = S5_TC, S5_GROUPS, S5_GROUP_CH
    lr, li = lam_re.astype(F32), lam_im.astype(F32)
    dt = jnp.exp(log_dt.astype(F32))[..., None]
    er, ei = lr * dt, li * dt
    lb = _cpow(er, ei, 1.0)
    den = lr * lr + li * li
    q = (((lb[0] - 1.0) * lr + lb[1] * li) / den, (lb[1] * lr - (lb[0] - 1.0) * li) / den)
    bbar = _cmul((q[0][..., None], q[1][..., None]), (b_re.astype(F32), b_im.astype(F32)))
    cm = (c_re.astype(F32), c_im.astype(F32))

    kk = np.arange(tc, dtype=np.float32)[:, None, None, None]
    pw = _cpow(er[None], ei[None], kk)
    t = _cmul((cm[0][None], cm[1][None]), (pw[0][:, :, :, None, :], pw[1][:, :, :, None, :]))
    taps = (jnp.einsum('kdghp,dgpi->kdghi', t[0], bbar[0], precision=hp)
            - jnp.einsum('kdghp,dgpi->kdghi', t[1], bbar[1], precision=hp))

    tau = (np.arange(tc)[None, :] - np.arange(ng)[:, None]) % tc
    diff = tau[:, None, :] - tau[:, :, None]
    oh_f = (diff[..., None] == np.arange(tc)).astype(np.float32)
    oh_r = (-diff[..., None] == np.arange(tc)).astype(np.float32)
    m = (jnp.einsum('gabk,kghi->gaibh', oh_f, taps[:, 0], precision=hp)
         + jnp.einsum('gabk,kghi->gaibh', oh_r, taps[:, 1], precision=hp))
    eye_j = np.eye(tc, dtype=np.float32)[None, :, None, :, None]
    eye_h = np.eye(nh, dtype=np.float32)[None, None, :, None, :]
    m = (m + eye_j * eye_h * d_skip.astype(F32)[:, None, None, None, :]).reshape(ng, tc * nh, tc * nh)

    tauf = tau.astype(np.float32)
    bt = (jnp.transpose(bbar[0], (0, 1, 3, 2)), jnp.transpose(bbar[1], (0, 1, 3, 2)))
    parts = []
    for d, kin in ((0, tc - 1.0 - tauf), (1, tauf)):
        p_in = _cpow(er[d][:, None, None, :], ei[d][:, None, None, :], kin[:, :, None, None])
        parts.append(_cmul(p_in, (bt[0][d][:, None], bt[1][d][:, None])))
    wsum = jnp.concatenate([parts[0][0], parts[1][0], parts[0][1], parts[1][1]], axis=-1)
    wsum = wsum.reshape(ng, tc * nh, 4 * S5_STATE)

    ct = (jnp.transpose(cm[0], (0, 1, 3, 2)), jnp.transpose(cm[1], (0, 1, 3, 2)))
    parts = []
    for d, kout in ((0, tauf + 1.0), (1, tc - tauf)):
        p_out = _cpow(er[d][:, :, None, None], ei[d][:, :, None, None], kout[:, None, :, None])
        parts.append(_cmul(p_out, (ct[0][d][:, :, None, :], ct[1][d][:, :, None, :])))
    wc = jnp.concatenate([parts[0][0], parts[1][0], -parts[0][1], -parts[1][1]], axis=1)
    wc = wc.reshape(ng, 4 * S5_STATE, tc * nh)

    a16 = _cpow(er, ei, float(tc))
    a_re = jnp.concatenate([a16[0][0], a16[0][1]], axis=-1)
    a_im = jnp.concatenate([a16[1][0], a16[1][1]], axis=-1)
    return m.astype(BF16), wsum.astype(BF16), wc.astype(BF16), a_re, a_im


def _s5_in_kernel(x_ref, g1_ref, ws5_ref, wsum_ref, z_ref, sre_ref, sim_ref, s_scr):
    n = TILE_A // S5_TC
    hn = _rms(x_ref[0], g1_ref[...]).astype(BF16)
    s = jnp.dot(hn, ws5_ref[...], preferred_element_type=F32)
    s_scr[0] = s[:, :LANES]
    s_scr[1] = s[:, LANES:]
    rows = []
    for t in range(S5_TC):
        a = jnp.concatenate([s_scr[0, pl.ds(t, n, stride=S5_TC), :],
                             s_scr[1, pl.ds(t, n, stride=S5_TC), :]], axis=1)
        rows.append(pltpu.roll(a, S5_GROUP_CH * t, 1) if t else a)
    blk = _lane_block((n, S5_WIDTH))
    for g in range(S5_GROUPS):
        tsel = (blk - g) & (S5_TC - 1)
        z = rows[0]
        for t in range(1, S5_TC):
            z = jnp.where(tsel == t, rows[t], z)
        zb = z.astype(BF16)
        z_ref[g] = zb
        sm = jnp.dot(zb, wsum_ref[g], preferred_element_type=F32)
        sre_ref[pl.ds(g, n, stride=S5_GROUPS), :] = sm[:, :LANES]
        sim_ref[pl.ds(g, n, stride=S5_GROUPS), :] = sm[:, LANES:]


def _s5_in(x, g1, ws5, wsum):
    nt = SEQ // TILE_A
    n = TILE_A // S5_TC
    return pl.pallas_call(
        _s5_in_kernel,
        grid=(BATCH, nt),
        in_specs=[
            pl.BlockSpec((1, TILE_A, D_MODEL), lambda b, i: (b, i, 0)),
            pl.BlockSpec((1, D_MODEL), lambda b, i: (0, 0)),
            pl.BlockSpec((D_MODEL, S5_WIDTH), lambda b, i: (0, O_S5 // S5_WIDTH)),
            pl.BlockSpec((S5_GROUPS, S5_WIDTH, S5_WIDTH), lambda b, i: (0, 0, 0)),
        ],
        out_specs=[
            pl.BlockSpec((S5_GROUPS, n, S5_WIDTH), lambda b, i: (0, b * nt + i, 0)),
            pl.BlockSpec((n * S5_GROUPS, LANES), lambda b, i: (i, b)),
            pl.BlockSpec((n * S5_GROUPS, LANES), lambda b, i: (i, b)),
        ],
        out_shape=[
            jax.ShapeDtypeStruct((S5_GROUPS, BATCH * S5_NCH, S5_WIDTH), BF16),
            jax.ShapeDtypeStruct((S5_NCH * S5_GROUPS, BATCH * LANES), F32),
            jax.ShapeDtypeStruct((S5_NCH * S5_GROUPS, BATCH * LANES), F32),
        ],
        scratch_shapes=[pltpu.VMEM((2, TILE_A, LANES), F32)],
        compiler_params=pltpu.CompilerParams(
            dimension_semantics=("parallel", "parallel"), vmem_limit_bytes=VMEM_LIMIT),
        name="s5_in",
    )(x, g1, ws5, wsum)


def _s5_scan_kernel(sre_ref, sim_ref, are_ref, aim_ref, xre_ref, xim_ref):
    ar = are_ref[...]
    ai = aim_ref[...]
    fwd = (lax.broadcasted_iota(I32, ar.shape, 1) % LANES) < S5_STATE
    zero = jnp.zeros_like(ar)

    def step(c, xr, xi):
        sr = sre_ref[c]
        si = sim_ref[c]
        return ar * xr - ai * xi + sr, ar * xi + ai * xr + si

    def fwd_body(c, carry):
        xr, xi = carry
        xre_ref[c] = xr
        xim_ref[c] = xi
        return step(c, xr, xi)

    lax.fori_loop(0, S5_NCH, fwd_body, (zero, zero))

    def rev_body(i, carry):
        xr, xi = carry
        c = S5_NCH - 1 - i
        xre_ref[c] = jnp.where(fwd, xre_ref[c], xr)
        xim_ref[c] = jnp.where(fwd, xim_ref[c], xi)
        return step(c, xr, xi)

    lax.fori_loop(0, S5_NCH, rev_body, (zero, zero))


def _s5_scan(s_re, s_im, a_re, a_im):
    nl = BATCH * LANES
    blk3 = pl.BlockSpec((S5_NCH, S5_GROUPS, SCAN_LANES), lambda j: (0, 0, j))
    blk2 = pl.BlockSpec((S5_GROUPS, SCAN_LANES), lambda j: (0, j))
    return pl.pallas_call(
        _s5_scan_kernel,
        grid=(nl // SCAN_LANES,),
        in_specs=[blk3, blk3, blk2, blk2],
        out_specs=[blk3, blk3],
        out_shape=[jax.ShapeDtypeStruct((S5_NCH, S5_GROUPS, nl), F32)] * 2,
        compiler_params=pltpu.CompilerParams(
            dimension_semantics=("parallel",), vmem_limit_bytes=VMEM_LIMIT),
        name="s5_scan",
    )(s_re, s_im, a_re, a_im)


def _s5_out_kernel(z_ref, xre_ref, xim_ref, m_ref, wc_ref, y_ref, yg_scr):
    n = S5_NCH
    for g in range(S5_GROUPS):
        xc = jnp.concatenate([xre_ref[pl.ds(g, n, stride=S5_GROUPS), :],
                              xim_ref[pl.ds(g, n, stride=S5_GROUPS), :]], axis=1).astype(BF16)
        yg_scr[g] = (jnp.dot(z_ref[g], m_ref[g], preferred_element_type=F32)
                     + jnp.dot(xc, wc_ref[g], preferred_element_type=F32))
    blk = _lane_block((n, S5_WIDTH))
    for t in range(S5_TC):
        gsel = (blk - t) & (S5_GROUPS - 1)
        acc = yg_scr[0]
        for g in range(1, S5_GROUPS):
            acc = jnp.where(gsel == g, yg_scr[g], acc)
        if t:
            acc = pltpu.roll(acc, S5_WIDTH - S5_GROUP_CH * t, 1)
        y_ref[0, 0, pl.ds(t, n, stride=S5_TC), :] = acc[:, :LANES]
        y_ref[0, 1, pl.ds(t, n, stride=S5_TC), :] = acc[:, LANES:]


def _s5_out(z, xc_re, xc_im, m, wc):
    wspec = pl.BlockSpec((S5_GROUPS, S5_WIDTH, S5_WIDTH), lambda b: (0, 0, 0))
    xspec = pl.BlockSpec((S5_NCH * S5_GROUPS, LANES), lambda b: (0, b))
    return pl.pallas_call(
        _s5_out_kernel,
        grid=(BATCH,),
        in_specs=[pl.BlockSpec((S5_GROUPS, S5_NCH, S5_WIDTH), lambda b: (0, b, 0)), xspec, xspec, wspec, wspec],
        out_specs=pl.BlockSpec((1, 2, SEQ, LANES), lambda b: (b, 0, 0, 0)),
        out_shape=jax.ShapeDtypeStruct((BATCH, 2, SEQ, LANES), F32),
        scratch_shapes=[pltpu.VMEM((S5_GROUPS, S5_NCH, S5_WIDTH), F32)],
        compiler_params=pltpu.CompilerParams(
            dimension_semantics=("parallel",), vmem_limit_bytes=VMEM_LIMIT),
        name="s5_out",
    )(z, xc_re, xc_im, m, wc)


def _mixer_kernel(x_ref, y_ref, g1_ref, win_ref, bga_ref, bgb_ref, lng_ref, lnb_ref,
                  ws_ref, bs_ref, wglu_ref, bglu_ref, wupa_ref, wupb_ref, wout_ref, g2_ref, wr_ref,
                  x1_ref, h2p_ref, pr_ref, bra_scr, mrg_scr):
    x = x_ref[0]
    hn = _rms(x, g1_ref[...]).astype(BF16)

    def in_proj(c0):
        return jnp.dot(hn, win_ref[:, c0:c0 + MIX_COLS], preferred_element_type=F32)

    nuv = GMLP_WIDTH // MIX_COLS
    us = [_gelu(in_proj(c * MIX_COLS)) for c in range(nuv)]
    vs = [_gelu(in_proj(GMLP_WIDTH + c * MIX_COLS)) for c in range(nuv)]
    mu = sum(jnp.sum(v, axis=-1, keepdims=True) for v in vs) * (1.0 / GMLP_WIDTH)
    vcs = [v - mu for v in vs]
    var = sum(jnp.sum(vc * vc, axis=-1, keepdims=True) for vc in vcs) * (1.0 / GMLP_WIDTH)
    rstd = lax.rsqrt(var + EPS)
    vns = [(vc * rstd * lng_ref[:, c * MIX_COLS:(c + 1) * MIX_COLS]
            + lnb_ref[:, c * MIX_COLS:(c + 1) * MIX_COLS]).astype(BF16) for c, vc in enumerate(vcs)]
    low = lax.broadcasted_iota(I32, (CHUNK, LANES), 1) < GMLP_HEAD
    per = MIX_COLS // LANES
    for ch in range(TILE_M // CHUNK):
        r0 = ch * CHUNK
        for q in range(GMLP_WIDTH // LANES):
            c0 = q * LANES
            l0 = (q % per) * LANES
            vb = vns[q // per][r0:r0 + CHUNK, l0:l0 + LANES]
            z0 = jnp.dot(ws_ref[2 * q], vb, preferred_element_type=F32)
            z1 = jnp.dot(ws_ref[2 * q + 1], vb, preferred_element_type=F32)
            zz = jnp.where(low, z0, z1) + bs_ref[:, c0:c0 + LANES]
            bra_scr[r0:r0 + CHUNK, c0:c0 + LANES] = (us[q // per][r0:r0 + CHUNK, l0:l0 + LANES] * zz).astype(BF16)

    yb = _gelu(jnp.concatenate([y_ref[0, 0], y_ref[0, 1]], axis=1))
    gl = jnp.dot(yb.astype(BF16), wglu_ref[...], preferred_element_type=F32) + bglu_ref[...]
    brb = (yb * jax.nn.sigmoid(gl)).astype(BF16)

    bra = bra_scr[...]
    for c in range(D_MODEL // MIX_COLS):
        cs = slice(c * MIX_COLS, (c + 1) * MIX_COLS)
        ga = jax.nn.sigmoid(in_proj(O_GATE + c * MIX_COLS) + bga_ref[:, cs])
        gb = jax.nn.sigmoid(in_proj(O_GATE + D_MODEL + c * MIX_COLS) + bgb_ref[:, cs])
        mrg_scr[:, cs] = (ga * jnp.dot(bra, wupa_ref[:, cs], preferred_element_type=F32)
                          + gb * jnp.dot(brb, wupb_ref[:, cs], preferred_element_type=F32)).astype(BF16)
    mrg = mrg_scr[...]
    ssq = jnp.zeros((TILE_M, 1), F32)
    for c in range(D_MODEL // MIX_COLS):
        cs = slice(c * MIX_COLS, (c + 1) * MIX_COLS)
        x1c = x[:, cs] + jnp.dot(mrg, wout_ref[:, cs], preferred_element_type=F32)
        x1_ref[0, :, cs] = x1c
        ssq = ssq + jnp.sum(x1c * x1c, axis=-1, keepdims=True)

    h2 = x1_ref[0] * lax.rsqrt(ssq * (1.0 / D_MODEL) + EPS) * g2_ref[...]
    half = D_MODEL // 2
    h2p_ref[0] = pltpu.pack_elementwise([h2[:, :half], h2[:, half:]], packed_dtype=BF16)

    logits = lax.dot_general(wr_ref[...], h2, _NT, precision=lax.Precision.HIGHEST,
                             preferred_element_type=F32)
    mx = jnp.max(logits, axis=0, keepdims=True)
    ex = jnp.exp(logits - mx)
    pr_ref[0] = ex / jnp.sum(ex, axis=0, keepdims=True)


def _mixer(x, y, consts):
    nt = SEQ // TILE_M

    def cspec(a):
        nd = a.ndim
        return pl.BlockSpec(a.shape, lambda b, i: (0,) * nd, pipeline_mode=pl.Buffered(1))

    return pl.pallas_call(
        _mixer_kernel,
        grid=(BATCH, nt),
        in_specs=[pl.BlockSpec((1, TILE_M, D_MODEL), lambda b, i: (b, i, 0)),
                  pl.BlockSpec((1, 2, TILE_M, LANES), lambda b, i: (b, 0, i, 0))] + [cspec(a) for a in consts],
        out_specs=[pl.BlockSpec((1, TILE_M, D_MODEL), lambda b, i: (b, i, 0)),
                   pl.BlockSpec((1, TILE_M, D_MODEL // 2), lambda b, i: (b, i, 0)),
                   pl.BlockSpec((1, N_EXPERTS, TILE_M), lambda b, i: (b, 0, i))],
        out_shape=[jax.ShapeDtypeStruct((BATCH, SEQ, D_MODEL), F32),
                   jax.ShapeDtypeStruct((BATCH, SEQ, D_MODEL // 2), U32),
                   jax.ShapeDtypeStruct((BATCH, N_EXPERTS, SEQ), F32)],
        scratch_shapes=[pltpu.VMEM((TILE_M, GMLP_WIDTH), BF16), pltpu.VMEM((TILE_M, D_MODEL), BF16)],
        compiler_params=pltpu.CompilerParams(
            dimension_semantics=("parallel", "parallel"), vmem_limit_bytes=VMEM_LIMIT),
        name="mixer",
    )(x, y, *consts)


TOPK_NBLK = SEQ // LANES
TOPK_ROWS = 4 * N_EXPERTS
TOPK_SEARCH_STEPS = 28


def _lane_sum(x):
    return jnp.sum(x, axis=1, keepdims=True)


def _topk_kernel(p_ref, tri_ref, idx_ref, gate_ref, eqc_scr, loc_scr, pj_scr, cb_scr, cbp_scr):
    p = p_ref[...]

    def count_ge(th):
        return _lane_sum(jnp.where(p >= th, 1.0, 0.0))

    def search(_, c):
        lo, hi = c
        w = hi - lo
        m1, m2, m3 = lo + 0.25 * w, lo + 0.5 * w, lo + 0.75 * w
        c1, c2, c3 = count_ge(m1) >= CAPACITY, count_ge(m2) >= CAPACITY, count_ge(m3) >= CAPACITY
        return (jnp.where(c3, m3, jnp.where(c2, m2, jnp.where(c1, m1, lo))),
                jnp.where(c1, jnp.where(c2, jnp.where(c3, hi, m3), m2), m1))

    lo, hi = lax.fori_loop(0, TOPK_SEARCH_STEPS, search,
                           (jnp.zeros((TOPK_ROWS, 1), F32), jnp.full((TOPK_ROWS, 1), 2.0, F32)))
    gt = p >= hi
    eq = (p >= lo) & (p < hi)
    need = CAPACITY - _lane_sum(jnp.where(gt, 1.0, 0.0))
    tri = tri_ref[...]
    eqf = jnp.where(eq, 1.0, 0.0)
    carry = jnp.zeros((TOPK_ROWS, 1), F32)
    for j in range(TOPK_NBLK):
        sl = slice(j * LANES, (j + 1) * LANES)
        cs = jnp.dot(eqf[:, sl].astype(BF16), tri, preferred_element_type=F32) + carry
        eqc_scr[:, sl] = cs
        carry = cs[:, LANES - 1:LANES]
    sel = gt | (eq & ((eqc_scr[...] - eqf) < need))
    self = jnp.where(sel, 1.0, 0.0)

    lane = lax.broadcasted_iota(I32, (TOPK_ROWS, LANES), 1)
    cb = jnp.full((TOPK_ROWS, LANES), float(2 * SEQ), F32)
    cbp = cb
    carry = jnp.zeros((TOPK_ROWS, 1), F32)
    for j in range(TOPK_NBLK):
        sl = slice(j * LANES, (j + 1) * LANES)
        loc = jnp.dot(self[:, sl].astype(BF16), tri, preferred_element_type=F32)
        loc_scr[j * TOPK_ROWS:(j + 1) * TOPK_ROWS, :] = loc
        pj_scr[j * TOPK_ROWS:(j + 1) * TOPK_ROWS, :] = p[:, sl]
        cbp = jnp.where(lane == j, carry, cbp)
        carry = carry + loc[:, LANES - 1:LANES]
        cb = jnp.where(lane == j, carry, cb)
    cb_scr[...] = cb
    cbp_scr[...] = cbp

    slot = (lax.broadcasted_iota(I32, (CAPACITY, 1), 0) + 1).astype(F32)
    lane_k = lax.broadcasted_iota(I32, (CAPACITY, LANES), 1)
    lane_kf = lane_k.astype(F32)
    pad = jnp.zeros((LANES - TOPK_NBLK, LANES), F32)

    def block_rows(ref, e):
        return jnp.concatenate([ref[pl.ds(e, TOPK_NBLK, stride=TOPK_ROWS), :], pad], axis=0)

    def per_expert(e, _):
        cb_e = cb_scr[pl.ds(e, 1), :]
        cbp_e = cbp_scr[pl.ds(e, 1), :]
        hit = (cbp_e < slot) & (slot <= cb_e)
        hitb = jnp.where(hit, 1.0, 0.0).astype(BF16)
        both = _lane_sum(jnp.where(hit, cbp_e * float(LANES) + lane_kf, 0.0))
        before = jnp.floor(both * (1.0 / LANES))
        jblk = both - before * float(LANES)
        rank = slot - before
        loc = jnp.dot(hitb, block_rows(loc_scr, e).astype(BF16), preferred_element_type=F32)
        off = _lane_sum(jnp.where(loc < rank, 1.0, 0.0))
        pe = block_rows(pj_scr, e)
        hi = pe.astype(BF16)
        r1 = pe - hi.astype(F32)
        mid = r1.astype(BF16)
        lo = (r1 - mid.astype(F32)).astype(BF16)
        pg = (jnp.dot(hitb, hi, preferred_element_type=F32) + jnp.dot(hitb, mid, preferred_element_type=F32)
              + jnp.dot(hitb, lo, preferred_element_type=F32))
        gate = _lane_sum(jnp.where(lane_kf == off, pg, 0.0))
        tok = jblk * float(LANES) + off
        rt = jnp.where(lane_k == 0, tok, jnp.where(lane_k == 1, gate, 0.0)).T
        idx_ref[pl.ds(e, 1), :] = rt[0:1].astype(I32)
        gate_ref[pl.ds(e, 1), :] = rt[1:2]
        return 0

    lax.fori_loop(0, TOPK_ROWS, per_expert, 0)


def _topk(probs, tri):
    nrow = BATCH * N_EXPERTS
    return pl.pallas_call(
        _topk_kernel,
        grid=(nrow // TOPK_ROWS,),
        in_specs=[pl.BlockSpec((TOPK_ROWS, SEQ), lambda i: (i, 0)),
                  pl.BlockSpec((LANES, LANES), lambda i: (0, 0))],
        out_specs=[pl.BlockSpec((TOPK_ROWS, CAPACITY), lambda i: (i, 0)),
                   pl.BlockSpec((TOPK_ROWS, CAPACITY), lambda i: (i, 0))],
        out_shape=[jax.ShapeDtypeStruct((nrow, CAPACITY), I32),
                   jax.ShapeDtypeStruct((nrow, CAPACITY), F32)],
        scratch_shapes=[pltpu.VMEM((TOPK_ROWS, SEQ), F32),
                        pltpu.VMEM((TOPK_NBLK * TOPK_ROWS, LANES), F32),
                        pltpu.VMEM((TOPK_NBLK * TOPK_ROWS, LANES), F32),
                        pltpu.VMEM((TOPK_ROWS, LANES), F32),
                        pltpu.VMEM((TOPK_ROWS, LANES), F32)],
        compiler_params=pltpu.CompilerParams(
            dimension_semantics=("parallel",), vmem_limit_bytes=VMEM_LIMIT),
        name="topk",
    )(probs, tri)


MOE_GROUP = 8
MOE_FCHUNK = 512


def _gather_rows(idx_ref, ibase, h2p_ref, dst_ref, dbase, nrows):
    for k in range(nrows):
        r = idx_ref[0, 0, ibase + k]
        dst_ref[pl.ds(dbase + k, 1), :] = h2p_ref[0, pl.ds(r, 1), :]


def _scatter_rows(idx_ref, gate_ref, ibase, scale, y_ref, ybase, out_ref, nrows):
    for k0 in range(0, nrows, MOE_GROUP):
        rs, vals = [], []
        for k in range(k0, k0 + MOE_GROUP):
            r = idx_ref[0, 0, ibase + k]
            g = gate_ref[0, 0, ibase + k] * scale
            rs.append(r)
            vals.append(out_ref[0, pl.ds(r, 1), :] + g * y_ref[pl.ds(ybase + k, 1), :])
        for r, v in zip(rs, vals):
            out_ref[0, pl.ds(r, 1), :] = v


def _moe_kernel(idx_ref, gate_ref, h2p_ref, x1_ref, gf_ref, wg_ref, wu_ref, wd_ref, out_ref, xg_scr, xs_scr, y_scr):
    e = pl.program_id(1)
    f = pl.program_id(2)
    nf = EXPERT_FF // TILE_F
    step = e * nf + f
    rq = CAPACITY // nf
    half = D_MODEL // 2
    cur = e % 2
    y_cur, y_nxt = y_scr.at[cur], y_scr.at[1 - cur]

    @pl.when(step == 0)
    def _():
        out_ref[...] = jnp.zeros_like(out_ref)
        y_scr[...] = jnp.zeros_like(y_scr)

        def gather(i, _):
            base = pl.multiple_of(i * 32, 32)
            _gather_rows(idx_ref, base, h2p_ref, xg_scr, base, 32)
            return 0

        lax.fori_loop(0, CAPACITY // 32, gather, 0)

    @pl.when(step < SEQ // TILE_X)
    def _():
        rows = pl.ds(pl.multiple_of(step * TILE_X, TILE_X), TILE_X)
        out_ref[0, rows, :] = out_ref[0, rows, :] + x1_ref[0]

    @pl.when(f == 0)
    def _():
        w = xg_scr[...]
        for i in range(2):
            xs_scr[:, i * half:(i + 1) * half] = pltpu.unpack_elementwise(
                w, index=i, packed_dtype=BF16, unpacked_dtype=F32).astype(BF16)

    qbase = pl.multiple_of(f * rq, rq)
    e_prev = (e + N_EXPERTS - 1) % N_EXPERTS
    e_next = (e + 1) % N_EXPERTS
    scale = jnp.where(e >= 1, 1.0, 0.0).astype(F32)
    _scatter_rows(idx_ref, gate_ref, e_prev * CAPACITY + qbase, scale, y_nxt, qbase, out_ref, rq)
    _gather_rows(idx_ref, e_next * CAPACITY + qbase, h2p_ref, xg_scr, qbase, rq)

    xs = xs_scr[...]
    y = jnp.where(f > 0, y_cur[...], 0.0)
    for c0 in range(0, TILE_F, MOE_FCHUNK):
        cs = slice(c0, c0 + MOE_FCHUNK)
        a = jnp.dot(xs, wg_ref[0, :, cs], preferred_element_type=F32)
        u = jnp.dot(xs, wu_ref[0, :, cs], preferred_element_type=F32)
        hm = (a * jax.nn.sigmoid(a) * u).astype(BF16)
        y = y + jnp.dot(hm, wd_ref[0, cs, :], preferred_element_type=F32)
    y_cur[...] = y

    @pl.when((e == N_EXPERTS - 1) & (f == nf - 1))
    def _():
        def scatter(i, _):
            base = pl.multiple_of(i * 32, 32)
            _scatter_rows(idx_ref, gate_ref, e * CAPACITY + base, jnp.float32(1.0), y_cur, base, out_ref, 32)
            return 0

        lax.fori_loop(0, CAPACITY // 32, scatter, 0)

        def norm(i, _):
            rows = pl.ds(pl.multiple_of(i * TILE_X, TILE_X), TILE_X)
            out_ref[0, rows, :] = _rms(out_ref[0, rows, :], gf_ref[...])
            return 0

        lax.fori_loop(0, SEQ // TILE_X, norm, 0)


def _moe(idx, gate, h2p, x1, gf, wg, wu, wd):
    nf = EXPERT_FF // TILE_F
    nx = SEQ // TILE_X
    assert nx <= N_EXPERTS * nf
    sspec = pl.BlockSpec((1, 1, N_EXPERTS * CAPACITY), lambda b, e, f: (b, 0, 0), memory_space=pltpu.SMEM)
    return pl.pallas_call(
        _moe_kernel,
        grid=(BATCH, N_EXPERTS, nf),
        in_specs=[sspec, sspec,
                  pl.BlockSpec((1, SEQ, D_MODEL // 2), lambda b, e, f: (b, 0, 0), pipeline_mode=pl.Buffered(1)),
                  pl.BlockSpec((1, TILE_X, D_MODEL), lambda b, e, f: (b, jnp.minimum(e * nf + f, nx - 1), 0)),
                  pl.BlockSpec((1, D_MODEL), lambda b, e, f: (0, 0)),
                  pl.BlockSpec((1, D_MODEL, TILE_F), lambda b, e, f: (e, 0, f)),
                  pl.BlockSpec((1, D_MODEL, TILE_F), lambda b, e, f: (e, 0, f)),
                  pl.BlockSpec((1, TILE_F, D_MODEL), lambda b, e, f: (e, f, 0))],
        out_specs=pl.BlockSpec((1, SEQ, D_MODEL), lambda b, e, f: (b, 0, 0), pipeline_mode=pl.Buffered(1)),
        out_shape=jax.ShapeDtypeStruct((BATCH, SEQ, D_MODEL), F32),
        scratch_shapes=[pltpu.VMEM((CAPACITY, D_MODEL // 2), U32),
                        pltpu.VMEM((CAPACITY, D_MODEL), BF16),
                        pltpu.VMEM((2, CAPACITY, D_MODEL), F32)],
        compiler_params=pltpu.CompilerParams(
            dimension_semantics=("parallel", "arbitrary", "arbitrary"), vmem_limit_bytes=VMEM_LIMIT),
        name="moe",
    )(idx.reshape(BATCH, 1, N_EXPERTS * CAPACITY), gate.reshape(BATCH, 1, N_EXPERTS * CAPACITY), h2p, x1, gf,
      wg, wu, wd)


def kernel(x, norm1_g, w_in, b_gate, gmlp_ln_g, gmlp_ln_b, gmlp_w_s, gmlp_b_s, s5_lam_re, s5_lam_im, s5_log_dt,
           s5_b_re, s5_b_im, s5_c_re, s5_c_im, s5_d, s5_w_glu, s5_b_glu, w_up_a, w_up_b, w_out, norm2_g,
           w_router, w_gate, w_up, w_down, final_g):
    assert x.shape == (BATCH, SEQ, D_MODEL) and w_in.shape[0] == 1
    l = 0
    row = lambda a: a.astype(F32).reshape(1, -1)

    w_in_b = w_in[l].astype(BF16)
    m_mat, wsum, wc, a_re, a_im = _s5_tables(s5_lam_re[l], s5_lam_im[l], s5_log_dt[l], s5_b_re[l], s5_b_im[l],
                                             s5_c_re[l], s5_c_im[l], s5_d[l])
    g1 = row(norm1_g[l])

    z, s_re, s_im = _s5_in(x, g1, w_in_b, wsum)
    shape3 = (S5_NCH, S5_GROUPS, BATCH * LANES)
    xc_re, xc_im = _s5_scan(s_re.reshape(shape3), s_im.reshape(shape3),
                            jnp.tile(a_re, (1, BATCH)), jnp.tile(a_im, (1, BATCH)))
    shape2 = (S5_NCH * S5_GROUPS, BATCH * LANES)
    y = _s5_out(z, xc_re.reshape(shape2), xc_im.reshape(shape2), m_mat, wc)

    consts = (
        g1, w_in_b, row(b_gate[l][:D_MODEL]), row(b_gate[l][D_MODEL:]), row(gmlp_ln_g[l]), row(gmlp_ln_b[l]),
        gmlp_w_s[l].astype(BF16), jnp.repeat(jnp.transpose(gmlp_b_s[l].astype(F32)), GMLP_HEAD, axis=1),
        s5_w_glu[l].astype(BF16), row(s5_b_glu[l]), w_up_a[l].astype(BF16), w_up_b[l].astype(BF16),
        w_out[l].astype(BF16), row(norm2_g[l]), jnp.transpose(w_router[l].astype(F32)),
    )
    x1, h2p, probs = _mixer(x, y, consts)

    tri = (jnp.arange(LANES)[:, None] <= jnp.arange(LANES)[None, :]).astype(BF16)
    idx, gate = _topk(probs.reshape(BATCH * N_EXPERTS, SEQ), tri)
    return _moe(idx, gate, h2p, x1, row(final_g),
                w_gate[l].astype(BF16), w_up[l].astype(BF16), w_down[l].astype(BF16))
```

```python
import jax
import jax.numpy as jnp
import numpy as np
from jax import lax
from jax.experimental import pallas as pl
from jax.experimental.pallas import tpu as pltpu

F32 = jnp.float32
BF16 = jnp.bfloat16
I32 = jnp.int32
U32 = jnp.uint32

D_MODEL = 1024
BATCH = 16
SEQ = 4096
GMLP_WIDTH = 512
GMLP_GROUPS = 8
GMLP_HEAD = GMLP_WIDTH // GMLP_GROUPS
CHUNK = 128
S5_WIDTH = 256
S5_GROUP_CH = 16
S5_GROUPS = S5_WIDTH // S5_GROUP_CH
S5_STATE = 64
N_EXPERTS = 16
CAPACITY = 2 * SEQ // N_EXPERTS
EXPERT_FF = 2048
EPS = 1e-6
O_S5 = 2 * GMLP_WIDTH
O_GATE = O_S5 + S5_WIDTH

LANES = 128
S5_TC = 16
S5_NCH = SEQ // S5_TC
TILE_A = 1024
TILE_M = 1024
MIX_COLS = 256
TILE_F = 1024
TILE_X = 256
SCAN_LANES = 256
VMEM_LIMIT = 56 * 1024 * 1024

_NT = (((1,), (1,)), ((), ()))


def _gelu(x):
    return 0.5 * x * (1.0 + lax.erf(x * (2.0 ** -0.5)))


def _rms(x, g):
    return x * lax.rsqrt(jnp.mean(x * x, axis=-1, keepdims=True) + EPS) * g


def _block_rotate(arrs):
    n = len(arrs)
    blk = lax.broadcasted_iota(I32, arrs[0].shape, 1) // S5_GROUP_CH
    w = [arrs[(-i) % n] for i in range(n)]
    for s in range(n.bit_length() - 1):
        bit = ((blk >> s) & 1) == 1
        w = [jnp.where(bit, w[(i - (1 << s)) % n], w[i]) for i in range(n)]
    return w


def _cpow(er, ei, k):
    mag = jnp.exp(k * er)
    return mag * jnp.cos(k * ei), mag * jnp.sin(k * ei)


def _cmul(a, b):
    return a[0] * b[0] - a[1] * b[1], a[0] * b[1] + a[1] * b[0]


def _s5_tables(lam_re, lam_im, log_dt, b_re, b_im, c_re, c_im, d_skip):
    hp = lax.Precision.HIGH
    tc, ng, nh = S5_TC, S5_GROUPS, S5_GROUP_CH
    lr, li = lam_re.astype(F32), lam_im.astype(F32)
    dt = jnp.exp(log_dt.astype(F32))[..., None]
    er, ei = lr * dt, li * dt
    lb = _cpow(er, ei, 1.0)
    den = lr * lr + li * li
    q = (((lb[0] - 1.0) * lr + lb[1] * li) / den, (lb[1] * lr - (lb[0] - 1.0) * li) / den)
    bbar = _cmul((q[0][..., None], q[1][..., None]), (b_re.astype(F32), b_im.astype(F32)))
    cm = (c_re.astype(F32), c_im.astype(F32))

    kk = np.arange(tc, dtype=np.float32)[:, None, None, None]
    pw = _cpow(er[None], ei[None], kk)
    t = _cmul((cm[0][None], cm[1][None]), (pw[0][:, :, :, None, :], pw[1][:, :, :, None, :]))
    taps = (jnp.einsum('kdghp,dgpi->kdghi', t[0], bbar[0], precision=hp)
            - jnp.einsum('kdghp,dgpi->kdghi', t[1], bbar[1], precision=hp))

    tau = (np.arange(tc)[None, :] - np.arange(ng)[:, None]) % tc
    diff = tau[:, None, :] - tau[:, :, None]
    oh_f = (diff[..., None] == np.arange(tc)).astype(np.float32)
    oh_r = (-diff[..., None] == np.arange(tc)).astype(np.float32)
    m = (jnp.einsum('gabk,kghi->gaibh', oh_f, taps[:, 0], precision=hp)
         + jnp.einsum('gabk,kghi->gaibh', oh_r, taps[:, 1], precision=hp))
    eye_j = np.eye(tc, dtype=np.float32)[None, :, None, :, None]
    eye_h = np.eye(nh, dtype=np.float32)[None, None, :, None, :]
    m = (m + eye_j * eye_h * d_skip.astype(F32)[:, None, None, None, :]).reshape(ng, tc * nh, tc * nh)

    tauf = tau.astype(np.float32)
    bt = (jnp.transpose(bbar[0], (0, 1, 3, 2)), jnp.transpose(bbar[1], (0, 1, 3, 2)))
    parts = []
    for d, kin in ((0, tc - 1.0 - tauf), (1, tauf)):
        p_in = _cpow(er[d][:, None, None, :], ei[d][:, None, None, :], kin[:, :, None, None])
        parts.append(_cmul(p_in, (bt[0][d][:, None], bt[1][d][:, None])))
    wsum = jnp.concatenate([parts[0][0], parts[1][0], parts[0][1], parts[1][1]], axis=-1)
    wsum = wsum.reshape(ng, tc * nh, 4 * S5_STATE)

    ct = (jnp.transpose(cm[0], (0, 1, 3, 2)), jnp.transpose(cm[1], (0, 1, 3, 2)))
    parts = []
    for d, kout in ((0, tauf + 1.0), (1, tc - tauf)):
        p_out = _cpow(er[d][:, :, None, None], ei[d][:, :, None, None], kout[:, None, :, None])
        parts.append(_cmul(p_out, (ct[0][d][:, :, None, :], ct[1][d][:, :, None, :])))
    wc = jnp.concatenate([parts[0][0], parts[1][0], -parts[0][1], -parts[1][1]], axis=1)
    wc = wc.reshape(ng, 4 * S5_STATE, tc * nh)

    a16 = _cpow(er, ei, float(tc))
    a_re = jnp.concatenate([a16[0][0], a16[0][1]], axis=-1)
    a_im = jnp.concatenate([a16[1][0], a16[1][1]], axis=-1)
    return m.astype(BF16), wsum.astype(BF16), wc.astype(BF16), a_re, a_im


def _s5_in_kernel(x_ref, g1_ref, ws5_ref, wsum_ref, z_ref, sre_ref, sim_ref, s_scr):
    n = TILE_A // S5_TC
    hn = _rms(x_ref[0], g1_ref[...]).astype(BF16)
    s = jnp.dot(hn, ws5_ref[...], preferred_element_type=F32)
    s_scr[0] = s[:, :LANES]
    s_scr[1] = s[:, LANES:]
    rows = []
    for t in range(S5_TC):
        a = jnp.concatenate([s_scr[0, pl.ds(t, n, stride=S5_TC), :],
                             s_scr[1, pl.ds(t, n, stride=S5_TC), :]], axis=1)
        rows.append(pltpu.roll(a, S5_GROUP_CH * t, 1) if t else a)
    zs = _block_rotate(rows)
    for g in range(S5_GROUPS):
        zb = zs[g].astype(BF16)
        z_ref[g] = zb
        sm = jnp.dot(zb, wsum_ref[g], preferred_element_type=F32)
        sre_ref[pl.ds(g, n, stride=S5_GROUPS), :] = sm[:, :LANES]
        sim_ref[pl.ds(g, n, stride=S5_GROUPS), :] = sm[:, LANES:]


def _s5_in(x, g1, ws5, wsum):
    nt = SEQ // TILE_A
    n = TILE_A // S5_TC
    return pl.pallas_call(
        _s5_in_kernel,
        grid=(BATCH, nt),
        in_specs=[
            pl.BlockSpec((1, TILE_A, D_MODEL), lambda b, i: (b, i, 0)),
            pl.BlockSpec((1, D_MODEL), lambda b, i: (0, 0)),
            pl.BlockSpec((D_MODEL, S5_WIDTH), lambda b, i: (0, O_S5 // S5_WIDTH)),
            pl.BlockSpec((S5_GROUPS, S5_WIDTH, S5_WIDTH), lambda b, i: (0, 0, 0)),
        ],
        out_specs=[
            pl.BlockSpec((S5_GROUPS, n, S5_WIDTH), lambda b, i: (0, b * nt + i, 0)),
            pl.BlockSpec((n * S5_GROUPS, LANES), lambda b, i: (i, b)),
            pl.BlockSpec((n * S5_GROUPS, LANES), lambda b, i: (i, b)),
        ],
        out_shape=[
            jax.ShapeDtypeStruct((S5_GROUPS, BATCH * S5_NCH, S5_WIDTH), BF16),
            jax.ShapeDtypeStruct((S5_NCH * S5_GROUPS, BATCH * LANES), F32),
            jax.ShapeDtypeStruct((S5_NCH * S5_GROUPS, BATCH * LANES), F32),
        ],
        scratch_shapes=[pltpu.VMEM((2, TILE_A, LANES), F32)],
        compiler_params=pltpu.CompilerParams(
            dimension_semantics=("parallel", "parallel"), vmem_limit_bytes=VMEM_LIMIT),
        name="s5_in",
    )(x, g1, ws5, wsum)


def _s5_scan_kernel(sre_ref, sim_ref, are_ref, aim_ref, xre_ref, xim_ref):
    ar = are_ref[...]
    ai = aim_ref[...]
    fwd = (lax.broadcasted_iota(I32, ar.shape, 1) % LANES) < S5_STATE
    zero = jnp.zeros_like(ar)

    def step(c, xr, xi):
        sr = sre_ref[c]
        si = sim_ref[c]
        return ar * xr - ai * xi + sr, ar * xi + ai * xr + si

    def fwd_body(c, carry):
        xr, xi = carry
        xre_ref[c] = xr
        xim_ref[c] = xi
        return step(c, xr, xi)

    lax.fori_loop(0, S5_NCH, fwd_body, (zero, zero))

    def rev_body(i, carry):
        xr, xi = carry
        c = S5_NCH - 1 - i
        xre_ref[c] = jnp.where(fwd, xre_ref[c], xr)
        xim_ref[c] = jnp.where(fwd, xim_ref[c], xi)
        return step(c, xr, xi)

    lax.fori_loop(0, S5_NCH, rev_body, (zero, zero))


def _s5_scan(s_re, s_im, a_re, a_im):
    nl = BATCH * LANES
    blk3 = pl.BlockSpec((S5_NCH, S5_GROUPS, SCAN_LANES), lambda j: (0, 0, j))
    blk2 = pl.BlockSpec((S5_GROUPS, SCAN_LANES), lambda j: (0, j))
    return pl.pallas_call(
        _s5_scan_kernel,
        grid=(nl // SCAN_LANES,),
        in_specs=[blk3, blk3, blk2, blk2],
        out_specs=[blk3, blk3],
        out_shape=[jax.ShapeDtypeStruct((S5_NCH, S5_GROUPS, nl), F32)] * 2,
        compiler_params=pltpu.CompilerParams(
            dimension_semantics=("parallel",), vmem_limit_bytes=VMEM_LIMIT),
        name="s5_scan",
    )(s_re, s_im, a_re, a_im)


def _s5_out_kernel(z_ref, xre_ref, xim_ref, m_ref, wc_ref, y_ref):
    n = S5_NCH
    ys = []
    for g in range(S5_GROUPS):
        xc = jnp.concatenate([xre_ref[pl.ds(g, n, stride=S5_GROUPS), :],
                              xim_ref[pl.ds(g, n, stride=S5_GROUPS), :]], axis=1).astype(BF16)
        ys.append(jnp.dot(z_ref[g], m_ref[g], preferred_element_type=F32)
                  + jnp.dot(xc, wc_ref[g], preferred_element_type=F32))
    ts = _block_rotate(ys)
    for t in range(S5_TC):
        acc = ts[t]
        if t:
            acc = pltpu.roll(acc, S5_WIDTH - S5_GROUP_CH * t, 1)
        y_ref[0, 0, pl.ds(t, n, stride=S5_TC), :] = acc[:, :LANES]
        y_ref[0, 1, pl.ds(t, n, stride=S5_TC), :] = acc[:, LANES:]


def _s5_out(z, xc_re, xc_im, m, wc):
    wspec = pl.BlockSpec((S5_GROUPS, S5_WIDTH, S5_WIDTH), lambda b: (0, 0, 0))
    xspec = pl.BlockSpec((S5_NCH * S5_GROUPS, LANES), lambda b: (0, b))
    return pl.pallas_call(
        _s5_out_kernel,
        grid=(BATCH,),
        in_specs=[pl.BlockSpec((S5_GROUPS, S5_NCH, S5_WIDTH), lambda b: (0, b, 0)), xspec, xspec, wspec, wspec],
        out_specs=pl.BlockSpec((1, 2, SEQ, LANES), lambda b: (b, 0, 0, 0)),
        out_shape=jax.ShapeDtypeStruct((BATCH, 2, SEQ, LANES), F32),
        compiler_params=pltpu.CompilerParams(
            dimension_semantics=("parallel",), vmem_limit_bytes=VMEM_LIMIT),
        name="s5_out",
    )(z, xc_re, xc_im, m, wc)


def _mixer_kernel(x_ref, y_ref, g1_ref, win_ref, bga_ref, bgb_ref, lng_ref, lnb_ref,
                  ws_ref, bs_ref, wglu_ref, bglu_ref, wupa_ref, wupb_ref, wout_ref, g2_ref, wr_ref,
                  x1_ref, h2p_ref, pr_ref, bra_scr, mrg_scr):
    x = x_ref[0]
    hn = _rms(x, g1_ref[...]).astype(BF16)

    def in_proj(c0):
        return jnp.dot(hn, win_ref[:, c0:c0 + MIX_COLS], preferred_element_type=F32)

    nuv = GMLP_WIDTH // MIX_COLS
    us = [_gelu(in_proj(c * MIX_COLS)) for c in range(nuv)]
    vs = [_gelu(in_proj(GMLP_WIDTH + c * MIX_COLS)) for c in range(nuv)]
    mu = sum(jnp.sum(v, axis=-1, keepdims=True) for v in vs) * (1.0 / GMLP_WIDTH)
    vcs = [v - mu for v in vs]
    var = sum(jnp.sum(vc * vc, axis=-1, keepdims=True) for vc in vcs) * (1.0 / GMLP_WIDTH)
    rstd = lax.rsqrt(var + EPS)
    vns = [(vc * rstd * lng_ref[:, c * MIX_COLS:(c + 1) * MIX_COLS]
            + lnb_ref[:, c * MIX_COLS:(c + 1) * MIX_COLS]).astype(BF16) for c, vc in enumerate(vcs)]
    low = lax.broadcasted_iota(I32, (CHUNK, LANES), 1) < GMLP_HEAD
    per = MIX_COLS // LANES
    for ch in range(TILE_M // CHUNK):
        r0 = ch * CHUNK
        for q in range(GMLP_WIDTH // LANES):
            c0 = q * LANES
            l0 = (q % per) * LANES
            vb = vns[q // per][r0:r0 + CHUNK, l0:l0 + LANES]
            z0 = jnp.dot(ws_ref[2 * q], vb, preferred_element_type=F32)
            z1 = jnp.dot(ws_ref[2 * q + 1], vb, preferred_element_type=F32)
            zz = jnp.where(low, z0, z1) + bs_ref[:, c0:c0 + LANES]
            bra_scr[r0:r0 + CHUNK, c0:c0 + LANES] = (us[q // per][r0:r0 + CHUNK, l0:l0 + LANES] * zz).astype(BF16)

    yb = _gelu(jnp.concatenate([y_ref[0, 0], y_ref[0, 1]], axis=1))
    gl = jnp.dot(yb.astype(BF16), wglu_ref[...], preferred_element_type=F32) + bglu_ref[...]
    brb = (yb * jax.nn.sigmoid(gl)).astype(BF16)

    bra = bra_scr[...]
    for c in range(D_MODEL // MIX_COLS):
        cs = slice(c * MIX_COLS, (c + 1) * MIX_COLS)
        ga = jax.nn.sigmoid(in_proj(O_GATE + c * MIX_COLS) + bga_ref[:, cs])
        gb = jax.nn.sigmoid(in_proj(O_GATE + D_MODEL + c * MIX_COLS) + bgb_ref[:, cs])
        mrg_scr[:, cs] = (ga * jnp.dot(bra, wupa_ref[:, cs], preferred_element_type=F32)
                          + gb * jnp.dot(brb, wupb_ref[:, cs], preferred_element_type=F32)).astype(BF16)
    mrg = mrg_scr[...]
    ssq = jnp.zeros((TILE_M, 1), F32)
    for c in range(D_MODEL // MIX_COLS):
        cs = slice(c * MIX_COLS, (c + 1) * MIX_COLS)
        x1c = x[:, cs] + jnp.dot(mrg, wout_ref[:, cs], preferred_element_type=F32)
        x1_ref[0, :, cs] = x1c
        ssq = ssq + jnp.sum(x1c * x1c, axis=-1, keepdims=True)

    h2 = x1_ref[0] * lax.rsqrt(ssq * (1.0 / D_MODEL) + EPS) * g2_ref[...]
    half = D_MODEL // 2
    h2p_ref[0] = pltpu.pack_elementwise([h2[:, :half], h2[:, half:]], packed_dtype=BF16)

    logits = lax.dot_general(wr_ref[...], h2, _NT, precision=lax.Precision.HIGHEST,
                             preferred_element_type=F32)
    mx = jnp.max(logits, axis=0, keepdims=True)
    ex = jnp.exp(logits - mx)
    pr_ref[0] = ex / jnp.sum(ex, axis=0, keepdims=True)


def _mixer(x, y, consts):
    nt = SEQ // TILE_M

    def cspec(a):
        nd = a.ndim
        return pl.BlockSpec(a.shape, lambda b, i: (0,) * nd, pipeline_mode=pl.Buffered(1))

    return pl.pallas_call(
        _mixer_kernel,
        grid=(BATCH, nt),
        in_specs=[pl.BlockSpec((1, TILE_M, D_MODEL), lambda b, i: (b, i, 0)),
                  pl.BlockSpec((1, 2, TILE_M, LANES), lambda b, i: (b, 0, i, 0))] + [cspec(a) for a in consts],
        out_specs=[pl.BlockSpec((1, TILE_M, D_MODEL), lambda b, i: (b, i, 0)),
                   pl.BlockSpec((1, TILE_M, D_MODEL // 2), lambda b, i: (b, i, 0)),
                   pl.BlockSpec((1, N_EXPERTS, TILE_M), lambda b, i: (b, 0, i))],
        out_shape=[jax.ShapeDtypeStruct((BATCH, SEQ, D_MODEL), F32),
                   jax.ShapeDtypeStruct((BATCH, SEQ, D_MODEL // 2), U32),
                   jax.ShapeDtypeStruct((BATCH, N_EXPERTS, SEQ), F32)],
        scratch_shapes=[pltpu.VMEM((TILE_M, GMLP_WIDTH), BF16), pltpu.VMEM((TILE_M, D_MODEL), BF16)],
        compiler_params=pltpu.CompilerParams(
            dimension_semantics=("parallel", "parallel"), vmem_limit_bytes=VMEM_LIMIT),
        name="mixer",
    )(x, y, *consts)


TOPK_NBLK = SEQ // LANES
TOPK_ROWS = 4 * N_EXPERTS
TOPK_SEARCH_STEPS = 28


def _lane_sum(x):
    return jnp.sum(x, axis=1, keepdims=True)


def _topk_kernel(p_ref, tri_ref, idx_ref, gate_ref, eqc_scr, loc_scr, pj_scr, cb_scr, cbp_scr):
    p = p_ref[...]

    def count_ge(th):
        return _lane_sum(jnp.where(p >= th, 1.0, 0.0))

    def search(_, c):
        lo, hi = c
        w = hi - lo
        m1, m2, m3 = lo + 0.25 * w, lo + 0.5 * w, lo + 0.75 * w
        c1, c2, c3 = count_ge(m1) >= CAPACITY, count_ge(m2) >= CAPACITY, count_ge(m3) >= CAPACITY
        return (jnp.where(c3, m3, jnp.where(c2, m2, jnp.where(c1, m1, lo))),
                jnp.where(c1, jnp.where(c2, jnp.where(c3, hi, m3), m2), m1))

    lo, hi = lax.fori_loop(0, TOPK_SEARCH_STEPS, search,
                           (jnp.zeros((TOPK_ROWS, 1), F32), jnp.full((TOPK_ROWS, 1), 2.0, F32)))
    gt = p >= hi
    eq = (p >= lo) & (p < hi)
    need = CAPACITY - _lane_sum(jnp.where(gt, 1.0, 0.0))
    tri = tri_ref[...]
    eqf = jnp.where(eq, 1.0, 0.0)
    carry = jnp.zeros((TOPK_ROWS, 1), F32)
    for j in range(TOPK_NBLK):
        sl = slice(j * LANES, (j + 1) * LANES)
        cs = jnp.dot(eqf[:, sl].astype(BF16), tri, preferred_element_type=F32) + carry
        eqc_scr[:, sl] = cs
        carry = cs[:, LANES - 1:LANES]
    sel = gt | (eq & ((eqc_scr[...] - eqf) < need))
    self = jnp.where(sel, 1.0, 0.0)

    lane = lax.broadcasted_iota(I32, (TOPK_ROWS, LANES), 1)
    cb = jnp.full((TOPK_ROWS, LANES), float(2 * SEQ), F32)
    cbp = cb
    carry = jnp.zeros((TOPK_ROWS, 1), F32)
    for j in range(TOPK_NBLK):
        sl = slice(j * LANES, (j + 1) * LANES)
        loc = jnp.dot(self[:, sl].astype(BF16), tri, preferred_element_type=F32)
        loc_scr[j * TOPK_ROWS:(j + 1) * TOPK_ROWS, :] = loc
        pj_scr[j * TOPK_ROWS:(j + 1) * TOPK_ROWS, :] = p[:, sl]
        cbp = jnp.where(lane == j, carry, cbp)
        carry = carry + loc[:, LANES - 1:LANES]
        cb = jnp.where(lane == j, carry, cb)
    padr = jnp.zeros((LANES - TOPK_ROWS, LANES), F32)
    cb_scr[...] = jnp.concatenate([cb, padr], axis=0).T
    cbp_scr[...] = jnp.concatenate([cbp, padr], axis=0).T

    slot = (lax.broadcasted_iota(I32, (1, CAPACITY), 1) + 1).astype(F32)
    lane_r = lax.broadcasted_iota(I32, (LANES, LANES), 1)
    sub_f = lax.broadcasted_iota(I32, (LANES, 1), 0).astype(F32)
    pad = jnp.zeros((LANES - TOPK_NBLK, LANES), F32)

    def block_rows_t(ref, e):
        return jnp.concatenate([ref[pl.ds(e, TOPK_NBLK, stride=TOPK_ROWS), :], pad], axis=0).T

    def sub_sum(x):
        return jnp.sum(x, axis=0, keepdims=True)

    def per_expert(e, _):
        cb_e = _lane_sum(jnp.where(lane_r == e, cb_scr[...], 0.0))
        cbp_e = _lane_sum(jnp.where(lane_r == e, cbp_scr[...], 0.0))
        hit = (cbp_e < slot) & (slot <= cb_e)
        hitb = jnp.where(hit, 1.0, 0.0).astype(BF16)
        both = sub_sum(jnp.where(hit, cbp_e * float(LANES) + sub_f, 0.0))
        before = jnp.floor(both * (1.0 / LANES))
        jblk = both - before * float(LANES)
        rank = slot - before
        loc = jnp.dot(block_rows_t(loc_scr, e).astype(BF16), hitb, preferred_element_type=F32)
        off = sub_sum(jnp.where(loc < rank, 1.0, 0.0))
        pe = block_rows_t(pj_scr, e)
        hi = pe.astype(BF16)
        r1 = pe - hi.astype(F32)
        mid = r1.astype(BF16)
        lo = (r1 - mid.astype(F32)).astype(BF16)
        pg = (jnp.dot(hi, hitb, preferred_element_type=F32) + jnp.dot(mid, hitb, preferred_element_type=F32)
              + jnp.dot(lo, hitb, preferred_element_type=F32))
        idx_ref[pl.ds(e, 1), :] = (jblk * float(LANES) + off).astype(I32)
        gate_ref[pl.ds(e, 1), :] = sub_sum(jnp.where(sub_f == off, pg, 0.0))
        return 0

    lax.fori_loop(0, TOPK_ROWS, per_expert, 0, unroll=4)


def _topk(probs, tri):
    nrow = BATCH * N_EXPERTS
    return pl.pallas_call(
        _topk_kernel,
        grid=(nrow // TOPK_ROWS,),
        in_specs=[pl.BlockSpec((TOPK_ROWS, SEQ), lambda i: (i, 0)),
                  pl.BlockSpec((LANES, LANES), lambda i: (0, 0))],
        out_specs=[pl.BlockSpec((TOPK_ROWS, CAPACITY), lambda i: (i, 0)),
                   pl.BlockSpec((TOPK_ROWS, CAPACITY), lambda i: (i, 0))],
        out_shape=[jax.ShapeDtypeStruct((nrow, CAPACITY), I32),
                   jax.ShapeDtypeStruct((nrow, CAPACITY), F32)],
        scratch_shapes=[pltpu.VMEM((TOPK_ROWS, SEQ), F32),
                        pltpu.VMEM((TOPK_NBLK * TOPK_ROWS, LANES), F32),
                        pltpu.VMEM((TOPK_NBLK * TOPK_ROWS, LANES), F32),
                        pltpu.VMEM((LANES, LANES), F32),
                        pltpu.VMEM((LANES, LANES), F32)],
        compiler_params=pltpu.CompilerParams(
            dimension_semantics=("parallel",), vmem_limit_bytes=VMEM_LIMIT),
        name="topk",
    )(probs, tri)


MOE_GROUP = 8
MOE_FCHUNK = 512


def _gather_rows(idx_ref, ibase, h2p_ref, dst_ref, dbase, nrows):
    for k in range(nrows):
        r = idx_ref[0, 0, ibase + k]
        dst_ref[pl.ds(dbase + k, 1), :] = h2p_ref[0, pl.ds(r, 1), :]


def _scatter_rows(idx_ref, gate_ref, ibase, scale, y_ref, ybase, out_ref, nrows):
    for k0 in range(0, nrows, MOE_GROUP):
        rs, vals = [], []
        for k in range(k0, k0 + MOE_GROUP):
            r = idx_ref[0, 0, ibase + k]
            g = gate_ref[0, 0, ibase + k] * scale
            rs.append(r)
            vals.append(out_ref[0, pl.ds(r, 1), :] + g * y_ref[pl.ds(ybase + k, 1), :])
        for r, v in zip(rs, vals):
            out_ref[0, pl.ds(r, 1), :] = v


def _moe_kernel(idx_ref, gate_ref, h2p_ref, x1_ref, gf_ref, wg_ref, wu_ref, wd_ref, out_ref, xg_scr, xs_scr, y_scr):
    e = pl.program_id(1)
    f = pl.program_id(2)
    nf = EXPERT_FF // TILE_F
    step = e * nf + f
    rq = CAPACITY // nf
    half = D_MODEL // 2
    cur = e % 2
    y_cur, y_nxt = y_scr.at[cur], y_scr.at[1 - cur]

    @pl.when(step == 0)
    def _():
        out_ref[...] = jnp.zeros_like(out_ref)
        y_scr[...] = jnp.zeros_like(y_scr)

        def gather(i, _):
            base = pl.multiple_of(i * 32, 32)
            _gather_rows(idx_ref, base, h2p_ref, xg_scr, base, 32)
            return 0

        lax.fori_loop(0, CAPACITY // 32, gather, 0)

    @pl.when(step < SEQ // TILE_X)
    def _():
        rows = pl.ds(pl.multiple_of(step * TILE_X, TILE_X), TILE_X)
        out_ref[0, rows, :] = out_ref[0, rows, :] + x1_ref[0]

    @pl.when(f == 0)
    def _():
        w = xg_scr[...]
        for i in range(2):
            xs_scr[:, i * half:(i + 1) * half] = pltpu.unpack_elementwise(
                w, index=i, packed_dtype=BF16, unpacked_dtype=F32).astype(BF16)

    qbase = pl.multiple_of(f * rq, rq)
    e_prev = (e + N_EXPERTS - 1) % N_EXPERTS
    e_next = (e + 1) % N_EXPERTS
    scale = jnp.where(e >= 1, 1.0, 0.0).astype(F32)
    _scatter_rows(idx_ref, gate_ref, e_prev * CAPACITY + qbase, scale, y_nxt, qbase, out_ref, rq)
    _gather_rows(idx_ref, e_next * CAPACITY + qbase, h2p_ref, xg_scr, qbase, rq)

    xs = xs_scr[...]
    y = jnp.where(f > 0, y_cur[...], 0.0)
    for c0 in range(0, TILE_F, MOE_FCHUNK):
        cs = slice(c0, c0 + MOE_FCHUNK)
        a = jnp.dot(xs, wg_ref[0, :, cs], preferred_element_type=F32)
        u = jnp.dot(xs, wu_ref[0, :, cs], preferred_element_type=F32)
        hm = (a * jax.nn.sigmoid(a) * u).astype(BF16)
        y = y + jnp.dot(hm, wd_ref[0, cs, :], preferred_element_type=F32)
    y_cur[...] = y

    @pl.when((e == N_EXPERTS - 1) & (f == nf - 1))
    def _():
        def scatter(i, _):
            base = pl.multiple_of(i * 32, 32)
            _scatter_rows(idx_ref, gate_ref, e * CAPACITY + base, jnp.float32(1.0), y_cur, base, out_ref, 32)
            return 0

        lax.fori_loop(0, CAPACITY // 32, scatter, 0)

        def norm(i, _):
            rows = pl.ds(pl.multiple_of(i * TILE_X, TILE_X), TILE_X)
            out_ref[0, rows, :] = _rms(out_ref[0, rows, :], gf_ref[...])
            return 0

        lax.fori_loop(0, SEQ // TILE_X, norm, 0)


def _moe(idx, gate, h2p, x1, gf, wg, wu, wd):
    nf = EXPERT_FF // TILE_F
    nx = SEQ // TILE_X
    assert nx <= N_EXPERTS * nf
    sspec = pl.BlockSpec((1, 1, N_EXPERTS * CAPACITY), lambda b, e, f: (b, 0, 0), memory_space=pltpu.SMEM)
    return pl.pallas_call(
        _moe_kernel,
        grid=(BATCH, N_EXPERTS, nf),
        in_specs=[sspec, sspec,
                  pl.BlockSpec((1, SEQ, D_MODEL // 2), lambda b, e, f: (b, 0, 0), pipeline_mode=pl.Buffered(1)),
                  pl.BlockSpec((1, TILE_X, D_MODEL), lambda b, e, f: (b, jnp.minimum(e * nf + f, nx - 1), 0)),
                  pl.BlockSpec((1, D_MODEL), lambda b, e, f: (0, 0)),
                  pl.BlockSpec((1, D_MODEL, TILE_F), lambda b, e, f: (e, 0, f)),
                  pl.BlockSpec((1, D_MODEL, TILE_F), lambda b, e, f: (e, 0, f)),
                  pl.BlockSpec((1, TILE_F, D_MODEL), lambda b, e, f: (e, f, 0))],
        out_specs=pl.BlockSpec((1, SEQ, D_MODEL), lambda b, e, f: (b, 0, 0), pipeline_mode=pl.Buffered(1)),
        out_shape=jax.ShapeDtypeStruct((BATCH, SEQ, D_MODEL), F32),
        scratch_shapes=[pltpu.VMEM((CAPACITY, D_MODEL // 2), U32),
                        pltpu.VMEM((CAPACITY, D_MODEL), BF16),
                        pltpu.VMEM((2, CAPACITY, D_MODEL), F32)],
        compiler_params=pltpu.CompilerParams(
            dimension_semantics=("parallel", "arbitrary", "arbitrary"), vmem_limit_bytes=VMEM_LIMIT),
        name="moe",
    )(idx.reshape(BATCH, 1, N_EXPERTS * CAPACITY), gate.reshape(BATCH, 1, N_EXPERTS * CAPACITY), h2p, x1, gf,
      wg, wu, wd)


def kernel(x, norm1_g, w_in, b_gate, gmlp_ln_g, gmlp_ln_b, gmlp_w_s, gmlp_b_s, s5_lam_re, s5_lam_im, s5_log_dt,
           s5_b_re, s5_b_im, s5_c_re, s5_c_im, s5_d, s5_w_glu, s5_b_glu, w_up_a, w_up_b, w_out, norm2_g,
           w_router, w_gate, w_up, w_down, final_g):
    assert x.shape == (BATCH, SEQ, D_MODEL) and w_in.shape[0] == 1
    l = 0
    row = lambda a: a.astype(F32).reshape(1, -1)

    w_in_b = w_in[l].astype(BF16)
    m_mat, wsum, wc, a_re, a_im = _s5_tables(s5_lam_re[l], s5_lam_im[l], s5_log_dt[l], s5_b_re[l], s5_b_im[l],
                                             s5_c_re[l], s5_c_im[l], s5_d[l])
    g1 = row(norm1_g[l])

    z, s_re, s_im = _s5_in(x, g1, w_in_b, wsum)
    shape3 = (S5_NCH, S5_GROUPS, BATCH * LANES)
    xc_re, xc_im = _s5_scan(s_re.reshape(shape3), s_im.reshape(shape3),
                            jnp.tile(a_re, (1, BATCH)), jnp.tile(a_im, (1, BATCH)))
    shape2 = (S5_NCH * S5_GROUPS, BATCH * LANES)
    y = _s5_out(z, xc_re.reshape(shape2), xc_im.reshape(shape2), m_mat, wc)

    consts = (
        g1, w_in_b, row(b_gate[l][:D_MODEL]), row(b_gate[l][D_MODEL:]), row(gmlp_ln_g[l]), row(gmlp_ln_b[l]),
        gmlp_w_s[l].astype(BF16), jnp.repeat(jnp.transpose(gmlp_b_s[l].astype(F32)), GMLP_HEAD, axis=1),
        s5_w_glu[l].astype(BF16), row(s5_b_glu[l]), w_up_a[l].astype(BF16), w_up_b[l].astype(BF16),
        w_out[l].astype(BF16), row(norm2_g[l]), jnp.transpose(w_router[l].astype(F32)),
    )
    x1, h2p, probs = _mixer(x, y, consts)

    tri = (jnp.arange(LANES)[:, None] <= jnp.arange(LANES)[None, :]).astype(BF16)
    idx, gate = _topk(probs.reshape(BATCH * N_EXPERTS, SEQ), tri)
    return _moe(idx, gate, h2p, x1, row(final_g),
                w_gate[l].astype(BF16), w_up[l].astype(BF16), w_down[l].astype(BF16))
```

```python
import jax
import jax.numpy as jnp
import numpy as np
from jax import lax
from jax.experimental import pallas as pl
from jax.experimental.pallas import tpu as pltpu

F32 = jnp.float32
BF16 = jnp.bfloat16
I32 = jnp.int32
U32 = jnp.uint32

D_MODEL = 1024
BATCH = 16
SEQ = 4096
GMLP_WIDTH = 512
GMLP_GROUPS = 8
GMLP_HEAD = GMLP_WIDTH // GMLP_GROUPS
CHUNK = 128
S5_WIDTH = 256
S5_GROUP_CH = 16
S5_GROUPS = S5_WIDTH // S5_GROUP_CH
S5_STATE = 64
N_EXPERTS = 16
CAPACITY = 2 * SEQ // N_EXPERTS
EXPERT_FF = 2048
EPS = 1e-6
O_S5 = 2 * GMLP_WIDTH
O_GATE = O_S5 + S5_WIDTH

LANES = 128
S5_TC = 16
S5_NCH = SEQ // S5_TC
TILE_A = 1024
TILE_M = 1024
MIX_COLS = 256
TILE_F = 1024
TILE_X = 256
SCAN_LANES = 256
VMEM_LIMIT = 56 * 1024 * 1024

_NT = (((1,), (1,)), ((), ()))


def _gelu(x):
    return 0.5 * x * (1.0 + lax.erf(x * (2.0 ** -0.5)))


def _rms(x, g):
    return x * lax.rsqrt(jnp.mean(x * x, axis=-1, keepdims=True) + EPS) * g


def _block_rotate(arrs):
    n = len(arrs)
    blk = lax.broadcasted_iota(I32, arrs[0].shape, 1) // S5_GROUP_CH
    w = [arrs[(-i) % n] for i in range(n)]
    for s in range(n.bit_length() - 1):
        bit = ((blk >> s) & 1) == 1
        w = [jnp.where(bit, w[(i - (1 << s)) % n], w[i]) for i in range(n)]
    return w


def _cpow(er, ei, k):
    mag = jnp.exp(k * er)
    return mag * jnp.cos(k * ei), mag * jnp.sin(k * ei)


def _cmul(a, b):
    return a[0] * b[0] - a[1] * b[1], a[0] * b[1] + a[1] * b[0]


def _s5_tables(lam_re, lam_im, log_dt, b_re, b_im, c_re, c_im, d_skip):
    hp = lax.Precision.HIGH
    tc, ng, nh = S5_TC, S5_GROUPS, S5_GROUP_CH
    lr, li = lam_re.astype(F32), lam_im.astype(F32)
    dt = jnp.exp(log_dt.astype(F32))[..., None]
    er, ei = lr * dt, li * dt
    lb = _cpow(er, ei, 1.0)
    den = lr * lr + li * li
    q = (((lb[0] - 1.0) * lr + lb[1] * li) / den, (lb[1] * lr - (lb[0] - 1.0) * li) / den)
    bbar = _cmul((q[0][..., None], q[1][..., None]), (b_re.astype(F32), b_im.astype(F32)))
    cm = (c_re.astype(F32), c_im.astype(F32))

    kk = np.arange(tc, dtype=np.float32)[:, None, None, None]
    pw = _cpow(er[None], ei[None], kk)
    t = _cmul((cm[0][None], cm[1][None]), (pw[0][:, :, :, None, :], pw[1][:, :, :, None, :]))
    taps = (jnp.einsum('kdghp,dgpi->kdghi', t[0], bbar[0], precision=hp)
            - jnp.einsum('kdghp,dgpi->kdghi', t[1], bbar[1], precision=hp))

    tau = (np.arange(tc)[None, :] - np.arange(ng)[:, None]) % tc
    diff = tau[:, None, :] - tau[:, :, None]
    oh_f = (diff[..., None] == np.arange(tc)).astype(np.float32)
    oh_r = (-diff[..., None] == np.arange(tc)).astype(np.float32)
    m = (jnp.einsum('gabk,kghi->gaibh', oh_f, taps[:, 0], precision=hp)
         + jnp.einsum('gabk,kghi->gaibh', oh_r, taps[:, 1], precision=hp))
    eye_j = np.eye(tc, dtype=np.float32)[None, :, None, :, None]
    eye_h = np.eye(nh, dtype=np.float32)[None, None, :, None, :]
    m = (m + eye_j * eye_h * d_skip.astype(F32)[:, None, None, None, :]).reshape(ng, tc * nh, tc * nh)

    tauf = tau.astype(np.float32)
    bt = (jnp.transpose(bbar[0], (0, 1, 3, 2)), jnp.transpose(bbar[1], (0, 1, 3, 2)))
    parts = []
    for d, kin in ((0, tc - 1.0 - tauf), (1, tauf)):
        p_in = _cpow(er[d][:, None, None, :], ei[d][:, None, None, :], kin[:, :, None, None])
        parts.append(_cmul(p_in, (bt[0][d][:, None], bt[1][d][:, None])))
    wsum = jnp.concatenate([parts[0][0], parts[1][0], parts[0][1], parts[1][1]], axis=-1)
    wsum = wsum.reshape(ng, tc * nh, 4 * S5_STATE)

    ct = (jnp.transpose(cm[0], (0, 1, 3, 2)), jnp.transpose(cm[1], (0, 1, 3, 2)))
    parts = []
    for d, kout in ((0, tauf + 1.0), (1, tc - tauf)):
        p_out = _cpow(er[d][:, :, None, None], ei[d][:, :, None, None], kout[:, None, :, None])
        parts.append(_cmul(p_out, (ct[0][d][:, :, None, :], ct[1][d][:, :, None, :])))
    wc = jnp.concatenate([parts[0][0], parts[1][0], -parts[0][1], -parts[1][1]], axis=1)
    wc = wc.reshape(ng, 4 * S5_STATE, tc * nh)

    a16 = _cpow(er, ei, float(tc))
    a_re = jnp.concatenate([a16[0][0], a16[0][1]], axis=-1)
    a_im = jnp.concatenate([a16[1][0], a16[1][1]], axis=-1)
    return m.astype(BF16), wsum.astype(BF16), wc.astype(BF16), a_re, a_im


def _s5_in_kernel(x_ref, g1_ref, ws5_ref, wsum_ref, z_ref, sre_ref, sim_ref, s_scr):
    n = TILE_A // S5_TC
    hn = _rms(x_ref[0], g1_ref[...]).astype(BF16)
    s = jnp.dot(hn, ws5_ref[...], preferred_element_type=F32)
    s_scr[0] = s[:, :LANES]
    s_scr[1] = s[:, LANES:]
    rows = []
    for t in range(S5_TC):
        a = jnp.concatenate([s_scr[0, pl.ds(t, n, stride=S5_TC), :],
                             s_scr[1, pl.ds(t, n, stride=S5_TC), :]], axis=1)
        rows.append(pltpu.roll(a, S5_GROUP_CH * t, 1) if t else a)
    zs = _block_rotate(rows)
    for g in range(S5_GROUPS):
        zb = zs[g].astype(BF16)
        z_ref[g] = zb
        sm = jnp.dot(zb, wsum_ref[g], preferred_element_type=F32)
        sre_ref[pl.ds(g, n, stride=S5_GROUPS), :] = sm[:, :LANES]
        sim_ref[pl.ds(g, n, stride=S5_GROUPS), :] = sm[:, LANES:]


def _s5_in(x, g1, ws5, wsum):
    nt = SEQ // TILE_A
    n = TILE_A // S5_TC
    return pl.pallas_call(
        _s5_in_kernel,
        grid=(BATCH, nt),
        in_specs=[
            pl.BlockSpec((1, TILE_A, D_MODEL), lambda b, i: (b, i, 0)),
            pl.BlockSpec((1, D_MODEL), lambda b, i: (0, 0)),
            pl.BlockSpec((D_MODEL, S5_WIDTH), lambda b, i: (0, O_S5 // S5_WIDTH)),
            pl.BlockSpec((S5_GROUPS, S5_WIDTH, S5_WIDTH), lambda b, i: (0, 0, 0)),
        ],
        out_specs=[
            pl.BlockSpec((S5_GROUPS, n, S5_WIDTH), lambda b, i: (0, b * nt + i, 0)),
            pl.BlockSpec((n * S5_GROUPS, LANES), lambda b, i: (i, b)),
            pl.BlockSpec((n * S5_GROUPS, LANES), lambda b, i: (i, b)),
        ],
        out_shape=[
            jax.ShapeDtypeStruct((S5_GROUPS, BATCH * S5_NCH, S5_WIDTH), BF16),
            jax.ShapeDtypeStruct((S5_NCH * S5_GROUPS, BATCH * LANES), F32),
            jax.ShapeDtypeStruct((S5_NCH * S5_GROUPS, BATCH * LANES), F32),
        ],
        scratch_shapes=[pltpu.VMEM((2, TILE_A, LANES), F32)],
        compiler_params=pltpu.CompilerParams(
            dimension_semantics=("parallel", "parallel"), vmem_limit_bytes=VMEM_LIMIT),
        name="s5_in",
    )(x, g1, ws5, wsum)


def _s5_scan_kernel(sre_ref, sim_ref, are_ref, aim_ref, xre_ref, xim_ref):
    ar = are_ref[...]
    ai = aim_ref[...]
    fwd = (lax.broadcasted_iota(I32, ar.shape, 1) % LANES) < S5_STATE
    zero = jnp.zeros_like(ar)

    def step(c, xr, xi):
        sr = sre_ref[c]
        si = sim_ref[c]
        return ar * xr - ai * xi + sr, ar * xi + ai * xr + si

    def fwd_body(c, carry):
        xr, xi = carry
        xre_ref[c] = xr
        xim_ref[c] = xi
        return step(c, xr, xi)

    lax.fori_loop(0, S5_NCH, fwd_body, (zero, zero))

    def rev_body(i, carry):
        xr, xi = carry
        c = S5_NCH - 1 - i
        xre_ref[c] = jnp.where(fwd, xre_ref[c], xr)
        xim_ref[c] = jnp.where(fwd, xim_ref[c], xi)
        return step(c, xr, xi)

    lax.fori_loop(0, S5_NCH, rev_body, (zero, zero))


def _s5_scan(s_re, s_im, a_re, a_im):
    nl = BATCH * LANES
    blk3 = pl.BlockSpec((S5_NCH, S5_GROUPS, SCAN_LANES), lambda j: (0, 0, j))
    blk2 = pl.BlockSpec((S5_GROUPS, SCAN_LANES), lambda j: (0, j))
    return pl.pallas_call(
        _s5_scan_kernel,
        grid=(nl // SCAN_LANES,),
        in_specs=[blk3, blk3, blk2, blk2],
        out_specs=[blk3, blk3],
        out_shape=[jax.ShapeDtypeStruct((S5_NCH, S5_GROUPS, nl), F32)] * 2,
        compiler_params=pltpu.CompilerParams(
            dimension_semantics=("parallel",), vmem_limit_bytes=VMEM_LIMIT),
        name="s5_scan",
    )(s_re, s_im, a_re, a_im)


def _s5_out_kernel(z_ref, xre_ref, xim_ref, m_ref, wc_ref, y_ref):
    n = S5_NCH
    ys = []
    for g in range(S5_GROUPS):
        xc = jnp.concatenate([xre_ref[pl.ds(g, n, stride=S5_GROUPS), :],
                              xim_ref[pl.ds(g, n, stride=S5_GROUPS), :]], axis=1).astype(BF16)
        ys.append(jnp.dot(z_ref[g], m_ref[g], preferred_element_type=F32)
                  + jnp.dot(xc, wc_ref[g], preferred_element_type=F32))
    ts = _block_rotate(ys)
    for t in range(S5_TC):
        acc = ts[t]
        if t:
            acc = pltpu.roll(acc, S5_WIDTH - S5_GROUP_CH * t, 1)
        y_ref[0, 0, pl.ds(t, n, stride=S5_TC), :] = acc[:, :LANES]
        y_ref[0, 1, pl.ds(t, n, stride=S5_TC), :] = acc[:, LANES:]


def _s5_out(z, xc_re, xc_im, m, wc):
    wspec = pl.BlockSpec((S5_GROUPS, S5_WIDTH, S5_WIDTH), lambda b: (0, 0, 0))
    xspec = pl.BlockSpec((S5_NCH * S5_GROUPS, LANES), lambda b: (0, b))
    return pl.pallas_call(
        _s5_out_kernel,
        grid=(BATCH,),
        in_specs=[pl.BlockSpec((S5_GROUPS, S5_NCH, S5_WIDTH), lambda b: (0, b, 0)), xspec, xspec, wspec, wspec],
        out_specs=pl.BlockSpec((1, 2, SEQ, LANES), lambda b: (b, 0, 0, 0)),
        out_shape=jax.ShapeDtypeStruct((BATCH, 2, SEQ, LANES), F32),
        compiler_params=pltpu.CompilerParams(
            dimension_semantics=("parallel",), vmem_limit_bytes=VMEM_LIMIT),
        name="s5_out",
    )(z, xc_re, xc_im, m, wc)


def _mixer_kernel(x_ref, y_ref, g1_ref, win_ref, bga_ref, bgb_ref, lng_ref, lnb_ref,
                  ws_ref, bs_ref, wglu_ref, bglu_ref, wupa_ref, wupb_ref, wout_ref, g2_ref, wr_ref,
                  x1_ref, h2p_ref, pr_ref, bra_scr, mrg_scr):
    x = x_ref[0]
    hn = _rms(x, g1_ref[...]).astype(BF16)

    def in_proj(c0):
        return jnp.dot(hn, win_ref[:, c0:c0 + MIX_COLS], preferred_element_type=F32)

    nuv = GMLP_WIDTH // MIX_COLS
    us = [_gelu(in_proj(c * MIX_COLS)) for c in range(nuv)]
    vs = [_gelu(in_proj(GMLP_WIDTH + c * MIX_COLS)) for c in range(nuv)]
    mu = sum(jnp.sum(v, axis=-1, keepdims=True) for v in vs) * (1.0 / GMLP_WIDTH)
    vcs = [v - mu for v in vs]
    var = sum(jnp.sum(vc * vc, axis=-1, keepdims=True) for vc in vcs) * (1.0 / GMLP_WIDTH)
    rstd = lax.rsqrt(var + EPS)
    vns = [(vc * rstd * lng_ref[:, c * MIX_COLS:(c + 1) * MIX_COLS]
            + lnb_ref[:, c * MIX_COLS:(c + 1) * MIX_COLS]).astype(BF16) for c, vc in enumerate(vcs)]
    low = lax.broadcasted_iota(I32, (CHUNK, LANES), 1) < GMLP_HEAD
    per = MIX_COLS // LANES
    for ch in range(TILE_M // CHUNK):
        r0 = ch * CHUNK
        for q in range(GMLP_WIDTH // LANES):
            c0 = q * LANES
            l0 = (q % per) * LANES
            vb = vns[q // per][r0:r0 + CHUNK, l0:l0 + LANES]
            z0 = jnp.dot(ws_ref[2 * q], vb, preferred_element_type=F32)
            z1 = jnp.dot(ws_ref[2 * q + 1], vb, preferred_element_type=F32)
            zz = jnp.where(low, z0, z1) + bs_ref[:, c0:c0 + LANES]
            bra_scr[r0:r0 + CHUNK, c0:c0 + LANES] = (us[q // per][r0:r0 + CHUNK, l0:l0 + LANES] * zz).astype(BF16)

    yb = _gelu(jnp.concatenate([y_ref[0, 0], y_ref[0, 1]], axis=1))
    gl = jnp.dot(yb.astype(BF16), wglu_ref[...], preferred_element_type=F32) + bglu_ref[...]
    brb = (yb * jax.nn.sigmoid(gl)).astype(BF16)

    bra = bra_scr[...]
    for c in range(D_MODEL // MIX_COLS):
        cs = slice(c * MIX_COLS, (c + 1) * MIX_COLS)
        ga = jax.nn.sigmoid(in_proj(O_GATE + c * MIX_COLS) + bga_ref[:, cs])
        gb = jax.nn.sigmoid(in_proj(O_GATE + D_MODEL + c * MIX_COLS) + bgb_ref[:, cs])
        mrg_scr[:, cs] = (ga * jnp.dot(bra, wupa_ref[:, cs], preferred_element_type=F32)
                          + gb * jnp.dot(brb, wupb_ref[:, cs], preferred_element_type=F32)).astype(BF16)
    mrg = mrg_scr[...]
    ssq = jnp.zeros((TILE_M, 1), F32)
    for c in range(D_MODEL // MIX_COLS):
        cs = slice(c * MIX_COLS, (c + 1) * MIX_COLS)
        x1c = x[:, cs] + jnp.dot(mrg, wout_ref[:, cs], preferred_element_type=F32)
        x1_ref[0, :, cs] = x1c
        ssq = ssq + jnp.sum(x1c * x1c, axis=-1, keepdims=True)

    h2 = x1_ref[0] * lax.rsqrt(ssq * (1.0 / D_MODEL) + EPS) * g2_ref[...]
    half = D_MODEL // 2
    h2p_ref[0] = pltpu.pack_elementwise([h2[:, :half], h2[:, half:]], packed_dtype=BF16)

    lg = lax.dot_general(wr_ref[...], h2.astype(BF16), _NT, preferred_element_type=F32)
    logits = lg[:N_EXPERTS] + lg[N_EXPERTS:]
    mx = jnp.max(logits, axis=0, keepdims=True)
    ex = jnp.exp(logits - mx)
    pr_ref[0] = ex / jnp.sum(ex, axis=0, keepdims=True)


def _mixer(x, y, consts):
    nt = SEQ // TILE_M

    def cspec(a):
        nd = a.ndim
        return pl.BlockSpec(a.shape, lambda b, i: (0,) * nd, pipeline_mode=pl.Buffered(1))

    return pl.pallas_call(
        _mixer_kernel,
        grid=(BATCH, nt),
        in_specs=[pl.BlockSpec((1, TILE_M, D_MODEL), lambda b, i: (b, i, 0)),
                  pl.BlockSpec((1, 2, TILE_M, LANES), lambda b, i: (b, 0, i, 0))] + [cspec(a) for a in consts],
        out_specs=[pl.BlockSpec((1, TILE_M, D_MODEL), lambda b, i: (b, i, 0)),
                   pl.BlockSpec((1, TILE_M, D_MODEL // 2), lambda b, i: (b, i, 0)),
                   pl.BlockSpec((1, N_EXPERTS, TILE_M), lambda b, i: (b, 0, i))],
        out_shape=[jax.ShapeDtypeStruct((BATCH, SEQ, D_MODEL), F32),
                   jax.ShapeDtypeStruct((BATCH, SEQ, D_MODEL // 2), U32),
                   jax.ShapeDtypeStruct((BATCH, N_EXPERTS, SEQ), F32)],
        scratch_shapes=[pltpu.VMEM((TILE_M, GMLP_WIDTH), BF16), pltpu.VMEM((TILE_M, D_MODEL), BF16)],
        compiler_params=pltpu.CompilerParams(
            dimension_semantics=("parallel", "parallel"), vmem_limit_bytes=VMEM_LIMIT),
        name="mixer",
    )(x, y, *consts)


TOPK_NBLK = SEQ // LANES
TOPK_ROWS = 4 * N_EXPERTS
TOPK_SEARCH_STEPS = 28


def _lane_sum(x):
    return jnp.sum(x, axis=1, keepdims=True)


def _topk_kernel(p_ref, tri_ref, idx_ref, gate_ref, eqc_scr, loc_scr, pj_scr, cb_scr, cbp_scr):
    p = p_ref[...]

    def count_ge(th):
        return _lane_sum(jnp.where(p >= th, 1.0, 0.0))

    def search(_, c):
        lo, hi = c
        w = hi - lo
        m1, m2, m3 = lo + 0.25 * w, lo + 0.5 * w, lo + 0.75 * w
        c1, c2, c3 = count_ge(m1) >= CAPACITY, count_ge(m2) >= CAPACITY, count_ge(m3) >= CAPACITY
        return (jnp.where(c3, m3, jnp.where(c2, m2, jnp.where(c1, m1, lo))),
                jnp.where(c1, jnp.where(c2, jnp.where(c3, hi, m3), m2), m1))

    lo, hi = lax.fori_loop(0, TOPK_SEARCH_STEPS, search,
                           (jnp.zeros((TOPK_ROWS, 1), F32), jnp.full((TOPK_ROWS, 1), 2.0, F32)))
    gt = p >= hi
    eq = (p >= lo) & (p < hi)
    need = CAPACITY - _lane_sum(jnp.where(gt, 1.0, 0.0))
    tri = tri_ref[...]
    eqf = jnp.where(eq, 1.0, 0.0)
    carry = jnp.zeros((TOPK_ROWS, 1), F32)
    for j in range(TOPK_NBLK):
        sl = slice(j * LANES, (j + 1) * LANES)
        cs = jnp.dot(eqf[:, sl].astype(BF16), tri, preferred_element_type=F32) + carry
        eqc_scr[:, sl] = cs
        carry = cs[:, LANES - 1:LANES]
    sel = gt | (eq & ((eqc_scr[...] - eqf) < need))
    self = jnp.where(sel, 1.0, 0.0)

    lane = lax.broadcasted_iota(I32, (TOPK_ROWS, LANES), 1)
    cb = jnp.full((TOPK_ROWS, LANES), float(2 * SEQ), F32)
    cbp = cb
    carry = jnp.zeros((TOPK_ROWS, 1), F32)
    for j in range(TOPK_NBLK):
        sl = slice(j * LANES, (j + 1) * LANES)
        loc = jnp.dot(self[:, sl].astype(BF16), tri, preferred_element_type=F32)
        loc_scr[j * TOPK_ROWS:(j + 1) * TOPK_ROWS, :] = loc
        pj_scr[j * TOPK_ROWS:(j + 1) * TOPK_ROWS, :] = p[:, sl]
        cbp = jnp.where(lane == j, carry, cbp)
        carry = carry + loc[:, LANES - 1:LANES]
        cb = jnp.where(lane == j, carry, cb)
    padr = jnp.zeros((LANES - TOPK_ROWS, LANES), F32)
    cb_scr[...] = jnp.concatenate([cb, padr], axis=0).T
    cbp_scr[...] = jnp.concatenate([cbp, padr], axis=0).T

    slot = (lax.broadcasted_iota(I32, (1, CAPACITY), 1) + 1).astype(F32)
    lane_r = lax.broadcasted_iota(I32, (LANES, LANES), 1)
    sub_f = lax.broadcasted_iota(I32, (LANES, 1), 0).astype(F32)
    pad = jnp.zeros((LANES - TOPK_NBLK, LANES), F32)

    def block_rows_t(ref, e):
        return jnp.concatenate([ref[pl.ds(e, TOPK_NBLK, stride=TOPK_ROWS), :], pad], axis=0).T

    def sub_sum(x):
        return jnp.sum(x, axis=0, keepdims=True)

    def per_expert(e, _):
        cb_e = _lane_sum(jnp.where(lane_r == e, cb_scr[...], 0.0))
        cbp_e = _lane_sum(jnp.where(lane_r == e, cbp_scr[...], 0.0))
        hit = (cbp_e < slot) & (slot <= cb_e)
        hitb = jnp.where(hit, 1.0, 0.0).astype(BF16)
        both = sub_sum(jnp.where(hit, cbp_e * float(LANES) + sub_f, 0.0))
        before = jnp.floor(both * (1.0 / LANES))
        jblk = both - before * float(LANES)
        rank = slot - before
        loc = jnp.dot(block_rows_t(loc_scr, e).astype(BF16), hitb, preferred_element_type=F32)
        off = sub_sum(jnp.where(loc < rank, 1.0, 0.0))
        pe = block_rows_t(pj_scr, e)
        hi = pe.astype(BF16)
        r1 = pe - hi.astype(F32)
        mid = r1.astype(BF16)
        lo = (r1 - mid.astype(F32)).astype(BF16)
        pg = (jnp.dot(hi, hitb, preferred_element_type=F32) + jnp.dot(mid, hitb, preferred_element_type=F32)
              + jnp.dot(lo, hitb, preferred_element_type=F32))
        idx_ref[pl.ds(e, 1), :] = (jblk * float(LANES) + off).astype(I32)
        gate_ref[pl.ds(e, 1), :] = sub_sum(jnp.where(sub_f == off, pg, 0.0))
        return 0

    lax.fori_loop(0, TOPK_ROWS, per_expert, 0, unroll=4)


def _topk(probs, tri):
    nrow = BATCH * N_EXPERTS
    return pl.pallas_call(
        _topk_kernel,
        grid=(nrow // TOPK_ROWS,),
        in_specs=[pl.BlockSpec((TOPK_ROWS, SEQ), lambda i: (i, 0)),
                  pl.BlockSpec((LANES, LANES), lambda i: (0, 0))],
        out_specs=[pl.BlockSpec((TOPK_ROWS, CAPACITY), lambda i: (i, 0)),
                   pl.BlockSpec((TOPK_ROWS, CAPACITY), lambda i: (i, 0))],
        out_shape=[jax.ShapeDtypeStruct((nrow, CAPACITY), I32),
                   jax.ShapeDtypeStruct((nrow, CAPACITY), F32)],
        scratch_shapes=[pltpu.VMEM((TOPK_ROWS, SEQ), F32),
                        pltpu.VMEM((TOPK_NBLK * TOPK_ROWS, LANES), F32),
                        pltpu.VMEM((TOPK_NBLK * TOPK_ROWS, LANES), F32),
                        pltpu.VMEM((LANES, LANES), F32),
                        pltpu.VMEM((LANES, LANES), F32)],
        compiler_params=pltpu.CompilerParams(
            dimension_semantics=("parallel",), vmem_limit_bytes=VMEM_LIMIT),
        name="topk",
    )(probs, tri)


MOE_GROUP = 8
MOE_FCHUNK = 512


def _gather_rows(idx_ref, ibase, h2p_ref, dst_ref, dbase, nrows):
    for k in range(nrows):
        r = idx_ref[0, 0, ibase + k]
        dst_ref[pl.ds(dbase + k, 1), :] = h2p_ref[0, pl.ds(r, 1), :]


def _scatter_rows(idx_ref, gate_ref, ibase, scale, y_ref, ybase, out_ref, nrows):
    for k0 in range(0, nrows, MOE_GROUP):
        rs, vals = [], []
        for k in range(k0, k0 + MOE_GROUP):
            r = idx_ref[0, 0, ibase + k]
            g = gate_ref[0, 0, ibase + k] * scale
            rs.append(r)
            vals.append(out_ref[0, pl.ds(r, 1), :] + g * y_ref[pl.ds(ybase + k, 1), :])
        for r, v in zip(rs, vals):
            out_ref[0, pl.ds(r, 1), :] = v


def _moe_kernel(idx_ref, gate_ref, h2p_ref, x1_ref, gf_ref, wg_ref, wu_ref, wd_ref, out_ref, xg_scr, xs_scr, y_scr):
    e = pl.program_id(1)
    f = pl.program_id(2)
    nf = EXPERT_FF // TILE_F
    step = e * nf + f
    rq = CAPACITY // nf
    half = D_MODEL // 2
    cur = e % 2
    y_cur, y_nxt = y_scr.at[cur], y_scr.at[1 - cur]

    @pl.when(step == 0)
    def _():
        out_ref[...] = jnp.zeros_like(out_ref)
        y_scr[...] = jnp.zeros_like(y_scr)

        def gather(i, _):
            base = pl.multiple_of(i * 32, 32)
            _gather_rows(idx_ref, base, h2p_ref, xg_scr, base, 32)
            return 0

        lax.fori_loop(0, CAPACITY // 32, gather, 0)

    @pl.when(step < SEQ // TILE_X)
    def _():
        rows = pl.ds(pl.multiple_of(step * TILE_X, TILE_X), TILE_X)
        out_ref[0, rows, :] = out_ref[0, rows, :] + x1_ref[0]

    @pl.when(f == 0)
    def _():
        w = xg_scr[...]
        for i in range(2):
            xs_scr[:, i * half:(i + 1) * half] = pltpu.unpack_elementwise(
                w, index=i, packed_dtype=BF16, unpacked_dtype=F32).astype(BF16)

    qbase = pl.multiple_of(f * rq, rq)
    e_prev = (e + N_EXPERTS - 1) % N_EXPERTS
    e_next = (e + 1) % N_EXPERTS
    scale = jnp.where(e >= 1, 1.0, 0.0).astype(F32)
    _scatter_rows(idx_ref, gate_ref, e_prev * CAPACITY + qbase, scale, y_nxt, qbase, out_ref, rq)
    _gather_rows(idx_ref, e_next * CAPACITY + qbase, h2p_ref, xg_scr, qbase, rq)

    xs = xs_scr[...]
    y = jnp.where(f > 0, y_cur[...], 0.0)
    for c0 in range(0, TILE_F, MOE_FCHUNK):
        cs = slice(c0, c0 + MOE_FCHUNK)
        a = jnp.dot(xs, wg_ref[0, :, cs], preferred_element_type=F32)
        u = jnp.dot(xs, wu_ref[0, :, cs], preferred_element_type=F32)
        hm = (a * jax.nn.sigmoid(a) * u).astype(BF16)
        y = y + jnp.dot(hm, wd_ref[0, cs, :], preferred_element_type=F32)
    y_cur[...] = y

    @pl.when((e == N_EXPERTS - 1) & (f == nf - 1))
    def _():
        def scatter(i, _):
            base = pl.multiple_of(i * 32, 32)
            _scatter_rows(idx_ref, gate_ref, e * CAPACITY + base, jnp.float32(1.0), y_cur, base, out_ref, 32)
            return 0

        lax.fori_loop(0, CAPACITY // 32, scatter, 0)

        def norm(i, _):
            rows = pl.ds(pl.multiple_of(i * TILE_X, TILE_X), TILE_X)
            out_ref[0, rows, :] = _rms(out_ref[0, rows, :], gf_ref[...])
            return 0

        lax.fori_loop(0, SEQ // TILE_X, norm, 0)


def _moe(idx, gate, h2p, x1, gf, wg, wu, wd):
    nf = EXPERT_FF // TILE_F
    nx = SEQ // TILE_X
    assert nx <= N_EXPERTS * nf
    sspec = pl.BlockSpec((1, 1, N_EXPERTS * CAPACITY), lambda b, e, f: (b, 0, 0), memory_space=pltpu.SMEM)
    return pl.pallas_call(
        _moe_kernel,
        grid=(BATCH, N_EXPERTS, nf),
        in_specs=[sspec, sspec,
                  pl.BlockSpec((1, SEQ, D_MODEL // 2), lambda b, e, f: (b, 0, 0), pipeline_mode=pl.Buffered(1)),
                  pl.BlockSpec((1, TILE_X, D_MODEL), lambda b, e, f: (b, jnp.minimum(e * nf + f, nx - 1), 0)),
                  pl.BlockSpec((1, D_MODEL), lambda b, e, f: (0, 0)),
                  pl.BlockSpec((1, D_MODEL, TILE_F), lambda b, e, f: (e, 0, f)),
                  pl.BlockSpec((1, D_MODEL, TILE_F), lambda b, e, f: (e, 0, f)),
                  pl.BlockSpec((1, TILE_F, D_MODEL), lambda b, e, f: (e, f, 0))],
        out_specs=pl.BlockSpec((1, SEQ, D_MODEL), lambda b, e, f: (b, 0, 0), pipeline_mode=pl.Buffered(1)),
        out_shape=jax.ShapeDtypeStruct((BATCH, SEQ, D_MODEL), F32),
        scratch_shapes=[pltpu.VMEM((CAPACITY, D_MODEL // 2), U32),
                        pltpu.VMEM((CAPACITY, D_MODEL), BF16),
                        pltpu.VMEM((2, CAPACITY, D_MODEL), F32)],
        compiler_params=pltpu.CompilerParams(
            dimension_semantics=("parallel", "arbitrary", "arbitrary"), vmem_limit_bytes=VMEM_LIMIT),
        name="moe",
    )(idx.reshape(BATCH, 1, N_EXPERTS * CAPACITY), gate.reshape(BATCH, 1, N_EXPERTS * CAPACITY), h2p, x1, gf,
      wg, wu, wd)


def kernel(x, norm1_g, w_in, b_gate, gmlp_ln_g, gmlp_ln_b, gmlp_w_s, gmlp_b_s, s5_lam_re, s5_lam_im, s5_log_dt,
           s5_b_re, s5_b_im, s5_c_re, s5_c_im, s5_d, s5_w_glu, s5_b_glu, w_up_a, w_up_b, w_out, norm2_g,
           w_router, w_gate, w_up, w_down, final_g):
    assert x.shape == (BATCH, SEQ, D_MODEL) and w_in.shape[0] == 1
    l = 0
    row = lambda a: a.astype(F32).reshape(1, -1)

    w_in_b = w_in[l].astype(BF16)
    m_mat, wsum, wc, a_re, a_im = _s5_tables(s5_lam_re[l], s5_lam_im[l], s5_log_dt[l], s5_b_re[l], s5_b_im[l],
                                             s5_c_re[l], s5_c_im[l], s5_d[l])
    g1 = row(norm1_g[l])

    z, s_re, s_im = _s5_in(x, g1, w_in_b, wsum)
    shape3 = (S5_NCH, S5_GROUPS, BATCH * LANES)
    xc_re, xc_im = _s5_scan(s_re.reshape(shape3), s_im.reshape(shape3),
                            jnp.tile(a_re, (1, BATCH)), jnp.tile(a_im, (1, BATCH)))
    shape2 = (S5_NCH * S5_GROUPS, BATCH * LANES)
    y = _s5_out(z, xc_re.reshape(shape2), xc_im.reshape(shape2), m_mat, wc)

    wr_t = jnp.transpose(w_router[l].astype(F32))
    wr_hi = wr_t.astype(BF16)
    wr_lo = (wr_t - wr_hi.astype(F32)).astype(BF16)
    consts = (
        g1, w_in_b, row(b_gate[l][:D_MODEL]), row(b_gate[l][D_MODEL:]), row(gmlp_ln_g[l]), row(gmlp_ln_b[l]),
        gmlp_w_s[l].astype(BF16), jnp.repeat(jnp.transpose(gmlp_b_s[l].astype(F32)), GMLP_HEAD, axis=1),
        s5_w_glu[l].astype(BF16), row(s5_b_glu[l]), w_up_a[l].astype(BF16), w_up_b[l].astype(BF16),
        w_out[l].astype(BF16), row(norm2_g[l]), jnp.concatenate([wr_hi, wr_lo], axis=0),
    )
    x1, h2p, probs = _mixer(x, y, consts)

    tri = (jnp.arange(LANES)[:, None] <= jnp.arange(LANES)[None, :]).astype(BF16)
    idx, gate = _topk(probs.reshape(BATCH * N_EXPERTS, SEQ), tri)
    return _moe(idx, gate, h2p, x1, row(final_g),
                w_gate[l].astype(BF16), w_up[l].astype(BF16), w_down[l].astype(BF16))
```

```python
import jax
import jax.numpy as jnp
import numpy as np
from jax import lax
from jax.experimental import pallas as pl
from jax.experimental.pallas import tpu as pltpu

F32 = jnp.float32
BF16 = jnp.bfloat16
I32 = jnp.int32
U32 = jnp.uint32

D_MODEL = 1024
BATCH = 16
SEQ = 4096
GMLP_WIDTH = 512
GMLP_GROUPS = 8
GMLP_HEAD = GMLP_WIDTH // GMLP_GROUPS
CHUNK = 128
S5_WIDTH = 256
S5_GROUP_CH = 16
S5_GROUPS = S5_WIDTH // S5_GROUP_CH
S5_STATE = 64
N_EXPERTS = 16
CAPACITY = 2 * SEQ // N_EXPERTS
EXPERT_FF = 2048
EPS = 1e-6
O_S5 = 2 * GMLP_WIDTH
O_GATE = O_S5 + S5_WIDTH

LANES = 128
S5_TC = 16
S5_NCH = SEQ // S5_TC
TILE_A = 1024
TILE_M = 1024
MIX_COLS = 256
TILE_F = 1024
TILE_X = 256
SCAN_LANES = 256
VMEM_LIMIT = 56 * 1024 * 1024

_NT = (((1,), (1,)), ((), ()))


def _gelu(x):
    return 0.5 * x * (1.0 + lax.erf(x * (2.0 ** -0.5)))


def _rms(x, g):
    return x * lax.rsqrt(jnp.mean(x * x, axis=-1, keepdims=True) + EPS) * g


def _block_rotate(arrs):
    n = len(arrs)
    blk = lax.broadcasted_iota(I32, arrs[0].shape, 1) // S5_GROUP_CH
    w = [arrs[(-i) % n] for i in range(n)]
    for s in range(n.bit_length() - 1):
        bit = ((blk >> s) & 1) == 1
        w = [jnp.where(bit, w[(i - (1 << s)) % n], w[i]) for i in range(n)]
    return w


def _cpow(er, ei, k):
    mag = jnp.exp(k * er)
    return mag * jnp.cos(k * ei), mag * jnp.sin(k * ei)


def _cmul(a, b):
    return a[0] * b[0] - a[1] * b[1], a[0] * b[1] + a[1] * b[0]


def _s5_tables(lam_re, lam_im, log_dt, b_re, b_im, c_re, c_im, d_skip):
    hp = lax.Precision.HIGH
    tc, ng, nh = S5_TC, S5_GROUPS, S5_GROUP_CH
    lr, li = lam_re.astype(F32), lam_im.astype(F32)
    dt = jnp.exp(log_dt.astype(F32))[..., None]
    er, ei = lr * dt, li * dt
    lb = _cpow(er, ei, 1.0)
    den = lr * lr + li * li
    q = (((lb[0] - 1.0) * lr + lb[1] * li) / den, (lb[1] * lr - (lb[0] - 1.0) * li) / den)
    bbar = _cmul((q[0][..., None], q[1][..., None]), (b_re.astype(F32), b_im.astype(F32)))
    cm = (c_re.astype(F32), c_im.astype(F32))

    kk = np.arange(tc, dtype=np.float32)[:, None, None, None]
    pw = _cpow(er[None], ei[None], kk)
    t = _cmul((cm[0][None], cm[1][None]), (pw[0][:, :, :, None, :], pw[1][:, :, :, None, :]))
    taps = (jnp.einsum('kdghp,dgpi->kdghi', t[0], bbar[0], precision=hp)
            - jnp.einsum('kdghp,dgpi->kdghi', t[1], bbar[1], precision=hp))

    tau = (np.arange(tc)[None, :] - np.arange(ng)[:, None]) % tc
    diff = tau[:, None, :] - tau[:, :, None]
    oh_f = (diff[..., None] == np.arange(tc)).astype(np.float32)
    oh_r = (-diff[..., None] == np.arange(tc)).astype(np.float32)
    m = (jnp.einsum('gabk,kghi->gaibh', oh_f, taps[:, 0], precision=hp)
         + jnp.einsum('gabk,kghi->gaibh', oh_r, taps[:, 1], precision=hp))
    eye_j = np.eye(tc, dtype=np.float32)[None, :, None, :, None]
    eye_h = np.eye(nh, dtype=np.float32)[None, None, :, None, :]
    m = (m + eye_j * eye_h * d_skip.astype(F32)[:, None, None, None, :]).reshape(ng, tc * nh, tc * nh)

    tauf = tau.astype(np.float32)
    bt = (jnp.transpose(bbar[0], (0, 1, 3, 2)), jnp.transpose(bbar[1], (0, 1, 3, 2)))
    parts = []
    for d, kin in ((0, tc - 1.0 - tauf), (1, tauf)):
        p_in = _cpow(er[d][:, None, None, :], ei[d][:, None, None, :], kin[:, :, None, None])
        parts.append(_cmul(p_in, (bt[0][d][:, None], bt[1][d][:, None])))
    wsum = jnp.concatenate([parts[0][0], parts[1][0], parts[0][1], parts[1][1]], axis=-1)
    wsum = wsum.reshape(ng, tc * nh, 4 * S5_STATE)

    ct = (jnp.transpose(cm[0], (0, 1, 3, 2)), jnp.transpose(cm[1], (0, 1, 3, 2)))
    parts = []
    for d, kout in ((0, tauf + 1.0), (1, tc - tauf)):
        p_out = _cpow(er[d][:, :, None, None], ei[d][:, :, None, None], kout[:, None, :, None])
        parts.append(_cmul(p_out, (ct[0][d][:, :, None, :], ct[1][d][:, :, None, :])))
    wc = jnp.concatenate([parts[0][0], parts[1][0], -parts[0][1], -parts[1][1]], axis=1)
    wc = wc.reshape(ng, 4 * S5_STATE, tc * nh)

    a16 = _cpow(er, ei, float(tc))
    a_re = jnp.concatenate([a16[0][0], a16[0][1]], axis=-1)
    a_im = jnp.concatenate([a16[1][0], a16[1][1]], axis=-1)
    return m.astype(BF16), wsum.astype(BF16), wc.astype(BF16), a_re, a_im


def _s5_in_kernel(x_ref, g1_ref, ws5_ref, wsum_ref, z_ref, sre_ref, sim_ref, s_scr):
    n = TILE_A // S5_TC
    hn = _rms(x_ref[0], g1_ref[...]).astype(BF16)
    s = jnp.dot(hn, ws5_ref[...], preferred_element_type=F32)
    s_scr[0] = s[:, :LANES]
    s_scr[1] = s[:, LANES:]
    rows = []
    for t in range(S5_TC):
        a = jnp.concatenate([s_scr[0, pl.ds(t, n, stride=S5_TC), :],
                             s_scr[1, pl.ds(t, n, stride=S5_TC), :]], axis=1)
        rows.append(pltpu.roll(a, S5_GROUP_CH * t, 1) if t else a)
    zs = _block_rotate(rows)
    for g in range(S5_GROUPS):
        zb = zs[g].astype(BF16)
        z_ref[g] = zb
        sm = jnp.dot(zb, wsum_ref[g], preferred_element_type=F32)
        sre_ref[pl.ds(g, n, stride=S5_GROUPS), :] = sm[:, :LANES]
        sim_ref[pl.ds(g, n, stride=S5_GROUPS), :] = sm[:, LANES:]


def _s5_in(x, g1, ws5, wsum):
    nt = SEQ // TILE_A
    n = TILE_A // S5_TC
    return pl.pallas_call(
        _s5_in_kernel,
        grid=(BATCH, nt),
        in_specs=[
            pl.BlockSpec((1, TILE_A, D_MODEL), lambda b, i: (b, i, 0)),
            pl.BlockSpec((1, D_MODEL), lambda b, i: (0, 0)),
            pl.BlockSpec((D_MODEL, S5_WIDTH), lambda b, i: (0, O_S5 // S5_WIDTH)),
            pl.BlockSpec((S5_GROUPS, S5_WIDTH, S5_WIDTH), lambda b, i: (0, 0, 0)),
        ],
        out_specs=[
            pl.BlockSpec((S5_GROUPS, n, S5_WIDTH), lambda b, i: (0, b * nt + i, 0)),
            pl.BlockSpec((n * S5_GROUPS, LANES), lambda b, i: (i, b)),
            pl.BlockSpec((n * S5_GROUPS, LANES), lambda b, i: (i, b)),
        ],
        out_shape=[
            jax.ShapeDtypeStruct((S5_GROUPS, BATCH * S5_NCH, S5_WIDTH), BF16),
            jax.ShapeDtypeStruct((S5_NCH * S5_GROUPS, BATCH * LANES), F32),
            jax.ShapeDtypeStruct((S5_NCH * S5_GROUPS, BATCH * LANES), F32),
        ],
        scratch_shapes=[pltpu.VMEM((2, TILE_A, LANES), F32)],
        compiler_params=pltpu.CompilerParams(
            dimension_semantics=("parallel", "parallel"), vmem_limit_bytes=VMEM_LIMIT),
        name="s5_in",
    )(x, g1, ws5, wsum)


def _s5_scan_kernel(sre_ref, sim_ref, are_ref, aim_ref, xre_ref, xim_ref):
    ar = are_ref[...]
    ai = aim_ref[...]
    fwd = (lax.broadcasted_iota(I32, ar.shape, 1) % LANES) < S5_STATE
    zero = jnp.zeros_like(ar)

    def step(c, xr, xi):
        sr = sre_ref[c]
        si = sim_ref[c]
        return ar * xr - ai * xi + sr, ar * xi + ai * xr + si

    def fwd_body(c, carry):
        xr, xi = carry
        xre_ref[c] = xr
        xim_ref[c] = xi
        return step(c, xr, xi)

    lax.fori_loop(0, S5_NCH, fwd_body, (zero, zero))

    def rev_body(i, carry):
        xr, xi = carry
        c = S5_NCH - 1 - i
        xre_ref[c] = jnp.where(fwd, xre_ref[c], xr)
        xim_ref[c] = jnp.where(fwd, xim_ref[c], xi)
        return step(c, xr, xi)

    lax.fori_loop(0, S5_NCH, rev_body, (zero, zero))


def _s5_scan(s_re, s_im, a_re, a_im):
    nl = BATCH * LANES
    blk3 = pl.BlockSpec((S5_NCH, S5_GROUPS, SCAN_LANES), lambda j: (0, 0, j))
    blk2 = pl.BlockSpec((S5_GROUPS, SCAN_LANES), lambda j: (0, j))
    return pl.pallas_call(
        _s5_scan_kernel,
        grid=(nl // SCAN_LANES,),
        in_specs=[blk3, blk3, blk2, blk2],
        out_specs=[blk3, blk3],
        out_shape=[jax.ShapeDtypeStruct((S5_NCH, S5_GROUPS, nl), F32)] * 2,
        compiler_params=pltpu.CompilerParams(
            dimension_semantics=("parallel",), vmem_limit_bytes=VMEM_LIMIT),
        name="s5_scan",
    )(s_re, s_im, a_re, a_im)


def _s5_out_kernel(z_ref, xre_ref, xim_ref, m_ref, wc_ref, y_ref):
    n = S5_NCH
    ys = []
    for g in range(S5_GROUPS):
        xc = jnp.concatenate([xre_ref[pl.ds(g, n, stride=S5_GROUPS), :],
                              xim_ref[pl.ds(g, n, stride=S5_GROUPS), :]], axis=1).astype(BF16)
        ys.append(jnp.dot(z_ref[g], m_ref[g], preferred_element_type=F32)
                  + jnp.dot(xc, wc_ref[g], preferred_element_type=F32))
    ts = _block_rotate(ys)
    for t in range(S5_TC):
        acc = ts[t]
        if t:
            acc = pltpu.roll(acc, S5_WIDTH - S5_GROUP_CH * t, 1)
        y_ref[0, 0, pl.ds(t, n, stride=S5_TC), :] = acc[:, :LANES]
        y_ref[0, 1, pl.ds(t, n, stride=S5_TC), :] = acc[:, LANES:]


def _s5_out(z, xc_re, xc_im, m, wc):
    wspec = pl.BlockSpec((S5_GROUPS, S5_WIDTH, S5_WIDTH), lambda b: (0, 0, 0))
    xspec = pl.BlockSpec((S5_NCH * S5_GROUPS, LANES), lambda b: (0, b))
    return pl.pallas_call(
        _s5_out_kernel,
        grid=(BATCH,),
        in_specs=[pl.BlockSpec((S5_GROUPS, S5_NCH, S5_WIDTH), lambda b: (0, b, 0)), xspec, xspec, wspec, wspec],
        out_specs=pl.BlockSpec((1, 2, SEQ, LANES), lambda b: (b, 0, 0, 0)),
        out_shape=jax.ShapeDtypeStruct((BATCH, 2, SEQ, LANES), F32),
        compiler_params=pltpu.CompilerParams(
            dimension_semantics=("parallel",), vmem_limit_bytes=VMEM_LIMIT),
        name="s5_out",
    )(z, xc_re, xc_im, m, wc)


def _mixer_kernel(x_ref, y_ref, g1_ref, win_ref, bga_ref, bgb_ref, lng_ref, lnb_ref,
                  ws_ref, bs_ref, wglu_ref, bglu_ref, wupa_ref, wupb_ref, wout_ref, g2_ref, wr_ref,
                  x1_ref, h2p_ref, pr_ref, bra_scr, mrg_scr):
    x = x_ref[0]
    hn = _rms(x, g1_ref[...]).astype(BF16)

    def in_proj(c0):
        return jnp.dot(hn, win_ref[:, c0:c0 + MIX_COLS], preferred_element_type=F32)

    nuv = GMLP_WIDTH // MIX_COLS
    us = [_gelu(in_proj(c * MIX_COLS)) for c in range(nuv)]
    vs = [_gelu(in_proj(GMLP_WIDTH + c * MIX_COLS)) for c in range(nuv)]
    mu = sum(jnp.sum(v, axis=-1, keepdims=True) for v in vs) * (1.0 / GMLP_WIDTH)
    vcs = [v - mu for v in vs]
    var = sum(jnp.sum(vc * vc, axis=-1, keepdims=True) for vc in vcs) * (1.0 / GMLP_WIDTH)
    rstd = lax.rsqrt(var + EPS)
    vns = [(vc * rstd * lng_ref[:, c * MIX_COLS:(c + 1) * MIX_COLS]
            + lnb_ref[:, c * MIX_COLS:(c + 1) * MIX_COLS]).astype(BF16) for c, vc in enumerate(vcs)]
    low = lax.broadcasted_iota(I32, (CHUNK, LANES), 1) < GMLP_HEAD
    per = MIX_COLS // LANES
    for ch in range(TILE_M // CHUNK):
        r0 = ch * CHUNK
        for q in range(GMLP_WIDTH // LANES):
            c0 = q * LANES
            l0 = (q % per) * LANES
            vb = vns[q // per][r0:r0 + CHUNK, l0:l0 + LANES]
            z0 = jnp.dot(ws_ref[2 * q], vb, preferred_element_type=F32)
            z1 = jnp.dot(ws_ref[2 * q + 1], vb, preferred_element_type=F32)
            zz = jnp.where(low, z0, z1) + bs_ref[:, c0:c0 + LANES]
            bra_scr[r0:r0 + CHUNK, c0:c0 + LANES] = (us[q // per][r0:r0 + CHUNK, l0:l0 + LANES] * zz).astype(BF16)

    yb = _gelu(jnp.concatenate([y_ref[0, 0], y_ref[0, 1]], axis=1))
    gl = jnp.dot(yb.astype(BF16), wglu_ref[...], preferred_element_type=F32) + bglu_ref[...]
    brb = (yb * jax.nn.sigmoid(gl)).astype(BF16)

    bra = bra_scr[...]
    for c in range(D_MODEL // MIX_COLS):
        cs = slice(c * MIX_COLS, (c + 1) * MIX_COLS)
        ga = jax.nn.sigmoid(in_proj(O_GATE + c * MIX_COLS) + bga_ref[:, cs])
        gb = jax.nn.sigmoid(in_proj(O_GATE + D_MODEL + c * MIX_COLS) + bgb_ref[:, cs])
        mrg_scr[:, cs] = (ga * jnp.dot(bra, wupa_ref[:, cs], preferred_element_type=F32)
                          + gb * jnp.dot(brb, wupb_ref[:, cs], preferred_element_type=F32)).astype(BF16)
    mrg = mrg_scr[...]
    ssq = jnp.zeros((TILE_M, 1), F32)
    for c in range(D_MODEL // MIX_COLS):
        cs = slice(c * MIX_COLS, (c + 1) * MIX_COLS)
        x1c = x[:, cs] + jnp.dot(mrg, wout_ref[:, cs], preferred_element_type=F32)
        x1_ref[0, :, cs] = x1c
        ssq = ssq + jnp.sum(x1c * x1c, axis=-1, keepdims=True)

    h2 = x1_ref[0] * lax.rsqrt(ssq * (1.0 / D_MODEL) + EPS) * g2_ref[...]
    half = D_MODEL // 2
    h2p_ref[0] = pltpu.pack_elementwise([h2[:, :half], h2[:, half:]], packed_dtype=BF16)

    lg = lax.dot_general(wr_ref[...], h2.astype(BF16), _NT, preferred_element_type=F32)
    logits = lg[:N_EXPERTS] + lg[N_EXPERTS:]
    mx = jnp.max(logits, axis=0, keepdims=True)
    ex = jnp.exp(logits - mx)
    pr_ref[0] = ex / jnp.sum(ex, axis=0, keepdims=True)


def _mixer(x, y, consts):
    nt = SEQ // TILE_M

    def cspec(a):
        nd = a.ndim
        return pl.BlockSpec(a.shape, lambda b, i: (0,) * nd, pipeline_mode=pl.Buffered(1))

    return pl.pallas_call(
        _mixer_kernel,
        grid=(BATCH, nt),
        in_specs=[pl.BlockSpec((1, TILE_M, D_MODEL), lambda b, i: (b, i, 0)),
                  pl.BlockSpec((1, 2, TILE_M, LANES), lambda b, i: (b, 0, i, 0))] + [cspec(a) for a in consts],
        out_specs=[pl.BlockSpec((1, TILE_M, D_MODEL), lambda b, i: (b, i, 0)),
                   pl.BlockSpec((1, TILE_M, D_MODEL // 2), lambda b, i: (b, i, 0)),
                   pl.BlockSpec((1, N_EXPERTS, TILE_M), lambda b, i: (b, 0, i))],
        out_shape=[jax.ShapeDtypeStruct((BATCH, SEQ, D_MODEL), F32),
                   jax.ShapeDtypeStruct((BATCH, SEQ, D_MODEL // 2), U32),
                   jax.ShapeDtypeStruct((BATCH, N_EXPERTS, SEQ), F32)],
        scratch_shapes=[pltpu.VMEM((TILE_M, GMLP_WIDTH), BF16), pltpu.VMEM((TILE_M, D_MODEL), BF16)],
        compiler_params=pltpu.CompilerParams(
            dimension_semantics=("parallel", "parallel"), vmem_limit_bytes=VMEM_LIMIT),
        name="mixer",
    )(x, y, *consts)


TOPK_NBLK = SEQ // LANES
TOPK_ROWS = 4 * N_EXPERTS
TOPK_SEARCH_STEPS = 28


def _lane_sum(x):
    return jnp.sum(x, axis=1, keepdims=True)


def _topk_kernel(p_ref, tri_ref, idx_ref, gate_ref, eqc_scr, loc_scr, pj_scr, cb_scr, cbp_scr):
    p = p_ref[...]

    def count_ge(th):
        return _lane_sum(jnp.where(p >= th, 1.0, 0.0))

    def search(_, c):
        lo, hi = c
        w = hi - lo
        m1, m2, m3 = lo + 0.25 * w, lo + 0.5 * w, lo + 0.75 * w
        c1, c2, c3 = count_ge(m1) >= CAPACITY, count_ge(m2) >= CAPACITY, count_ge(m3) >= CAPACITY
        return (jnp.where(c3, m3, jnp.where(c2, m2, jnp.where(c1, m1, lo))),
                jnp.where(c1, jnp.where(c2, jnp.where(c3, hi, m3), m2), m1))

    lo, hi = lax.fori_loop(0, TOPK_SEARCH_STEPS, search,
                           (jnp.zeros((TOPK_ROWS, 1), F32), jnp.full((TOPK_ROWS, 1), 2.0, F32)))
    gt = p >= hi
    eq = (p >= lo) & (p < hi)
    need = CAPACITY - _lane_sum(jnp.where(gt, 1.0, 0.0))
    tri = tri_ref[...]
    eqf = jnp.where(eq, 1.0, 0.0)
    carry = jnp.zeros((TOPK_ROWS, 1), F32)
    for j in range(TOPK_NBLK):
        sl = slice(j * LANES, (j + 1) * LANES)
        cs = jnp.dot(eqf[:, sl].astype(BF16), tri, preferred_element_type=F32) + carry
        eqc_scr[:, sl] = cs
        carry = cs[:, LANES - 1:LANES]
    sel = gt | (eq & ((eqc_scr[...] - eqf) < need))
    self = jnp.where(sel, 1.0, 0.0)

    lane = lax.broadcasted_iota(I32, (TOPK_ROWS, LANES), 1)
    cb = jnp.full((TOPK_ROWS, LANES), float(2 * SEQ), F32)
    cbp = cb
    carry = jnp.zeros((TOPK_ROWS, 1), F32)
    for j in range(TOPK_NBLK):
        sl = slice(j * LANES, (j + 1) * LANES)
        loc = jnp.dot(self[:, sl].astype(BF16), tri, preferred_element_type=F32)
        loc_scr[j * TOPK_ROWS:(j + 1) * TOPK_ROWS, :] = loc
        pj_scr[j * TOPK_ROWS:(j + 1) * TOPK_ROWS, :] = p[:, sl]
        cbp = jnp.where(lane == j, carry, cbp)
        carry = carry + loc[:, LANES - 1:LANES]
        cb = jnp.where(lane == j, carry, cb)
    padr = jnp.zeros((LANES - TOPK_ROWS, LANES), F32)
    cb_scr[...] = jnp.concatenate([cb, padr], axis=0).T
    cbp_scr[...] = jnp.concatenate([cbp, padr], axis=0).T

    slot = (lax.broadcasted_iota(I32, (1, CAPACITY), 1) + 1).astype(F32)
    lane_r = lax.broadcasted_iota(I32, (LANES, LANES), 1)
    sub_f = lax.broadcasted_iota(I32, (LANES, 1), 0).astype(F32)
    pad = jnp.zeros((LANES - TOPK_NBLK, LANES), F32)

    def block_rows_t(ref, e):
        return jnp.concatenate([ref[pl.ds(e, TOPK_NBLK, stride=TOPK_ROWS), :], pad], axis=0).T

    def sub_sum(x):
        return jnp.sum(x, axis=0, keepdims=True)

    def per_expert(e, _):
        cb_e = _lane_sum(jnp.where(lane_r == e, cb_scr[...], 0.0))
        cbp_e = _lane_sum(jnp.where(lane_r == e, cbp_scr[...], 0.0))
        hit = (cbp_e < slot) & (slot <= cb_e)
        hitb = jnp.where(hit, 1.0, 0.0).astype(BF16)
        both = sub_sum(jnp.where(hit, cbp_e * float(LANES) + sub_f, 0.0))
        before = jnp.floor(both * (1.0 / LANES))
        jblk = both - before * float(LANES)
        rank = slot - before
        loc = jnp.dot(block_rows_t(loc_scr, e).astype(BF16), hitb, preferred_element_type=F32)
        off = sub_sum(jnp.where(loc < rank, 1.0, 0.0))
        pe = block_rows_t(pj_scr, e)
        hi = pe.astype(BF16)
        r1 = pe - hi.astype(F32)
        mid = r1.astype(BF16)
        lo = (r1 - mid.astype(F32)).astype(BF16)
        pg = (jnp.dot(hi, hitb, preferred_element_type=F32) + jnp.dot(mid, hitb, preferred_element_type=F32)
              + jnp.dot(lo, hitb, preferred_element_type=F32))
        idx_ref[pl.ds(e, 1), :] = (jblk * float(LANES) + off).astype(I32)
        gate_ref[pl.ds(e, 1), :] = sub_sum(jnp.where(sub_f == off, pg, 0.0))
        return 0

    lax.fori_loop(0, TOPK_ROWS, per_expert, 0, unroll=4)


def _topk(probs, tri):
    nrow = BATCH * N_EXPERTS
    return pl.pallas_call(
        _topk_kernel,
        grid=(nrow // TOPK_ROWS,),
        in_specs=[pl.BlockSpec((TOPK_ROWS, SEQ), lambda i: (i, 0)),
                  pl.BlockSpec((LANES, LANES), lambda i: (0, 0))],
        out_specs=[pl.BlockSpec((TOPK_ROWS, CAPACITY), lambda i: (i, 0)),
                   pl.BlockSpec((TOPK_ROWS, CAPACITY), lambda i: (i, 0))],
        out_shape=[jax.ShapeDtypeStruct((nrow, CAPACITY), I32),
                   jax.ShapeDtypeStruct((nrow, CAPACITY), F32)],
        scratch_shapes=[pltpu.VMEM((TOPK_ROWS, SEQ), F32),
                        pltpu.VMEM((TOPK_NBLK * TOPK_ROWS, LANES), F32),
                        pltpu.VMEM((TOPK_NBLK * TOPK_ROWS, LANES), F32),
                        pltpu.VMEM((LANES, LANES), F32),
                        pltpu.VMEM((LANES, LANES), F32)],
        compiler_params=pltpu.CompilerParams(
            dimension_semantics=("parallel",), vmem_limit_bytes=VMEM_LIMIT),
        name="topk",
    )(probs, tri)


MOE_GROUP = 8
MOE_FCHUNK = 512


def _gather_rows(idx_ref, ibase, h2p_ref, dst_ref, dbase, nrows):
    for k in range(nrows):
        r = idx_ref[0, 0, ibase + k]
        dst_ref[pl.ds(dbase + k, 1), :] = h2p_ref[0, pl.ds(r, 1), :]


def _scatter_rows(idx_ref, gate_ref, ibase, scale, y_ref, ybase, out_ref, nrows):
    for k0 in range(0, nrows, MOE_GROUP):
        rs, vals = [], []
        for k in range(k0, k0 + MOE_GROUP):
            r = idx_ref[0, 0, ibase + k]
            g = gate_ref[0, 0, ibase + k] * scale
            rs.append(r)
            vals.append(out_ref[0, pl.ds(r, 1), :] + g * y_ref[pl.ds(ybase + k, 1), :])
        for r, v in zip(rs, vals):
            out_ref[0, pl.ds(r, 1), :] = v


def _moe_kernel(idx_ref, gate_ref, h2p_ref, x1_ref, gf_ref, wg_ref, wu_ref, wd_ref, out_ref, xg_scr, xs_scr, y_scr):
    e = pl.program_id(1)
    f = pl.program_id(2)
    nf = EXPERT_FF // TILE_F
    nx = SEQ // TILE_X
    step = e * nf + f
    rq = CAPACITY // nf
    cur = e % 2
    xg_cur, xg_nxt = xg_scr.at[cur], xg_scr.at[1 - cur]
    y_cur, y_nxt = y_scr.at[cur], y_scr.at[1 - cur]

    @pl.when(step == 0)
    def _():
        out_ref[...] = jnp.zeros_like(out_ref)
        y_scr[...] = jnp.zeros_like(y_scr)

        def gather(i, _):
            base = pl.multiple_of(i * 32, 32)
            _gather_rows(idx_ref, base, h2p_ref, xg_cur, base, 32)
            return 0

        lax.fori_loop(0, CAPACITY // 32, gather, 0)

    w = xg_cur[...]
    half = D_MODEL // 2
    for i in range(2):
        xs_scr[:, i * half:(i + 1) * half] = pltpu.unpack_elementwise(
            w, index=i, packed_dtype=BF16, unpacked_dtype=F32).astype(BF16)
    xrows = pl.ds(pl.multiple_of(jnp.minimum(step, nx - 1) * TILE_X, TILE_X), TILE_X)
    out_ref[0, xrows, :] = out_ref[0, xrows, :] + jnp.where(step < nx, x1_ref[0], 0.0)
    qbase = pl.multiple_of(f * rq, rq)
    e_prev = (e + N_EXPERTS - 1) % N_EXPERTS
    e_next = (e + 1) % N_EXPERTS
    scale = jnp.where(e >= 1, 1.0, 0.0).astype(F32)
    _scatter_rows(idx_ref, gate_ref, e_prev * CAPACITY + qbase, scale, y_nxt, qbase, out_ref, rq)
    _gather_rows(idx_ref, e_next * CAPACITY + qbase, h2p_ref, xg_nxt, qbase, rq)

    xs = xs_scr[...]
    y = jnp.where(f > 0, y_cur[...], 0.0)
    for c0 in range(0, TILE_F, MOE_FCHUNK):
        cs = slice(c0, c0 + MOE_FCHUNK)
        a = jnp.dot(xs, wg_ref[0, :, cs], preferred_element_type=F32)
        u = jnp.dot(xs, wu_ref[0, :, cs], preferred_element_type=F32)
        hm = (a * jax.nn.sigmoid(a) * u).astype(BF16)
        y = y + jnp.dot(hm, wd_ref[0, cs, :], preferred_element_type=F32)
    y_cur[...] = y

    @pl.when((e == N_EXPERTS - 1) & (f == nf - 1))
    def _():
        def scatter(i, _):
            base = pl.multiple_of(i * 32, 32)
            _scatter_rows(idx_ref, gate_ref, e * CAPACITY + base, jnp.float32(1.0), y_cur, base, out_ref, 32)
            return 0

        lax.fori_loop(0, CAPACITY // 32, scatter, 0)

        def norm(i, _):
            rows = pl.ds(pl.multiple_of(i * TILE_X, TILE_X), TILE_X)
            out_ref[0, rows, :] = _rms(out_ref[0, rows, :], gf_ref[...])
            return 0

        lax.fori_loop(0, SEQ // TILE_X, norm, 0)


def _moe(idx, gate, h2p, x1, gf, wg, wu, wd):
    nf = EXPERT_FF // TILE_F
    nx = SEQ // TILE_X
    assert nx <= N_EXPERTS * nf
    sspec = pl.BlockSpec((1, 1, N_EXPERTS * CAPACITY), lambda b, e, f: (b, 0, 0), memory_space=pltpu.SMEM)
    return pl.pallas_call(
        _moe_kernel,
        grid=(BATCH, N_EXPERTS, nf),
        in_specs=[sspec, sspec,
                  pl.BlockSpec((1, SEQ, D_MODEL // 2), lambda b, e, f: (b, 0, 0), pipeline_mode=pl.Buffered(1)),
                  pl.BlockSpec((1, TILE_X, D_MODEL), lambda b, e, f: (b, jnp.minimum(e * nf + f, nx - 1), 0)),
                  pl.BlockSpec((1, D_MODEL), lambda b, e, f: (0, 0)),
                  pl.BlockSpec((1, D_MODEL, TILE_F), lambda b, e, f: (e, 0, f)),
                  pl.BlockSpec((1, D_MODEL, TILE_F), lambda b, e, f: (e, 0, f)),
                  pl.BlockSpec((1, TILE_F, D_MODEL), lambda b, e, f: (e, f, 0))],
        out_specs=pl.BlockSpec((1, SEQ, D_MODEL), lambda b, e, f: (b, 0, 0), pipeline_mode=pl.Buffered(1)),
        out_shape=jax.ShapeDtypeStruct((BATCH, SEQ, D_MODEL), F32),
        scratch_shapes=[pltpu.VMEM((2, CAPACITY, D_MODEL // 2), U32),
                        pltpu.VMEM((CAPACITY, D_MODEL), BF16),
                        pltpu.VMEM((2, CAPACITY, D_MODEL), F32)],
        compiler_params=pltpu.CompilerParams(
            dimension_semantics=("parallel", "arbitrary", "arbitrary"), vmem_limit_bytes=VMEM_LIMIT),
        name="moe",
    )(idx.reshape(BATCH, 1, N_EXPERTS * CAPACITY), gate.reshape(BATCH, 1, N_EXPERTS * CAPACITY), h2p, x1, gf,
      wg, wu, wd)


def kernel(x, norm1_g, w_in, b_gate, gmlp_ln_g, gmlp_ln_b, gmlp_w_s, gmlp_b_s, s5_lam_re, s5_lam_im, s5_log_dt,
           s5_b_re, s5_b_im, s5_c_re, s5_c_im, s5_d, s5_w_glu, s5_b_glu, w_up_a, w_up_b, w_out, norm2_g,
           w_router, w_gate, w_up, w_down, final_g):
    assert x.shape == (BATCH, SEQ, D_MODEL) and w_in.shape[0] == 1
    l = 0
    row = lambda a: a.astype(F32).reshape(1, -1)

    w_in_b = w_in[l].astype(BF16)
    m_mat, wsum, wc, a_re, a_im = _s5_tables(s5_lam_re[l], s5_lam_im[l], s5_log_dt[l], s5_b_re[l], s5_b_im[l],
                                             s5_c_re[l], s5_c_im[l], s5_d[l])
    g1 = row(norm1_g[l])

    z, s_re, s_im = _s5_in(x, g1, w_in_b, wsum)
    shape3 = (S5_NCH, S5_GROUPS, BATCH * LANES)
    xc_re, xc_im = _s5_scan(s_re.reshape(shape3), s_im.reshape(shape3),
                            jnp.tile(a_re, (1, BATCH)), jnp.tile(a_im, (1, BATCH)))
    shape2 = (S5_NCH * S5_GROUPS, BATCH * LANES)
    y = _s5_out(z, xc_re.reshape(shape2), xc_im.reshape(shape2), m_mat, wc)

    wr_t = jnp.transpose(w_router[l].astype(F32))
    wr_hi = wr_t.astype(BF16)
    wr_lo = (wr_t - wr_hi.astype(F32)).astype(BF16)
    consts = (
        g1, w_in_b, row(b_gate[l][:D_MODEL]), row(b_gate[l][D_MODEL:]), row(gmlp_ln_g[l]), row(gmlp_ln_b[l]),
        gmlp_w_s[l].astype(BF16), jnp.repeat(jnp.transpose(gmlp_b_s[l].astype(F32)), GMLP_HEAD, axis=1),
        s5_w_glu[l].astype(BF16), row(s5_b_glu[l]), w_up_a[l].astype(BF16), w_up_b[l].astype(BF16),
        w_out[l].astype(BF16), row(norm2_g[l]), jnp.concatenate([wr_hi, wr_lo], axis=0),
    )
    x1, h2p, probs = _mixer(x, y, consts)

    tri = (jnp.arange(LANES)[:, None] <= jnp.arange(LANES)[None, :]).astype(BF16)
    idx, gate = _topk(probs.reshape(BATCH * N_EXPERTS, SEQ), tri)
    return _moe(idx, gate, h2p, x1, row(final_g),
                w_gate[l].astype(BF16), w_up[l].astype(BF16), w_down[l].astype(BF16))
```

```python
import jax
import jax.numpy as jnp
import numpy as np
from jax import lax
from jax.experimental import pallas as pl
from jax.experimental.pallas import tpu as pltpu

F32 = jnp.float32
BF16 = jnp.bfloat16
I32 = jnp.int32
U32 = jnp.uint32

D_MODEL = 1024
BATCH = 16
SEQ = 4096
GMLP_WIDTH = 512
GMLP_GROUPS = 8
GMLP_HEAD = GMLP_WIDTH // GMLP_GROUPS
CHUNK = 128
S5_WIDTH = 256
S5_GROUP_CH = 16
S5_GROUPS = S5_WIDTH // S5_GROUP_CH
S5_STATE = 64
N_EXPERTS = 16
CAPACITY = 2 * SEQ // N_EXPERTS
EXPERT_FF = 2048
EPS = 1e-6
O_S5 = 2 * GMLP_WIDTH
O_GATE = O_S5 + S5_WIDTH

LANES = 128
S5_TC = 16
S5_NCH = SEQ // S5_TC
TILE_A = 1024
TILE_M = 1024
MIX_COLS = 256
TILE_F = 1024
TILE_X = 256
SCAN_LANES = 256
VMEM_LIMIT = 56 * 1024 * 1024
VMEM_LIMIT_MIXER = 60 * 1024 * 1024

_NT = (((1,), (1,)), ((), ()))


def _gelu(x):
    return 0.5 * x * (1.0 + lax.erf(x * (2.0 ** -0.5)))


def _rms(x, g):
    return x * lax.rsqrt(jnp.mean(x * x, axis=-1, keepdims=True) + EPS) * g


def _block_rotate(arrs):
    n = len(arrs)
    blk = lax.broadcasted_iota(I32, arrs[0].shape, 1) // S5_GROUP_CH
    w = [arrs[(-i) % n] for i in range(n)]
    for s in range(n.bit_length() - 1):
        bit = ((blk >> s) & 1) == 1
        w = [jnp.where(bit, w[(i - (1 << s)) % n], w[i]) for i in range(n)]
    return w


def _cpow(er, ei, k):
    mag = jnp.exp(k * er)
    return mag * jnp.cos(k * ei), mag * jnp.sin(k * ei)


def _cmul(a, b):
    return a[0] * b[0] - a[1] * b[1], a[0] * b[1] + a[1] * b[0]


def _s5_tables(lam_re, lam_im, log_dt, b_re, b_im, c_re, c_im, d_skip):
    hp = lax.Precision.HIGH
    tc, ng, nh = S5_TC, S5_GROUPS, S5_GROUP_CH
    lr, li = lam_re.astype(F32), lam_im.astype(F32)
    dt = jnp.exp(log_dt.astype(F32))[..., None]
    er, ei = lr * dt, li * dt
    lb = _cpow(er, ei, 1.0)
    den = lr * lr + li * li
    q = (((lb[0] - 1.0) * lr + lb[1] * li) / den, (lb[1] * lr - (lb[0] - 1.0) * li) / den)
    bbar = _cmul((q[0][..., None], q[1][..., None]), (b_re.astype(F32), b_im.astype(F32)))
    cm = (c_re.astype(F32), c_im.astype(F32))

    kk = np.arange(tc, dtype=np.float32)[:, None, None, None]
    pw = _cpow(er[None], ei[None], kk)
    t = _cmul((cm[0][None], cm[1][None]), (pw[0][:, :, :, None, :], pw[1][:, :, :, None, :]))
    taps = (jnp.einsum('kdghp,dgpi->kdghi', t[0], bbar[0], precision=hp)
            - jnp.einsum('kdghp,dgpi->kdghi', t[1], bbar[1], precision=hp))

    tau = (np.arange(tc)[None, :] - np.arange(ng)[:, None]) % tc
    diff = tau[:, None, :] - tau[:, :, None]
    oh_f = (diff[..., None] == np.arange(tc)).astype(np.float32)
    oh_r = (-diff[..., None] == np.arange(tc)).astype(np.float32)
    m = (jnp.einsum('gabk,kghi->gaibh', oh_f, taps[:, 0], precision=hp)
         + jnp.einsum('gabk,kghi->gaibh', oh_r, taps[:, 1], precision=hp))
    eye_j = np.eye(tc, dtype=np.float32)[None, :, None, :, None]
    eye_h = np.eye(nh, dtype=np.float32)[None, None, :, None, :]
    m = (m + eye_j * eye_h * d_skip.astype(F32)[:, None, None, None, :]).reshape(ng, tc * nh, tc * nh)

    tauf = tau.astype(np.float32)
    bt = (jnp.transpose(bbar[0], (0, 1, 3, 2)), jnp.transpose(bbar[1], (0, 1, 3, 2)))
    parts = []
    for d, kin in ((0, tc - 1.0 - tauf), (1, tauf)):
        p_in = _cpow(er[d][:, None, None, :], ei[d][:, None, None, :], kin[:, :, None, None])
        parts.append(_cmul(p_in, (bt[0][d][:, None], bt[1][d][:, None])))
    wsum = jnp.concatenate([parts[0][0], parts[1][0], parts[0][1], parts[1][1]], axis=-1)
    wsum = wsum.reshape(ng, tc * nh, 4 * S5_STATE)

    ct = (jnp.transpose(cm[0], (0, 1, 3, 2)), jnp.transpose(cm[1], (0, 1, 3, 2)))
    parts = []
    for d, kout in ((0, tauf + 1.0), (1, tc - tauf)):
        p_out = _cpow(er[d][:, :, None, None], ei[d][:, :, None, None], kout[:, None, :, None])
        parts.append(_cmul(p_out, (ct[0][d][:, :, None, :], ct[1][d][:, :, None, :])))
    wc = jnp.concatenate([parts[0][0], parts[1][0], -parts[0][1], -parts[1][1]], axis=1)
    wc = wc.reshape(ng, 4 * S5_STATE, tc * nh)

    a16 = _cpow(er, ei, float(tc))
    a_re = jnp.concatenate([a16[0][0], a16[0][1]], axis=-1)
    a_im = jnp.concatenate([a16[1][0], a16[1][1]], axis=-1)
    return m.astype(BF16), wsum.astype(BF16), wc.astype(BF16), a_re, a_im


def _s5_in_kernel(x_ref, g1_ref, ws5_ref, wsum_ref, z_ref, sre_ref, sim_ref, s_scr):
    n = TILE_A // S5_TC
    hn = _rms(x_ref[0], g1_ref[...]).astype(BF16)
    s = jnp.dot(hn, ws5_ref[...], preferred_element_type=F32)
    s_scr[0] = s[:, :LANES]
    s_scr[1] = s[:, LANES:]
    rows = []
    for t in range(S5_TC):
        a = jnp.concatenate([s_scr[0, pl.ds(t, n, stride=S5_TC), :],
                             s_scr[1, pl.ds(t, n, stride=S5_TC), :]], axis=1)
        rows.append(pltpu.roll(a, S5_GROUP_CH * t, 1) if t else a)
    zs = _block_rotate(rows)
    for g in range(S5_GROUPS):
        zb = zs[g].astype(BF16)
        z_ref[g] = zb
        sm = jnp.dot(zb, wsum_ref[g], preferred_element_type=F32)
        sre_ref[pl.ds(g, n, stride=S5_GROUPS), :] = sm[:, :LANES]
        sim_ref[pl.ds(g, n, stride=S5_GROUPS), :] = sm[:, LANES:]


def _s5_in(x, g1, ws5, wsum):
    nt = SEQ // TILE_A
    n = TILE_A // S5_TC
    return pl.pallas_call(
        _s5_in_kernel,
        grid=(BATCH, nt),
        in_specs=[
            pl.BlockSpec((1, TILE_A, D_MODEL), lambda b, i: (b, i, 0)),
            pl.BlockSpec((1, D_MODEL), lambda b, i: (0, 0)),
            pl.BlockSpec((D_MODEL, S5_WIDTH), lambda b, i: (0, O_S5 // S5_WIDTH)),
            pl.BlockSpec((S5_GROUPS, S5_WIDTH, S5_WIDTH), lambda b, i: (0, 0, 0)),
        ],
        out_specs=[
            pl.BlockSpec((S5_GROUPS, n, S5_WIDTH), lambda b, i: (0, b * nt + i, 0)),
            pl.BlockSpec((n * S5_GROUPS, LANES), lambda b, i: (i, b)),
            pl.BlockSpec((n * S5_GROUPS, LANES), lambda b, i: (i, b)),
        ],
        out_shape=[
            jax.ShapeDtypeStruct((S5_GROUPS, BATCH * S5_NCH, S5_WIDTH), BF16),
            jax.ShapeDtypeStruct((S5_NCH * S5_GROUPS, BATCH * LANES), F32),
            jax.ShapeDtypeStruct((S5_NCH * S5_GROUPS, BATCH * LANES), F32),
        ],
        scratch_shapes=[pltpu.VMEM((2, TILE_A, LANES), F32)],
        compiler_params=pltpu.CompilerParams(
            dimension_semantics=("parallel", "parallel"), vmem_limit_bytes=VMEM_LIMIT),
        name="s5_in",
    )(x, g1, ws5, wsum)


def _s5_scan_kernel(sre_ref, sim_ref, are_ref, aim_ref, xre_ref, xim_ref):
    ar = are_ref[...]
    ai = aim_ref[...]
    fwd = (lax.broadcasted_iota(I32, ar.shape, 1) % LANES) < S5_STATE
    zero = jnp.zeros_like(ar)

    def step(c, xr, xi):
        sr = sre_ref[c]
        si = sim_ref[c]
        return ar * xr - ai * xi + sr, ar * xi + ai * xr + si

    def fwd_body(c, carry):
        xr, xi = carry
        xre_ref[c] = xr
        xim_ref[c] = xi
        return step(c, xr, xi)

    lax.fori_loop(0, S5_NCH, fwd_body, (zero, zero))

    def rev_body(i, carry):
        xr, xi = carry
        c = S5_NCH - 1 - i
        xre_ref[c] = jnp.where(fwd, xre_ref[c], xr)
        xim_ref[c] = jnp.where(fwd, xim_ref[c], xi)
        return step(c, xr, xi)

    lax.fori_loop(0, S5_NCH, rev_body, (zero, zero))


def _s5_scan(s_re, s_im, a_re, a_im):
    nl = BATCH * LANES
    blk3 = pl.BlockSpec((S5_NCH, S5_GROUPS, SCAN_LANES), lambda j: (0, 0, j))
    blk2 = pl.BlockSpec((S5_GROUPS, SCAN_LANES), lambda j: (0, j))
    return pl.pallas_call(
        _s5_scan_kernel,
        grid=(nl // SCAN_LANES,),
        in_specs=[blk3, blk3, blk2, blk2],
        out_specs=[blk3, blk3],
        out_shape=[jax.ShapeDtypeStruct((S5_NCH, S5_GROUPS, nl), F32)] * 2,
        compiler_params=pltpu.CompilerParams(
            dimension_semantics=("parallel",), vmem_limit_bytes=VMEM_LIMIT),
        name="s5_scan",
    )(s_re, s_im, a_re, a_im)


def _s5_out_kernel(z_ref, xre_ref, xim_ref, m_ref, wc_ref, y_ref):
    n = S5_NCH
    ys = []
    for g in range(S5_GROUPS):
        xc = jnp.concatenate([xre_ref[pl.ds(g, n, stride=S5_GROUPS), :],
                              xim_ref[pl.ds(g, n, stride=S5_GROUPS), :]], axis=1).astype(BF16)
        ys.append(jnp.dot(z_ref[g], m_ref[g], preferred_element_type=F32)
                  + jnp.dot(xc, wc_ref[g], preferred_element_type=F32))
    ts = _block_rotate(ys)
    for t in range(S5_TC):
        acc = ts[t]
        if t:
            acc = pltpu.roll(acc, S5_WIDTH - S5_GROUP_CH * t, 1)
        y_ref[0, 0, pl.ds(t, n, stride=S5_TC), :] = acc[:, :LANES]
        y_ref[0, 1, pl.ds(t, n, stride=S5_TC), :] = acc[:, LANES:]


def _s5_out(z, xc_re, xc_im, m, wc):
    wspec = pl.BlockSpec((S5_GROUPS, S5_WIDTH, S5_WIDTH), lambda b: (0, 0, 0))
    xspec = pl.BlockSpec((S5_NCH * S5_GROUPS, LANES), lambda b: (0, b))
    return pl.pallas_call(
        _s5_out_kernel,
        grid=(BATCH,),
        in_specs=[pl.BlockSpec((S5_GROUPS, S5_NCH, S5_WIDTH), lambda b: (0, b, 0)), xspec, xspec, wspec, wspec],
        out_specs=pl.BlockSpec((1, 2, SEQ, LANES), lambda b: (b, 0, 0, 0)),
        out_shape=jax.ShapeDtypeStruct((BATCH, 2, SEQ, LANES), F32),
        compiler_params=pltpu.CompilerParams(
            dimension_semantics=("parallel",), vmem_limit_bytes=VMEM_LIMIT),
        name="s5_out",
    )(z, xc_re, xc_im, m, wc)


def _mixer_kernel(x_ref, y_ref, g1_ref, win_ref, bga_ref, bgb_ref, lng_ref, lnb_ref,
                  ws_ref, bs_ref, wglu_ref, bglu_ref, wupa_ref, wupb_ref, wout_ref, g2_ref, wr_ref,
                  ewg_ref, ewu_ref, ewd_ref,
                  x1_ref, h2p_ref, pr_ref, owg_ref, owu_ref, owd_ref, bra_scr, mrg_scr):
    for src, dst in ((ewg_ref, owg_ref), (ewu_ref, owu_ref), (ewd_ref, owd_ref)):
        dst[...] = src[...].astype(BF16)

    x = x_ref[0]
    hn = _rms(x, g1_ref[...]).astype(BF16)

    def in_proj(c0):
        return jnp.dot(hn, win_ref[:, c0:c0 + MIX_COLS], preferred_element_type=F32)

    nuv = GMLP_WIDTH // MIX_COLS
    us = [_gelu(in_proj(c * MIX_COLS)) for c in range(nuv)]
    vs = [_gelu(in_proj(GMLP_WIDTH + c * MIX_COLS)) for c in range(nuv)]
    mu = sum(jnp.sum(v, axis=-1, keepdims=True) for v in vs) * (1.0 / GMLP_WIDTH)
    vcs = [v - mu for v in vs]
    var = sum(jnp.sum(vc * vc, axis=-1, keepdims=True) for vc in vcs) * (1.0 / GMLP_WIDTH)
    rstd = lax.rsqrt(var + EPS)
    vns = [(vc * rstd * lng_ref[:, c * MIX_COLS:(c + 1) * MIX_COLS]
            + lnb_ref[:, c * MIX_COLS:(c + 1) * MIX_COLS]).astype(BF16) for c, vc in enumerate(vcs)]
    low = lax.broadcasted_iota(I32, (CHUNK, LANES), 1) < GMLP_HEAD
    per = MIX_COLS // LANES
    for ch in range(TILE_M // CHUNK):
        r0 = ch * CHUNK
        for q in range(GMLP_WIDTH // LANES):
            c0 = q * LANES
            l0 = (q % per) * LANES
            vb = vns[q // per][r0:r0 + CHUNK, l0:l0 + LANES]
            z0 = jnp.dot(ws_ref[2 * q], vb, preferred_element_type=F32)
            z1 = jnp.dot(ws_ref[2 * q + 1], vb, preferred_element_type=F32)
            zz = jnp.where(low, z0, z1) + bs_ref[:, c0:c0 + LANES]
            bra_scr[r0:r0 + CHUNK, c0:c0 + LANES] = (us[q // per][r0:r0 + CHUNK, l0:l0 + LANES] * zz).astype(BF16)

    yb = _gelu(jnp.concatenate([y_ref[0, 0], y_ref[0, 1]], axis=1))
    gl = jnp.dot(yb.astype(BF16), wglu_ref[...], preferred_element_type=F32) + bglu_ref[...]
    brb = (yb * jax.nn.sigmoid(gl)).astype(BF16)

    bra = bra_scr[...]
    for c in range(D_MODEL // MIX_COLS):
        cs = slice(c * MIX_COLS, (c + 1) * MIX_COLS)
        ga = jax.nn.sigmoid(in_proj(O_GATE + c * MIX_COLS) + bga_ref[:, cs])
        gb = jax.nn.sigmoid(in_proj(O_GATE + D_MODEL + c * MIX_COLS) + bgb_ref[:, cs])
        mrg_scr[:, cs] = (ga * jnp.dot(bra, wupa_ref[:, cs], preferred_element_type=F32)
                          + gb * jnp.dot(brb, wupb_ref[:, cs], preferred_element_type=F32)).astype(BF16)
    mrg = mrg_scr[...]
    ssq = jnp.zeros((TILE_M, 1), F32)
    for c in range(D_MODEL // MIX_COLS):
        cs = slice(c * MIX_COLS, (c + 1) * MIX_COLS)
        x1c = x[:, cs] + jnp.dot(mrg, wout_ref[:, cs], preferred_element_type=F32)
        x1_ref[0, :, cs] = x1c
        ssq = ssq + jnp.sum(x1c * x1c, axis=-1, keepdims=True)

    h2 = x1_ref[0] * lax.rsqrt(ssq * (1.0 / D_MODEL) + EPS) * g2_ref[...]
    half = D_MODEL // 2
    h2p_ref[0] = pltpu.pack_elementwise([h2[:, :half], h2[:, half:]], packed_dtype=BF16)

    lg = lax.dot_general(wr_ref[...], h2.astype(BF16), _NT, preferred_element_type=F32)
    logits = lg[:N_EXPERTS] + lg[N_EXPERTS:]
    mx = jnp.max(logits, axis=0, keepdims=True)
    ex = jnp.exp(logits - mx)
    pr_ref[0] = ex / jnp.sum(ex, axis=0, keepdims=True)


def _mixer(x, y, consts, expert_w):
    nt = SEQ // TILE_M
    nstep = BATCH * nt

    def cspec(a):
        nd = a.ndim
        return pl.BlockSpec(a.shape, lambda b, i: (0,) * nd, pipeline_mode=pl.Buffered(1))

    flat = [w.reshape(-1, w.shape[-1]) for w in expert_w]
    wspecs = [pl.BlockSpec((w.shape[0] // nstep, w.shape[1]), lambda b, i: (b * nt + i, 0)) for w in flat]
    outs = pl.pallas_call(
        _mixer_kernel,
        grid=(BATCH, nt),
        in_specs=[pl.BlockSpec((1, TILE_M, D_MODEL), lambda b, i: (b, i, 0)),
                  pl.BlockSpec((1, 2, TILE_M, LANES), lambda b, i: (b, 0, i, 0))]
                 + [cspec(a) for a in consts] + wspecs,
        out_specs=[pl.BlockSpec((1, TILE_M, D_MODEL), lambda b, i: (b, i, 0)),
                   pl.BlockSpec((1, TILE_M, D_MODEL // 2), lambda b, i: (b, i, 0)),
                   pl.BlockSpec((1, N_EXPERTS, TILE_M), lambda b, i: (b, 0, i))] + wspecs,
        out_shape=[jax.ShapeDtypeStruct((BATCH, SEQ, D_MODEL), F32),
                   jax.ShapeDtypeStruct((BATCH, SEQ, D_MODEL // 2), U32),
                   jax.ShapeDtypeStruct((BATCH, N_EXPERTS, SEQ), F32)]
                  + [jax.ShapeDtypeStruct(w.shape, BF16) for w in flat],
        scratch_shapes=[pltpu.VMEM((TILE_M, GMLP_WIDTH), BF16), pltpu.VMEM((TILE_M, D_MODEL), BF16)],
        compiler_params=pltpu.CompilerParams(
            dimension_semantics=("parallel", "parallel"), vmem_limit_bytes=VMEM_LIMIT_MIXER),
        name="mixer",
    )(x, y, *consts, *flat)
    return outs[:3], [o.reshape(w.shape) for o, w in zip(outs[3:], expert_w)]


TOPK_NBLK = SEQ // LANES
TOPK_ROWS = 4 * N_EXPERTS
TOPK_SEARCH_STEPS = 28


def _lane_sum(x):
    return jnp.sum(x, axis=1, keepdims=True)


def _topk_kernel(p_ref, tri_ref, idx_ref, gate_ref, eqc_scr, loc_scr, pj_scr, cb_scr, cbp_scr):
    p = p_ref[...]

    def count_ge(th):
        return _lane_sum(jnp.where(p >= th, 1.0, 0.0))

    def search(_, c):
        lo, hi = c
        w = hi - lo
        m1, m2, m3 = lo + 0.25 * w, lo + 0.5 * w, lo + 0.75 * w
        c1, c2, c3 = count_ge(m1) >= CAPACITY, count_ge(m2) >= CAPACITY, count_ge(m3) >= CAPACITY
        return (jnp.where(c3, m3, jnp.where(c2, m2, jnp.where(c1, m1, lo))),
                jnp.where(c1, jnp.where(c2, jnp.where(c3, hi, m3), m2), m1))

    lo, hi = lax.fori_loop(0, TOPK_SEARCH_STEPS, search,
                           (jnp.zeros((TOPK_ROWS, 1), F32), jnp.full((TOPK_ROWS, 1), 2.0, F32)))
    gt = p >= hi
    eq = (p >= lo) & (p < hi)
    need = CAPACITY - _lane_sum(jnp.where(gt, 1.0, 0.0))
    tri = tri_ref[...]
    eqf = jnp.where(eq, 1.0, 0.0)
    carry = jnp.zeros((TOPK_ROWS, 1), F32)
    for j in range(TOPK_NBLK):
        sl = slice(j * LANES, (j + 1) * LANES)
        cs = jnp.dot(eqf[:, sl].astype(BF16), tri, preferred_element_type=F32) + carry
        eqc_scr[:, sl] = cs
        carry = cs[:, LANES - 1:LANES]
    sel = gt | (eq & ((eqc_scr[...] - eqf) < need))
    self = jnp.where(sel, 1.0, 0.0)

    lane = lax.broadcasted_iota(I32, (TOPK_ROWS, LANES), 1)
    cb = jnp.full((TOPK_ROWS, LANES), float(2 * SEQ), F32)
    cbp = cb
    carry = jnp.zeros((TOPK_ROWS, 1), F32)
    for j in range(TOPK_NBLK):
        sl = slice(j * LANES, (j + 1) * LANES)
        loc = jnp.dot(self[:, sl].astype(BF16), tri, preferred_element_type=F32)
        loc_scr[j * TOPK_ROWS:(j + 1) * TOPK_ROWS, :] = loc
        pj_scr[j * TOPK_ROWS:(j + 1) * TOPK_ROWS, :] = p[:, sl]
        cbp = jnp.where(lane == j, carry, cbp)
        carry = carry + loc[:, LANES - 1:LANES]
        cb = jnp.where(lane == j, carry, cb)
    padr = jnp.zeros((LANES - TOPK_ROWS, LANES), F32)
    cb_scr[...] = jnp.concatenate([cb, padr], axis=0).T
    cbp_scr[...] = jnp.concatenate([cbp, padr], axis=0).T

    slot = (lax.broadcasted_iota(I32, (1, CAPACITY), 1) + 1).astype(F32)
    lane_r = lax.broadcasted_iota(I32, (LANES, LANES), 1)
    sub_f = lax.broadcasted_iota(I32, (LANES, 1), 0).astype(F32)
    pad = jnp.zeros((LANES - TOPK_NBLK, LANES), F32)

    def block_rows_t(ref, e):
        return jnp.concatenate([ref[pl.ds(e, TOPK_NBLK, stride=TOPK_ROWS), :], pad], axis=0).T

    def sub_sum(x):
        return jnp.sum(x, axis=0, keepdims=True)

    def per_expert(e, _):
        cb_e = _lane_sum(jnp.where(lane_r == e, cb_scr[...], 0.0))
        cbp_e = _lane_sum(jnp.where(lane_r == e, cbp_scr[...], 0.0))
        hit = (cbp_e < slot) & (slot <= cb_e)
        hitb = jnp.where(hit, 1.0, 0.0).astype(BF16)
        both = sub_sum(jnp.where(hit, cbp_e * float(LANES) + sub_f, 0.0))
        before = jnp.floor(both * (1.0 / LANES))
        jblk = both - before * float(LANES)
        rank = slot - before
        loc = jnp.dot(block_rows_t(loc_scr, e).astype(BF16), hitb, preferred_element_type=F32)
        off = sub_sum(jnp.where(loc < rank, 1.0, 0.0))
        pe = block_rows_t(pj_scr, e)
        hi = pe.astype(BF16)
        r1 = pe - hi.astype(F32)
        mid = r1.astype(BF16)
        lo = (r1 - mid.astype(F32)).astype(BF16)
        pg = (jnp.dot(hi, hitb, preferred_element_type=F32) + jnp.dot(mid, hitb, preferred_element_type=F32)
              + jnp.dot(lo, hitb, preferred_element_type=F32))
        idx_ref[pl.ds(e, 1), :] = (jblk * float(LANES) + off).astype(I32)
        gate_ref[pl.ds(e, 1), :] = sub_sum(jnp.where(sub_f == off, pg, 0.0))
        return 0

    lax.fori_loop(0, TOPK_ROWS, per_expert, 0, unroll=4)


def _topk(probs, tri):
    nrow = BATCH * N_EXPERTS
    return pl.pallas_call(
        _topk_kernel,
        grid=(nrow // TOPK_ROWS,),
        in_specs=[pl.BlockSpec((TOPK_ROWS, SEQ), lambda i: (i, 0)),
                  pl.BlockSpec((LANES, LANES), lambda i: (0, 0))],
        out_specs=[pl.BlockSpec((TOPK_ROWS, CAPACITY), lambda i: (i, 0)),
                   pl.BlockSpec((TOPK_ROWS, CAPACITY), lambda i: (i, 0))],
        out_shape=[jax.ShapeDtypeStruct((nrow, CAPACITY), I32),
                   jax.ShapeDtypeStruct((nrow, CAPACITY), F32)],
        scratch_shapes=[pltpu.VMEM((TOPK_ROWS, SEQ), F32),
                        pltpu.VMEM((TOPK_NBLK * TOPK_ROWS, LANES), F32),
                        pltpu.VMEM((TOPK_NBLK * TOPK_ROWS, LANES), F32),
                        pltpu.VMEM((LANES, LANES), F32),
                        pltpu.VMEM((LANES, LANES), F32)],
        compiler_params=pltpu.CompilerParams(
            dimension_semantics=("parallel",), vmem_limit_bytes=VMEM_LIMIT),
        name="topk",
    )(probs, tri)


MOE_GROUP = 8
MOE_FCHUNK = 512


def _gather_rows(idx_ref, ibase, h2p_ref, dst_ref, dbase, nrows):
    for k in range(nrows):
        r = idx_ref[0, 0, ibase + k]
        dst_ref[pl.ds(dbase + k, 1), :] = h2p_ref[0, pl.ds(r, 1), :]


def _scatter_rows(idx_ref, gate_ref, ibase, scale, y_ref, ybase, out_ref, nrows):
    for k0 in range(0, nrows, MOE_GROUP):
        rs, vals = [], []
        for k in range(k0, k0 + MOE_GROUP):
            r = idx_ref[0, 0, ibase + k]
            g = gate_ref[0, 0, ibase + k] * scale
            rs.append(r)
            vals.append(out_ref[0, pl.ds(r, 1), :] + g * y_ref[pl.ds(ybase + k, 1), :])
        for r, v in zip(rs, vals):
            out_ref[0, pl.ds(r, 1), :] = v


def _moe_kernel(idx_ref, gate_ref, h2p_ref, x1_ref, gf_ref, wg_ref, wu_ref, wd_ref, out_ref, xg_scr, xs_scr, y_scr):
    e = pl.program_id(1)
    f = pl.program_id(2)
    nf = EXPERT_FF // TILE_F
    nx = SEQ // TILE_X
    step = e * nf + f
    rq = CAPACITY // nf
    cur = e % 2
    xg_cur, xg_nxt = xg_scr.at[cur], xg_scr.at[1 - cur]
    y_cur, y_nxt = y_scr.at[cur], y_scr.at[1 - cur]

    @pl.when(step == 0)
    def _():
        out_ref[...] = jnp.zeros_like(out_ref)
        y_scr[...] = jnp.zeros_like(y_scr)

        def gather(i, _):
            base = pl.multiple_of(i * 32, 32)
            _gather_rows(idx_ref, base, h2p_ref, xg_cur, base, 32)
            return 0

        lax.fori_loop(0, CAPACITY // 32, gather, 0)

    w = xg_cur[...]
    half = D_MODEL // 2
    for i in range(2):
        xs_scr[:, i * half:(i + 1) * half] = pltpu.unpack_elementwise(
            w, index=i, packed_dtype=BF16, unpacked_dtype=F32).astype(BF16)
    xrows = pl.ds(pl.multiple_of(jnp.minimum(step, nx - 1) * TILE_X, TILE_X), TILE_X)
    out_ref[0, xrows, :] = out_ref[0, xrows, :] + jnp.where(step < nx, x1_ref[0], 0.0)
    qbase = pl.multiple_of(f * rq, rq)
    e_prev = (e + N_EXPERTS - 1) % N_EXPERTS
    e_next = (e + 1) % N_EXPERTS
    scale = jnp.where(e >= 1, 1.0, 0.0).astype(F32)
    _scatter_rows(idx_ref, gate_ref, e_prev * CAPACITY + qbase, scale, y_nxt, qbase, out_ref, rq)
    _gather_rows(idx_ref, e_next * CAPACITY + qbase, h2p_ref, xg_nxt, qbase, rq)

    xs = xs_scr[...]
    y = jnp.where(f > 0, y_cur[...], 0.0)
    for c0 in range(0, TILE_F, MOE_FCHUNK):
        cs = slice(c0, c0 + MOE_FCHUNK)
        a = jnp.dot(xs, wg_ref[0, :, cs], preferred_element_type=F32)
        u = jnp.dot(xs, wu_ref[0, :, cs], preferred_element_type=F32)
        hm = (a * jax.nn.sigmoid(a) * u).astype(BF16)
        y = y + jnp.dot(hm, wd_ref[0, cs, :], preferred_element_type=F32)
    y_cur[...] = y

    @pl.when((e == N_EXPERTS - 1) & (f == nf - 1))
    def _():
        def scatter(i, _):
            base = pl.multiple_of(i * 32, 32)
            _scatter_rows(idx_ref, gate_ref, e * CAPACITY + base, jnp.float32(1.0), y_cur, base, out_ref, 32)
            return 0

        lax.fori_loop(0, CAPACITY // 32, scatter, 0)

        def norm(i, _):
            rows = pl.ds(pl.multiple_of(i * TILE_X, TILE_X), TILE_X)
            out_ref[0, rows, :] = _rms(out_ref[0, rows, :], gf_ref[...])
            return 0

        lax.fori_loop(0, SEQ // TILE_X, norm, 0)


def _moe(idx, gate, h2p, x1, gf, wg, wu, wd):
    nf = EXPERT_FF // TILE_F
    nx = SEQ // TILE_X
    assert nx <= N_EXPERTS * nf
    sspec = pl.BlockSpec((1, 1, N_EXPERTS * CAPACITY), lambda b, e, f: (b, 0, 0), memory_space=pltpu.SMEM)
    return pl.pallas_call(
        _moe_kernel,
        grid=(BATCH, N_EXPERTS, nf),
        in_specs=[sspec, sspec,
                  pl.BlockSpec((1, SEQ, D_MODEL // 2), lambda b, e, f: (b, 0, 0), pipeline_mode=pl.Buffered(1)),
                  pl.BlockSpec((1, TILE_X, D_MODEL), lambda b, e, f: (b, jnp.minimum(e * nf + f, nx - 1), 0)),
                  pl.BlockSpec((1, D_MODEL), lambda b, e, f: (0, 0)),
                  pl.BlockSpec((1, D_MODEL, TILE_F), lambda b, e, f: (e, 0, f)),
                  pl.BlockSpec((1, D_MODEL, TILE_F), lambda b, e, f: (e, 0, f)),
                  pl.BlockSpec((1, TILE_F, D_MODEL), lambda b, e, f: (e, f, 0))],
        out_specs=pl.BlockSpec((1, SEQ, D_MODEL), lambda b, e, f: (b, 0, 0), pipeline_mode=pl.Buffered(1)),
        out_shape=jax.ShapeDtypeStruct((BATCH, SEQ, D_MODEL), F32),
        scratch_shapes=[pltpu.VMEM((2, CAPACITY, D_MODEL // 2), U32),
                        pltpu.VMEM((CAPACITY, D_MODEL), BF16),
                        pltpu.VMEM((2, CAPACITY, D_MODEL), F32)],
        compiler_params=pltpu.CompilerParams(
            dimension_semantics=("parallel", "arbitrary", "arbitrary"), vmem_limit_bytes=VMEM_LIMIT),
        name="moe",
    )(idx.reshape(BATCH, 1, N_EXPERTS * CAPACITY), gate.reshape(BATCH, 1, N_EXPERTS * CAPACITY), h2p, x1, gf,
      wg, wu, wd)


def kernel(x, norm1_g, w_in, b_gate, gmlp_ln_g, gmlp_ln_b, gmlp_w_s, gmlp_b_s, s5_lam_re, s5_lam_im, s5_log_dt,
           s5_b_re, s5_b_im, s5_c_re, s5_c_im, s5_d, s5_w_glu, s5_b_glu, w_up_a, w_up_b, w_out, norm2_g,
           w_router, w_gate, w_up, w_down, final_g):
    assert x.shape == (BATCH, SEQ, D_MODEL) and w_in.shape[0] == 1
    l = 0
    row = lambda a: a.astype(F32).reshape(1, -1)

    w_in_b = w_in[l].astype(BF16)
    m_mat, wsum, wc, a_re, a_im = _s5_tables(s5_lam_re[l], s5_lam_im[l], s5_log_dt[l], s5_b_re[l], s5_b_im[l],
                                             s5_c_re[l], s5_c_im[l], s5_d[l])
    g1 = row(norm1_g[l])

    z, s_re, s_im = _s5_in(x, g1, w_in_b, wsum)
    shape3 = (S5_NCH, S5_GROUPS, BATCH * LANES)
    xc_re, xc_im = _s5_scan(s_re.reshape(shape3), s_im.reshape(shape3),
                            jnp.tile(a_re, (1, BATCH)), jnp.tile(a_im, (1, BATCH)))
    shape2 = (S5_NCH * S5_GROUPS, BATCH * LANES)
    y = _s5_out(z, xc_re.reshape(shape2), xc_im.reshape(shape2), m_mat, wc)

    wr_t = jnp.transpose(w_router[l].astype(F32))
    wr_hi = wr_t.astype(BF16)
    wr_lo = (wr_t - wr_hi.astype(F32)).astype(BF16)
    consts = (
        g1, w_in_b, row(b_gate[l][:D_MODEL]), row(b_gate[l][D_MODEL:]), row(gmlp_ln_g[l]), row(gmlp_ln_b[l]),
        gmlp_w_s[l].astype(BF16), jnp.repeat(jnp.transpose(gmlp_b_s[l].astype(F32)), GMLP_HEAD, axis=1),
        s5_w_glu[l].astype(BF16), row(s5_b_glu[l]), w_up_a[l].astype(BF16), w_up_b[l].astype(BF16),
        w_out[l].astype(BF16), row(norm2_g[l]), jnp.concatenate([wr_hi, wr_lo], axis=0),
    )
    (x1, h2p, probs), (wg_b, wu_b, wd_b) = _mixer(x, y, consts, (w_gate[l], w_up[l], w_down[l]))

    tri = (jnp.arange(LANES)[:, None] <= jnp.arange(LANES)[None, :]).astype(BF16)
    idx, gate = _topk(probs.reshape(BATCH * N_EXPERTS, SEQ), tri)
    return _moe(idx, gate, h2p, x1, row(final_g), wg_b, wu_b, wd_b)
```

```python
import jax
import jax.numpy as jnp
import numpy as np
from jax import lax
from jax.experimental import pallas as pl
from jax.experimental.pallas import tpu as pltpu

F32 = jnp.float32
BF16 = jnp.bfloat16
I32 = jnp.int32
U32 = jnp.uint32

D_MODEL = 1024
BATCH = 16
SEQ = 4096
GMLP_WIDTH = 512
GMLP_GROUPS = 8
GMLP_HEAD = GMLP_WIDTH // GMLP_GROUPS
CHUNK = 128
S5_WIDTH = 256
S5_GROUP_CH = 16
S5_GROUPS = S5_WIDTH // S5_GROUP_CH
S5_STATE = 64
N_EXPERTS = 16
CAPACITY = 2 * SEQ // N_EXPERTS
EXPERT_FF = 2048
EPS = 1e-6
O_S5 = 2 * GMLP_WIDTH
O_GATE = O_S5 + S5_WIDTH

LANES = 128
S5_TC = 16
S5_NCH = SEQ // S5_TC
TILE_A = 1024
TILE_M = 1024
MIX_COLS = 256
TILE_F = 2048
TILE_X = 256
SCAN_LANES = 256
VMEM_LIMIT = 56 * 1024 * 1024
VMEM_LIMIT_MIXER = 60 * 1024 * 1024
VMEM_LIMIT_MOE = 62 * 1024 * 1024

_NT = (((1,), (1,)), ((), ()))


def _gelu(x):
    return 0.5 * x * (1.0 + lax.erf(x * (2.0 ** -0.5)))


def _rms(x, g):
    return x * lax.rsqrt(jnp.mean(x * x, axis=-1, keepdims=True) + EPS) * g


def _block_rotate(arrs):
    n = len(arrs)
    blk = lax.broadcasted_iota(I32, arrs[0].shape, 1) // S5_GROUP_CH
    w = [arrs[(-i) % n] for i in range(n)]
    for s in range(n.bit_length() - 1):
        bit = ((blk >> s) & 1) == 1
        w = [jnp.where(bit, w[(i - (1 << s)) % n], w[i]) for i in range(n)]
    return w


def _cpow(er, ei, k):
    mag = jnp.exp(k * er)
    return mag * jnp.cos(k * ei), mag * jnp.sin(k * ei)


def _cmul(a, b):
    return a[0] * b[0] - a[1] * b[1], a[0] * b[1] + a[1] * b[0]


def _s5_tables(lam_re, lam_im, log_dt, b_re, b_im, c_re, c_im, d_skip):
    hp = lax.Precision.HIGH
    tc, ng, nh = S5_TC, S5_GROUPS, S5_GROUP_CH
    lr, li = lam_re.astype(F32), lam_im.astype(F32)
    dt = jnp.exp(log_dt.astype(F32))[..., None]
    er, ei = lr * dt, li * dt
    lb = _cpow(er, ei, 1.0)
    den = lr * lr + li * li
    q = (((lb[0] - 1.0) * lr + lb[1] * li) / den, (lb[1] * lr - (lb[0] - 1.0) * li) / den)
    bbar = _cmul((q[0][..., None], q[1][..., None]), (b_re.astype(F32), b_im.astype(F32)))
    cm = (c_re.astype(F32), c_im.astype(F32))

    kk = np.arange(tc, dtype=np.float32)[:, None, None, None]
    pw = _cpow(er[None], ei[None], kk)
    t = _cmul((cm[0][None], cm[1][None]), (pw[0][:, :, :, None, :], pw[1][:, :, :, None, :]))
    taps = (jnp.einsum('kdghp,dgpi->kdghi', t[0], bbar[0], precision=hp)
            - jnp.einsum('kdghp,dgpi->kdghi', t[1], bbar[1], precision=hp))

    tau = (np.arange(tc)[None, :] - np.arange(ng)[:, None]) % tc
    diff = tau[:, None, :] - tau[:, :, None]
    oh_f = (diff[..., None] == np.arange(tc)).astype(np.float32)
    oh_r = (-diff[..., None] == np.arange(tc)).astype(np.float32)
    m = (jnp.einsum('gabk,kghi->gaibh', oh_f, taps[:, 0], precision=hp)
         + jnp.einsum('gabk,kghi->gaibh', oh_r, taps[:, 1], precision=hp))
    eye_j = np.eye(tc, dtype=np.float32)[None, :, None, :, None]
    eye_h = np.eye(nh, dtype=np.float32)[None, None, :, None, :]
    m = (m + eye_j * eye_h * d_skip.astype(F32)[:, None, None, None, :]).reshape(ng, tc * nh, tc * nh)

    tauf = tau.astype(np.float32)
    bt = (jnp.transpose(bbar[0], (0, 1, 3, 2)), jnp.transpose(bbar[1], (0, 1, 3, 2)))
    parts = []
    for d, kin in ((0, tc - 1.0 - tauf), (1, tauf)):
        p_in = _cpow(er[d][:, None, None, :], ei[d][:, None, None, :], kin[:, :, None, None])
        parts.append(_cmul(p_in, (bt[0][d][:, None], bt[1][d][:, None])))
    wsum = jnp.concatenate([parts[0][0], parts[1][0], parts[0][1], parts[1][1]], axis=-1)
    wsum = wsum.reshape(ng, tc * nh, 4 * S5_STATE)

    ct = (jnp.transpose(cm[0], (0, 1, 3, 2)), jnp.transpose(cm[1], (0, 1, 3, 2)))
    parts = []
    for d, kout in ((0, tauf + 1.0), (1, tc - tauf)):
        p_out = _cpow(er[d][:, :, None, None], ei[d][:, :, None, None], kout[:, None, :, None])
        parts.append(_cmul(p_out, (ct[0][d][:, :, None, :], ct[1][d][:, :, None, :])))
    wc = jnp.concatenate([parts[0][0], parts[1][0], -parts[0][1], -parts[1][1]], axis=1)
    wc = wc.reshape(ng, 4 * S5_STATE, tc * nh)

    a16 = _cpow(er, ei, float(tc))
    a_re = jnp.concatenate([a16[0][0], a16[0][1]], axis=-1)
    a_im = jnp.concatenate([a16[1][0], a16[1][1]], axis=-1)
    return m.astype(BF16), wsum.astype(BF16), wc.astype(BF16), a_re, a_im


def _s5_in_kernel(x_ref, g1_ref, ws5_ref, wsum_ref, z_ref, sre_ref, sim_ref, s_scr):
    n = TILE_A // S5_TC
    hn = _rms(x_ref[0], g1_ref[...]).astype(BF16)
    s = jnp.dot(hn, ws5_ref[...], preferred_element_type=F32)
    s_scr[0] = s[:, :LANES]
    s_scr[1] = s[:, LANES:]
    rows = []
    for t in range(S5_TC):
        a = jnp.concatenate([s_scr[0, pl.ds(t, n, stride=S5_TC), :],
                             s_scr[1, pl.ds(t, n, stride=S5_TC), :]], axis=1)
        rows.append(pltpu.roll(a, S5_GROUP_CH * t, 1) if t else a)
    zs = _block_rotate(rows)
    for g in range(S5_GROUPS):
        zb = zs[g].astype(BF16)
        z_ref[g] = zb
        sm = jnp.dot(zb, wsum_ref[g], preferred_element_type=F32)
        sre_ref[pl.ds(g, n, stride=S5_GROUPS), :] = sm[:, :LANES]
        sim_ref[pl.ds(g, n, stride=S5_GROUPS), :] = sm[:, LANES:]


def _s5_in(x, g1, ws5, wsum):
    nt = SEQ // TILE_A
    n = TILE_A // S5_TC
    return pl.pallas_call(
        _s5_in_kernel,
        grid=(BATCH, nt),
        in_specs=[
            pl.BlockSpec((1, TILE_A, D_MODEL), lambda b, i: (b, i, 0)),
            pl.BlockSpec((1, D_MODEL), lambda b, i: (0, 0)),
            pl.BlockSpec((D_MODEL, S5_WIDTH), lambda b, i: (0, O_S5 // S5_WIDTH)),
            pl.BlockSpec((S5_GROUPS, S5_WIDTH, S5_WIDTH), lambda b, i: (0, 0, 0)),
        ],
        out_specs=[
            pl.BlockSpec((S5_GROUPS, n, S5_WIDTH), lambda b, i: (0, b * nt + i, 0)),
            pl.BlockSpec((n * S5_GROUPS, LANES), lambda b, i: (i, b)),
            pl.BlockSpec((n * S5_GROUPS, LANES), lambda b, i: (i, b)),
        ],
        out_shape=[
            jax.ShapeDtypeStruct((S5_GROUPS, BATCH * S5_NCH, S5_WIDTH), BF16),
            jax.ShapeDtypeStruct((S5_NCH * S5_GROUPS, BATCH * LANES), F32),
            jax.ShapeDtypeStruct((S5_NCH * S5_GROUPS, BATCH * LANES), F32),
        ],
        scratch_shapes=[pltpu.VMEM((2, TILE_A, LANES), F32)],
        compiler_params=pltpu.CompilerParams(
            dimension_semantics=("parallel", "parallel"), vmem_limit_bytes=VMEM_LIMIT),
        name="s5_in",
    )(x, g1, ws5, wsum)


def _s5_scan_kernel(sre_ref, sim_ref, are_ref, aim_ref, xre_ref, xim_ref):
    ar = are_ref[...]
    ai = aim_ref[...]
    fwd = (lax.broadcasted_iota(I32, ar.shape, 1) % LANES) < S5_STATE
    zero = jnp.zeros_like(ar)

    def step(c, xr, xi):
        sr = sre_ref[c]
        si = sim_ref[c]
        return ar * xr - ai * xi + sr, ar * xi + ai * xr + si

    def fwd_body(c, carry):
        xr, xi = carry
        xre_ref[c] = xr
        xim_ref[c] = xi
        return step(c, xr, xi)

    lax.fori_loop(0, S5_NCH, fwd_body, (zero, zero))

    def rev_body(i, carry):
        xr, xi = carry
        c = S5_NCH - 1 - i
        xre_ref[c] = jnp.where(fwd, xre_ref[c], xr)
        xim_ref[c] = jnp.where(fwd, xim_ref[c], xi)
        return step(c, xr, xi)

    lax.fori_loop(0, S5_NCH, rev_body, (zero, zero))


def _s5_scan(s_re, s_im, a_re, a_im):
    nl = BATCH * LANES
    blk3 = pl.BlockSpec((S5_NCH, S5_GROUPS, SCAN_LANES), lambda j: (0, 0, j))
    blk2 = pl.BlockSpec((S5_GROUPS, SCAN_LANES), lambda j: (0, j))
    return pl.pallas_call(
        _s5_scan_kernel,
        grid=(nl // SCAN_LANES,),
        in_specs=[blk3, blk3, blk2, blk2],
        out_specs=[blk3, blk3],
        out_shape=[jax.ShapeDtypeStruct((S5_NCH, S5_GROUPS, nl), F32)] * 2,
        compiler_params=pltpu.CompilerParams(
            dimension_semantics=("parallel",), vmem_limit_bytes=VMEM_LIMIT),
        name="s5_scan",
    )(s_re, s_im, a_re, a_im)


def _s5_out_kernel(z_ref, xre_ref, xim_ref, m_ref, wc_ref, y_ref):
    n = S5_NCH
    ys = []
    for g in range(S5_GROUPS):
        xc = jnp.concatenate([xre_ref[pl.ds(g, n, stride=S5_GROUPS), :],
                              xim_ref[pl.ds(g, n, stride=S5_GROUPS), :]], axis=1).astype(BF16)
        ys.append(jnp.dot(z_ref[g], m_ref[g], preferred_element_type=F32)
                  + jnp.dot(xc, wc_ref[g], preferred_element_type=F32))
    ts = _block_rotate(ys)
    for t in range(S5_TC):
        acc = ts[t]
        if t:
            acc = pltpu.roll(acc, S5_WIDTH - S5_GROUP_CH * t, 1)
        y_ref[0, 0, pl.ds(t, n, stride=S5_TC), :] = acc[:, :LANES]
        y_ref[0, 1, pl.ds(t, n, stride=S5_TC), :] = acc[:, LANES:]


def _s5_out(z, xc_re, xc_im, m, wc):
    wspec = pl.BlockSpec((S5_GROUPS, S5_WIDTH, S5_WIDTH), lambda b: (0, 0, 0))
    xspec = pl.BlockSpec((S5_NCH * S5_GROUPS, LANES), lambda b: (0, b))
    return pl.pallas_call(
        _s5_out_kernel,
        grid=(BATCH,),
        in_specs=[pl.BlockSpec((S5_GROUPS, S5_NCH, S5_WIDTH), lambda b: (0, b, 0)), xspec, xspec, wspec, wspec],
        out_specs=pl.BlockSpec((1, 2, SEQ, LANES), lambda b: (b, 0, 0, 0)),
        out_shape=jax.ShapeDtypeStruct((BATCH, 2, SEQ, LANES), F32),
        compiler_params=pltpu.CompilerParams(
            dimension_semantics=("parallel",), vmem_limit_bytes=VMEM_LIMIT),
        name="s5_out",
    )(z, xc_re, xc_im, m, wc)


def _mixer_kernel(x_ref, y_ref, g1_ref, win_ref, bga_ref, bgb_ref, lng_ref, lnb_ref,
                  ws_ref, bs_ref, wglu_ref, bglu_ref, wupa_ref, wupb_ref, wout_ref, g2_ref, wr_ref,
                  ewg_ref, ewu_ref, ewd_ref,
                  x1_ref, h2p_ref, pr_ref, owg_ref, owu_ref, owd_ref, bra_scr, mrg_scr):
    for src, dst in ((ewg_ref, owg_ref), (ewu_ref, owu_ref), (ewd_ref, owd_ref)):
        dst[...] = src[...].astype(BF16)

    x = x_ref[0]
    hn = _rms(x, g1_ref[...]).astype(BF16)

    def in_proj(c0):
        return jnp.dot(hn, win_ref[:, c0:c0 + MIX_COLS], preferred_element_type=F32)

    nuv = GMLP_WIDTH // MIX_COLS
    us = [_gelu(in_proj(c * MIX_COLS)) for c in range(nuv)]
    vs = [_gelu(in_proj(GMLP_WIDTH + c * MIX_COLS)) for c in range(nuv)]
    mu = sum(jnp.sum(v, axis=-1, keepdims=True) for v in vs) * (1.0 / GMLP_WIDTH)
    vcs = [v - mu for v in vs]
    var = sum(jnp.sum(vc * vc, axis=-1, keepdims=True) for vc in vcs) * (1.0 / GMLP_WIDTH)
    rstd = lax.rsqrt(var + EPS)
    vns = [(vc * rstd * lng_ref[:, c * MIX_COLS:(c + 1) * MIX_COLS]
            + lnb_ref[:, c * MIX_COLS:(c + 1) * MIX_COLS]).astype(BF16) for c, vc in enumerate(vcs)]
    low = lax.broadcasted_iota(I32, (CHUNK, LANES), 1) < GMLP_HEAD
    per = MIX_COLS // LANES
    for ch in range(TILE_M // CHUNK):
        r0 = ch * CHUNK
        for q in range(GMLP_WIDTH // LANES):
            c0 = q * LANES
            l0 = (q % per) * LANES
            vb = vns[q // per][r0:r0 + CHUNK, l0:l0 + LANES]
            z0 = jnp.dot(ws_ref[2 * q], vb, preferred_element_type=F32)
            z1 = jnp.dot(ws_ref[2 * q + 1], vb, preferred_element_type=F32)
            zz = jnp.where(low, z0, z1) + bs_ref[:, c0:c0 + LANES]
            bra_scr[r0:r0 + CHUNK, c0:c0 + LANES] = (us[q // per][r0:r0 + CHUNK, l0:l0 + LANES] * zz).astype(BF16)

    yb = _gelu(jnp.concatenate([y_ref[0, 0], y_ref[0, 1]], axis=1))
    gl = jnp.dot(yb.astype(BF16), wglu_ref[...], preferred_element_type=F32) + bglu_ref[...]
    brb = (yb * jax.nn.sigmoid(gl)).astype(BF16)

    bra = bra_scr[...]
    for c in range(D_MODEL // MIX_COLS):
        cs = slice(c * MIX_COLS, (c + 1) * MIX_COLS)
        ga = jax.nn.sigmoid(in_proj(O_GATE + c * MIX_COLS) + bga_ref[:, cs])
        gb = jax.nn.sigmoid(in_proj(O_GATE + D_MODEL + c * MIX_COLS) + bgb_ref[:, cs])
        mrg_scr[:, cs] = (ga * jnp.dot(bra, wupa_ref[:, cs], preferred_element_type=F32)
                          + gb * jnp.dot(brb, wupb_ref[:, cs], preferred_element_type=F32)).astype(BF16)
    mrg = mrg_scr[...]
    ssq = jnp.zeros((TILE_M, 1), F32)
    for c in range(D_MODEL // MIX_COLS):
        cs = slice(c * MIX_COLS, (c + 1) * MIX_COLS)
        x1c = x[:, cs] + jnp.dot(mrg, wout_ref[:, cs], preferred_element_type=F32)
        x1_ref[0, :, cs] = x1c
        ssq = ssq + jnp.sum(x1c * x1c, axis=-1, keepdims=True)

    h2 = x1_ref[0] * lax.rsqrt(ssq * (1.0 / D_MODEL) + EPS) * g2_ref[...]
    half = D_MODEL // 2
    h2p_ref[0] = pltpu.pack_elementwise([h2[:, :half], h2[:, half:]], packed_dtype=BF16)

    lg = lax.dot_general(wr_ref[...], h2.astype(BF16), _NT, preferred_element_type=F32)
    logits = lg[:N_EXPERTS] + lg[N_EXPERTS:]
    mx = jnp.max(logits, axis=0, keepdims=True)
    ex = jnp.exp(logits - mx)
    pr_ref[0] = ex / jnp.sum(ex, axis=0, keepdims=True)


def _mixer(x, y, consts, expert_w):
    nt = SEQ // TILE_M
    nstep = BATCH * nt

    def cspec(a):
        nd = a.ndim
        return pl.BlockSpec(a.shape, lambda b, i: (0,) * nd, pipeline_mode=pl.Buffered(1))

    flat = [w.reshape(-1, w.shape[-1]) for w in expert_w]
    wspecs = [pl.BlockSpec((w.shape[0] // nstep, w.shape[1]), lambda b, i: (b * nt + i, 0)) for w in flat]
    outs = pl.pallas_call(
        _mixer_kernel,
        grid=(BATCH, nt),
        in_specs=[pl.BlockSpec((1, TILE_M, D_MODEL), lambda b, i: (b, i, 0)),
                  pl.BlockSpec((1, 2, TILE_M, LANES), lambda b, i: (b, 0, i, 0))]
                 + [cspec(a) for a in consts] + wspecs,
        out_specs=[pl.BlockSpec((1, TILE_M, D_MODEL), lambda b, i: (b, i, 0)),
                   pl.BlockSpec((1, TILE_M, D_MODEL // 2), lambda b, i: (b, i, 0)),
                   pl.BlockSpec((1, N_EXPERTS, TILE_M), lambda b, i: (b, 0, i))] + wspecs,
        out_shape=[jax.ShapeDtypeStruct((BATCH, SEQ, D_MODEL), F32),
                   jax.ShapeDtypeStruct((BATCH, SEQ, D_MODEL // 2), U32),
                   jax.ShapeDtypeStruct((BATCH, N_EXPERTS, SEQ), F32)]
                  + [jax.ShapeDtypeStruct(w.shape, BF16) for w in flat],
        scratch_shapes=[pltpu.VMEM((TILE_M, GMLP_WIDTH), BF16), pltpu.VMEM((TILE_M, D_MODEL), BF16)],
        compiler_params=pltpu.CompilerParams(
            dimension_semantics=("parallel", "parallel"), vmem_limit_bytes=VMEM_LIMIT_MIXER),
        name="mixer",
    )(x, y, *consts, *flat)
    return outs[:3], [o.reshape(w.shape) for o, w in zip(outs[3:], expert_w)]


TOPK_NBLK = SEQ // LANES
TOPK_ROWS = 4 * N_EXPERTS
TOPK_SEARCH_STEPS = 28


def _lane_sum(x):
    return jnp.sum(x, axis=1, keepdims=True)


def _topk_kernel(p_ref, tri_ref, idx_ref, gate_ref, eqc_scr, loc_scr, pj_scr, cb_scr, cbp_scr):
    p = p_ref[...]

    def count_ge(th):
        return _lane_sum(jnp.where(p >= th, 1.0, 0.0))

    def search(_, c):
        lo, hi = c
        w = hi - lo
        m1, m2, m3 = lo + 0.25 * w, lo + 0.5 * w, lo + 0.75 * w
        c1, c2, c3 = count_ge(m1) >= CAPACITY, count_ge(m2) >= CAPACITY, count_ge(m3) >= CAPACITY
        return (jnp.where(c3, m3, jnp.where(c2, m2, jnp.where(c1, m1, lo))),
                jnp.where(c1, jnp.where(c2, jnp.where(c3, hi, m3), m2), m1))

    lo, hi = lax.fori_loop(0, TOPK_SEARCH_STEPS, search,
                           (jnp.zeros((TOPK_ROWS, 1), F32), jnp.full((TOPK_ROWS, 1), 2.0, F32)))
    gt = p >= hi
    eq = (p >= lo) & (p < hi)
    need = CAPACITY - _lane_sum(jnp.where(gt, 1.0, 0.0))
    tri = tri_ref[...]
    eqf = jnp.where(eq, 1.0, 0.0)
    carry = jnp.zeros((TOPK_ROWS, 1), F32)
    for j in range(TOPK_NBLK):
        sl = slice(j * LANES, (j + 1) * LANES)
        cs = jnp.dot(eqf[:, sl].astype(BF16), tri, preferred_element_type=F32) + carry
        eqc_scr[:, sl] = cs
        carry = cs[:, LANES - 1:LANES]
    sel = gt | (eq & ((eqc_scr[...] - eqf) < need))
    self = jnp.where(sel, 1.0, 0.0)

    lane = lax.broadcasted_iota(I32, (TOPK_ROWS, LANES), 1)
    cb = jnp.full((TOPK_ROWS, LANES), float(2 * SEQ), F32)
    cbp = cb
    carry = jnp.zeros((TOPK_ROWS, 1), F32)
    for j in range(TOPK_NBLK):
        sl = slice(j * LANES, (j + 1) * LANES)
        loc = jnp.dot(self[:, sl].astype(BF16), tri, preferred_element_type=F32)
        loc_scr[j * TOPK_ROWS:(j + 1) * TOPK_ROWS, :] = loc
        pj_scr[j * TOPK_ROWS:(j + 1) * TOPK_ROWS, :] = p[:, sl]
        cbp = jnp.where(lane == j, carry, cbp)
        carry = carry + loc[:, LANES - 1:LANES]
        cb = jnp.where(lane == j, carry, cb)
    padr = jnp.zeros((LANES - TOPK_ROWS, LANES), F32)
    cb_scr[...] = jnp.concatenate([cb, padr], axis=0).T
    cbp_scr[...] = jnp.concatenate([cbp, padr], axis=0).T

    slot = (lax.broadcasted_iota(I32, (1, CAPACITY), 1) + 1).astype(F32)
    lane_r = lax.broadcasted_iota(I32, (LANES, LANES), 1)
    sub_f = lax.broadcasted_iota(I32, (LANES, 1), 0).astype(F32)
    pad = jnp.zeros((LANES - TOPK_NBLK, LANES), F32)

    def block_rows_t(ref, e):
        return jnp.concatenate([ref[pl.ds(e, TOPK_NBLK, stride=TOPK_ROWS), :], pad], axis=0).T

    def sub_sum(x):
        return jnp.sum(x, axis=0, keepdims=True)

    def per_expert(e, _):
        cb_e = _lane_sum(jnp.where(lane_r == e, cb_scr[...], 0.0))
        cbp_e = _lane_sum(jnp.where(lane_r == e, cbp_scr[...], 0.0))
        hit = (cbp_e < slot) & (slot <= cb_e)
        hitb = jnp.where(hit, 1.0, 0.0).astype(BF16)
        both = sub_sum(jnp.where(hit, cbp_e * float(LANES) + sub_f, 0.0))
        before = jnp.floor(both * (1.0 / LANES))
        jblk = both - before * float(LANES)
        rank = slot - before
        loc = jnp.dot(block_rows_t(loc_scr, e).astype(BF16), hitb, preferred_element_type=F32)
        off = sub_sum(jnp.where(loc < rank, 1.0, 0.0))
        pe = block_rows_t(pj_scr, e)
        hi = pe.astype(BF16)
        r1 = pe - hi.astype(F32)
        mid = r1.astype(BF16)
        lo = (r1 - mid.astype(F32)).astype(BF16)
        pg = (jnp.dot(hi, hitb, preferred_element_type=F32) + jnp.dot(mid, hitb, preferred_element_type=F32)
              + jnp.dot(lo, hitb, preferred_element_type=F32))
        idx_ref[pl.ds(e, 1), :] = (jblk * float(LANES) + off).astype(I32)
        gate_ref[pl.ds(e, 1), :] = sub_sum(jnp.where(sub_f == off, pg, 0.0))
        return 0

    lax.fori_loop(0, TOPK_ROWS, per_expert, 0, unroll=4)


def _topk(probs, tri):
    nrow = BATCH * N_EXPERTS
    return pl.pallas_call(
        _topk_kernel,
        grid=(nrow // TOPK_ROWS,),
        in_specs=[pl.BlockSpec((TOPK_ROWS, SEQ), lambda i: (i, 0)),
                  pl.BlockSpec((LANES, LANES), lambda i: (0, 0))],
        out_specs=[pl.BlockSpec((TOPK_ROWS, CAPACITY), lambda i: (i, 0)),
                   pl.BlockSpec((TOPK_ROWS, CAPACITY), lambda i: (i, 0))],
        out_shape=[jax.ShapeDtypeStruct((nrow, CAPACITY), I32),
                   jax.ShapeDtypeStruct((nrow, CAPACITY), F32)],
        scratch_shapes=[pltpu.VMEM((TOPK_ROWS, SEQ), F32),
                        pltpu.VMEM((TOPK_NBLK * TOPK_ROWS, LANES), F32),
                        pltpu.VMEM((TOPK_NBLK * TOPK_ROWS, LANES), F32),
                        pltpu.VMEM((LANES, LANES), F32),
                        pltpu.VMEM((LANES, LANES), F32)],
        compiler_params=pltpu.CompilerParams(
            dimension_semantics=("parallel",), vmem_limit_bytes=VMEM_LIMIT),
        name="topk",
    )(probs, tri)


MOE_GROUP = 8
MOE_FCHUNK = 512
MOE_NBUF = 1 if TILE_F == EXPERT_FF else 2


def _gather_rows(idx_ref, ibase, h2p_ref, dst_ref, dbase, nrows):
    for k in range(nrows):
        r = idx_ref[0, 0, ibase + k]
        dst_ref[pl.ds(dbase + k, 1), :] = h2p_ref[0, pl.ds(r, 1), :]


def _scatter_rows(idx_ref, gate_ref, ibase, scale, y_ref, ybase, out_ref, nrows):
    for k0 in range(0, nrows, MOE_GROUP):
        rs, vals = [], []
        for k in range(k0, k0 + MOE_GROUP):
            r = idx_ref[0, 0, ibase + k]
            g = gate_ref[0, 0, ibase + k] * scale
            rs.append(r)
            vals.append(out_ref[0, pl.ds(r, 1), :] + g * y_ref[pl.ds(ybase + k, 1), :])
        for r, v in zip(rs, vals):
            out_ref[0, pl.ds(r, 1), :] = v


def _moe_kernel(idx_ref, gate_ref, h2p_ref, x1_ref, gf_ref, wg_ref, wu_ref, wd_ref, out_ref, xg_scr, xs_scr, y_scr):
    e = pl.program_id(1)
    f = pl.program_id(2)
    nf = EXPERT_FF // TILE_F
    nx = SEQ // TILE_X
    step = e * nf + f
    rq = CAPACITY // nf
    cur = e % MOE_NBUF
    nxt = (e + 1) % MOE_NBUF
    xg_cur, xg_nxt = xg_scr.at[cur], xg_scr.at[nxt]
    y_cur, y_nxt = y_scr.at[cur], y_scr.at[nxt]

    @pl.when(step == 0)
    def _():
        out_ref[...] = jnp.zeros_like(out_ref)
        y_scr[...] = jnp.zeros_like(y_scr)

        def gather(i, _):
            base = pl.multiple_of(i * 32, 32)
            _gather_rows(idx_ref, base, h2p_ref, xg_cur, base, 32)
            return 0

        lax.fori_loop(0, CAPACITY // 32, gather, 0)

    w = xg_cur[...]
    half = D_MODEL // 2
    for i in range(2):
        xs_scr[:, i * half:(i + 1) * half] = pltpu.unpack_elementwise(
            w, index=i, packed_dtype=BF16, unpacked_dtype=F32).astype(BF16)
    xrows = pl.ds(pl.multiple_of(jnp.minimum(step, nx - 1) * TILE_X, TILE_X), TILE_X)
    out_ref[0, xrows, :] = out_ref[0, xrows, :] + jnp.where(step < nx, x1_ref[0], 0.0)
    qbase = pl.multiple_of(f * rq, rq)
    e_prev = (e + N_EXPERTS - 1) % N_EXPERTS
    e_next = (e + 1) % N_EXPERTS
    scale = jnp.where(e >= 1, 1.0, 0.0).astype(F32)
    _scatter_rows(idx_ref, gate_ref, e_prev * CAPACITY + qbase, scale, y_nxt, qbase, out_ref, rq)
    _gather_rows(idx_ref, e_next * CAPACITY + qbase, h2p_ref, xg_nxt, qbase, rq)

    xs = xs_scr[...]
    y = jnp.where(f > 0, y_cur[...], 0.0) if nf > 1 else None
    for c0 in range(0, TILE_F, MOE_FCHUNK):
        cs = slice(c0, c0 + MOE_FCHUNK)
        a = jnp.dot(xs, wg_ref[0, :, cs], preferred_element_type=F32)
        u = jnp.dot(xs, wu_ref[0, :, cs], preferred_element_type=F32)
        hm = (a * jax.nn.sigmoid(a) * u).astype(BF16)
        yc = jnp.dot(hm, wd_ref[0, cs, :], preferred_element_type=F32)
        y = yc if y is None else y + yc
    y_cur[...] = y

    @pl.when((e == N_EXPERTS - 1) & (f == nf - 1))
    def _():
        def scatter(i, _):
            base = pl.multiple_of(i * 32, 32)
            _scatter_rows(idx_ref, gate_ref, e * CAPACITY + base, jnp.float32(1.0), y_cur, base, out_ref, 32)
            return 0

        lax.fori_loop(0, CAPACITY // 32, scatter, 0)

        def norm(i, _):
            rows = pl.ds(pl.multiple_of(i * TILE_X, TILE_X), TILE_X)
            out_ref[0, rows, :] = _rms(out_ref[0, rows, :], gf_ref[...])
            return 0

        lax.fori_loop(0, SEQ // TILE_X, norm, 0)


def _moe(idx, gate, h2p, x1, gf, wg, wu, wd):
    nf = EXPERT_FF // TILE_F
    nx = SEQ // TILE_X
    assert nx <= N_EXPERTS * nf
    sspec = pl.BlockSpec((1, 1, N_EXPERTS * CAPACITY), lambda b, e, f: (b, 0, 0), memory_space=pltpu.SMEM)
    return pl.pallas_call(
        _moe_kernel,
        grid=(BATCH, N_EXPERTS, nf),
        in_specs=[sspec, sspec,
                  pl.BlockSpec((1, SEQ, D_MODEL // 2), lambda b, e, f: (b, 0, 0), pipeline_mode=pl.Buffered(1)),
                  pl.BlockSpec((1, TILE_X, D_MODEL), lambda b, e, f: (b, jnp.minimum(e * nf + f, nx - 1), 0)),
                  pl.BlockSpec((1, D_MODEL), lambda b, e, f: (0, 0)),
                  pl.BlockSpec((1, D_MODEL, TILE_F), lambda b, e, f: (e, 0, f)),
                  pl.BlockSpec((1, D_MODEL, TILE_F), lambda b, e, f: (e, 0, f)),
                  pl.BlockSpec((1, TILE_F, D_MODEL), lambda b, e, f: (e, f, 0))],
        out_specs=pl.BlockSpec((1, SEQ, D_MODEL), lambda b, e, f: (b, 0, 0), pipeline_mode=pl.Buffered(1)),
        out_shape=jax.ShapeDtypeStruct((BATCH, SEQ, D_MODEL), F32),
        scratch_shapes=[pltpu.VMEM((MOE_NBUF, CAPACITY, D_MODEL // 2), U32),
                        pltpu.VMEM((CAPACITY, D_MODEL), BF16),
                        pltpu.VMEM((MOE_NBUF, CAPACITY, D_MODEL), F32)],
        compiler_params=pltpu.CompilerParams(
            dimension_semantics=("parallel", "arbitrary", "arbitrary"), vmem_limit_bytes=VMEM_LIMIT_MOE),
        name="moe",
    )(idx.reshape(BATCH, 1, N_EXPERTS * CAPACITY), gate.reshape(BATCH, 1, N_EXPERTS * CAPACITY), h2p, x1, gf,
      wg, wu, wd)


def kernel(x, norm1_g, w_in, b_gate, gmlp_ln_g, gmlp_ln_b, gmlp_w_s, gmlp_b_s, s5_lam_re, s5_lam_im, s5_log_dt,
           s5_b_re, s5_b_im, s5_c_re, s5_c_im, s5_d, s5_w_glu, s5_b_glu, w_up_a, w_up_b, w_out, norm2_g,
           w_router, w_gate, w_up, w_down, final_g):
    assert x.shape == (BATCH, SEQ, D_MODEL) and w_in.shape[0] == 1
    l = 0
    row = lambda a: a.astype(F32).reshape(1, -1)

    w_in_b = w_in[l].astype(BF16)
    m_mat, wsum, wc, a_re, a_im = _s5_tables(s5_lam_re[l], s5_lam_im[l], s5_log_dt[l], s5_b_re[l], s5_b_im[l],
                                             s5_c_re[l], s5_c_im[l], s5_d[l])
    g1 = row(norm1_g[l])

    z, s_re, s_im = _s5_in(x, g1, w_in_b, wsum)
    shape3 = (S5_NCH, S5_GROUPS, BATCH * LANES)
    xc_re, xc_im = _s5_scan(s_re.reshape(shape3), s_im.reshape(shape3),
                            jnp.tile(a_re, (1, BATCH)), jnp.tile(a_im, (1, BATCH)))
    shape2 = (S5_NCH * S5_GROUPS, BATCH * LANES)
    y = _s5_out(z, xc_re.reshape(shape2), xc_im.reshape(shape2), m_mat, wc)

    wr_t = jnp.transpose(w_router[l].astype(F32))
    wr_hi = wr_t.astype(BF16)
    wr_lo = (wr_t - wr_hi.astype(F32)).astype(BF16)
    consts = (
        g1, w_in_b, row(b_gate[l][:D_MODEL]), row(b_gate[l][D_MODEL:]), row(gmlp_ln_g[l]), row(gmlp_ln_b[l]),
        gmlp_w_s[l].astype(BF16), jnp.repeat(jnp.transpose(gmlp_b_s[l].astype(F32)), GMLP_HEAD, axis=1),
        s5_w_glu[l].astype(BF16), row(s5_b_glu[l]), w_up_a[l].astype(BF16), w_up_b[l].astype(BF16),
        w_out[l].astype(BF16), row(norm2_g[l]), jnp.concatenate([wr_hi, wr_lo], axis=0),
    )
    (x1, h2p, probs), (wg_b, wu_b, wd_b) = _mixer(x, y, consts, (w_gate[l], w_up[l], w_down[l]))

    tri = (jnp.arange(LANES)[:, None] <= jnp.arange(LANES)[None, :]).astype(BF16)
    idx, gate = _topk(probs.reshape(BATCH * N_EXPERTS, SEQ), tri)
    return _moe(idx, gate, h2p, x1, row(final_g), wg_b, wu_b, wd_b)
```

```python
import jax
import jax.numpy as jnp
import numpy as np
from jax import lax
from jax.experimental import pallas as pl
from jax.experimental.pallas import tpu as pltpu

F32 = jnp.float32
BF16 = jnp.bfloat16
I32 = jnp.int32
U32 = jnp.uint32

D_MODEL = 1024
BATCH = 16
SEQ = 4096
GMLP_WIDTH = 512
GMLP_GROUPS = 8
GMLP_HEAD = GMLP_WIDTH // GMLP_GROUPS
CHUNK = 128
S5_WIDTH = 256
S5_GROUP_CH = 16
S5_GROUPS = S5_WIDTH // S5_GROUP_CH
S5_STATE = 64
N_EXPERTS = 16
CAPACITY = 2 * SEQ // N_EXPERTS
EXPERT_FF = 2048
EPS = 1e-6
O_S5 = 2 * GMLP_WIDTH
O_GATE = O_S5 + S5_WIDTH

LANES = 128
S5_TC = 16
S5_NCH = SEQ // S5_TC
TILE_A = 1024
TILE_M = 1024
MIX_COLS = 256
TILE_F = 2048
TILE_X = 256
SCAN_LANES = 256
VMEM_LIMIT = 56 * 1024 * 1024
VMEM_LIMIT_MIXER = 60 * 1024 * 1024
VMEM_LIMIT_MOE = 62 * 1024 * 1024

_NT = (((1,), (1,)), ((), ()))


def _gelu(x):
    return 0.5 * x * (1.0 + lax.erf(x * (2.0 ** -0.5)))


def _rms(x, g):
    return x * lax.rsqrt(jnp.mean(x * x, axis=-1, keepdims=True) + EPS) * g


def _block_rotate(arrs):
    n = len(arrs)
    blk = lax.broadcasted_iota(I32, arrs[0].shape, 1) // S5_GROUP_CH
    w = [arrs[(-i) % n] for i in range(n)]
    for s in range(n.bit_length() - 1):
        bit = ((blk >> s) & 1) == 1
        w = [jnp.where(bit, w[(i - (1 << s)) % n], w[i]) for i in range(n)]
    return w


def _cpow(er, ei, k):
    mag = jnp.exp(k * er)
    return mag * jnp.cos(k * ei), mag * jnp.sin(k * ei)


def _cmul(a, b):
    return a[0] * b[0] - a[1] * b[1], a[0] * b[1] + a[1] * b[0]


def _s5_tables(lam_re, lam_im, log_dt, b_re, b_im, c_re, c_im, d_skip):
    hp = lax.Precision.HIGH
    tc, ng, nh = S5_TC, S5_GROUPS, S5_GROUP_CH
    lr, li = lam_re.astype(F32), lam_im.astype(F32)
    dt = jnp.exp(log_dt.astype(F32))[..., None]
    er, ei = lr * dt, li * dt
    lb = _cpow(er, ei, 1.0)
    den = lr * lr + li * li
    q = (((lb[0] - 1.0) * lr + lb[1] * li) / den, (lb[1] * lr - (lb[0] - 1.0) * li) / den)
    bbar = _cmul((q[0][..., None], q[1][..., None]), (b_re.astype(F32), b_im.astype(F32)))
    cm = (c_re.astype(F32), c_im.astype(F32))

    kk = np.arange(tc, dtype=np.float32)[:, None, None, None]
    pw = _cpow(er[None], ei[None], kk)
    t = _cmul((cm[0][None], cm[1][None]), (pw[0][:, :, :, None, :], pw[1][:, :, :, None, :]))
    taps = (jnp.einsum('kdghp,dgpi->kdghi', t[0], bbar[0], precision=hp)
            - jnp.einsum('kdghp,dgpi->kdghi', t[1], bbar[1], precision=hp))

    tau = (np.arange(tc)[None, :] - np.arange(ng)[:, None]) % tc
    fw = jnp.transpose(taps[:, 0], (1, 3, 0, 2))
    rv = jnp.transpose(taps[:, 1], (1, 3, 0, 2))
    mid = fw[:, :, 0] + rv[:, :, 0] + np.eye(nh, dtype=np.float32)[None] * d_skip.astype(F32)[:, None, :]
    kcat = jnp.concatenate([rv[:, :, :0:-1], mid[:, :, None], fw[:, :, 1:]], axis=2)
    kcat = kcat.reshape(ng, nh, (2 * tc - 1) * nh)
    m = jnp.stack([
        jnp.roll(jnp.stack([kcat[g, :, (tc - 1 - t) * nh:(2 * tc - 1 - t) * nh] for t in tau[g]], axis=0),
                 nh * g, axis=-1)
        for g in range(ng)], axis=0).reshape(ng, tc * nh, tc * nh)

    tauf = tau.astype(np.float32)
    bt = (jnp.transpose(bbar[0], (0, 1, 3, 2)), jnp.transpose(bbar[1], (0, 1, 3, 2)))
    parts = []
    for d, kin in ((0, tc - 1.0 - tauf), (1, tauf)):
        p_in = _cpow(er[d][:, None, None, :], ei[d][:, None, None, :], kin[:, :, None, None])
        parts.append(_cmul(p_in, (bt[0][d][:, None], bt[1][d][:, None])))
    wsum = jnp.concatenate([parts[0][0], parts[1][0], parts[0][1], parts[1][1]], axis=-1)
    wsum = wsum.reshape(ng, tc * nh, 4 * S5_STATE)

    ct = (jnp.transpose(cm[0], (0, 1, 3, 2)), jnp.transpose(cm[1], (0, 1, 3, 2)))
    parts = []
    for d, kout in ((0, tauf + 1.0), (1, tc - tauf)):
        p_out = _cpow(er[d][:, :, None, None], ei[d][:, :, None, None], kout[:, None, :, None])
        parts.append(_cmul(p_out, (ct[0][d][:, :, None, :], ct[1][d][:, :, None, :])))
    wc = jnp.concatenate([parts[0][0], parts[1][0], -parts[0][1], -parts[1][1]], axis=1)
    wc = wc.reshape(ng, 4 * S5_STATE, tc * nh)

    a16 = _cpow(er, ei, float(tc))
    a_re = jnp.concatenate([a16[0][0], a16[0][1]], axis=-1)
    a_im = jnp.concatenate([a16[1][0], a16[1][1]], axis=-1)
    return m.astype(BF16), wsum.astype(BF16), wc.astype(BF16), a_re, a_im


def _s5_in_kernel(x_ref, g1_ref, ws5_ref, wsum_ref, z_ref, sre_ref, sim_ref, s_scr):
    n = TILE_A // S5_TC
    hn = _rms(x_ref[0], g1_ref[...]).astype(BF16)
    s = jnp.dot(hn, ws5_ref[...], preferred_element_type=F32)
    s_scr[0] = s[:, :LANES]
    s_scr[1] = s[:, LANES:]
    rows = []
    for t in range(S5_TC):
        a = jnp.concatenate([s_scr[0, pl.ds(t, n, stride=S5_TC), :],
                             s_scr[1, pl.ds(t, n, stride=S5_TC), :]], axis=1)
        rows.append(pltpu.roll(a, S5_GROUP_CH * t, 1) if t else a)
    zs = _block_rotate(rows)
    for g in range(S5_GROUPS):
        zb = zs[g].astype(BF16)
        z_ref[g] = zb
        sm = jnp.dot(zb, wsum_ref[g], preferred_element_type=F32)
        sre_ref[pl.ds(g, n, stride=S5_GROUPS), :] = sm[:, :LANES]
        sim_ref[pl.ds(g, n, stride=S5_GROUPS), :] = sm[:, LANES:]


def _s5_in(x, g1, ws5, wsum):
    nt = SEQ // TILE_A
    n = TILE_A // S5_TC
    return pl.pallas_call(
        _s5_in_kernel,
        grid=(BATCH, nt),
        in_specs=[
            pl.BlockSpec((1, TILE_A, D_MODEL), lambda b, i: (b, i, 0)),
            pl.BlockSpec((1, D_MODEL), lambda b, i: (0, 0)),
            pl.BlockSpec((D_MODEL, S5_WIDTH), lambda b, i: (0, O_S5 // S5_WIDTH)),
            pl.BlockSpec((S5_GROUPS, S5_WIDTH, S5_WIDTH), lambda b, i: (0, 0, 0)),
        ],
        out_specs=[
            pl.BlockSpec((S5_GROUPS, n, S5_WIDTH), lambda b, i: (0, b * nt + i, 0)),
            pl.BlockSpec((n * S5_GROUPS, LANES), lambda b, i: (i, b)),
            pl.BlockSpec((n * S5_GROUPS, LANES), lambda b, i: (i, b)),
        ],
        out_shape=[
            jax.ShapeDtypeStruct((S5_GROUPS, BATCH * S5_NCH, S5_WIDTH), BF16),
            jax.ShapeDtypeStruct((S5_NCH * S5_GROUPS, BATCH * LANES), F32),
            jax.ShapeDtypeStruct((S5_NCH * S5_GROUPS, BATCH * LANES), F32),
        ],
        scratch_shapes=[pltpu.VMEM((2, TILE_A, LANES), F32)],
        compiler_params=pltpu.CompilerParams(
            dimension_semantics=("parallel", "parallel"), vmem_limit_bytes=VMEM_LIMIT),
        name="s5_in",
    )(x, g1, ws5, wsum)


def _s5_scan_kernel(sre_ref, sim_ref, are_ref, aim_ref, xre_ref, xim_ref):
    ar = are_ref[...]
    ai = aim_ref[...]
    fwd = (lax.broadcasted_iota(I32, ar.shape, 1) % LANES) < S5_STATE
    zero = jnp.zeros_like(ar)

    def step(c, xr, xi):
        sr = sre_ref[c]
        si = sim_ref[c]
        return ar * xr - ai * xi + sr, ar * xi + ai * xr + si

    def fwd_body(c, carry):
        xr, xi = carry
        xre_ref[c] = xr
        xim_ref[c] = xi
        return step(c, xr, xi)

    lax.fori_loop(0, S5_NCH, fwd_body, (zero, zero))

    def rev_body(i, carry):
        xr, xi = carry
        c = S5_NCH - 1 - i
        xre_ref[c] = jnp.where(fwd, xre_ref[c], xr)
        xim_ref[c] = jnp.where(fwd, xim_ref[c], xi)
        return step(c, xr, xi)

    lax.fori_loop(0, S5_NCH, rev_body, (zero, zero))


def _s5_scan(s_re, s_im, a_re, a_im):
    nl = BATCH * LANES
    blk3 = pl.BlockSpec((S5_NCH, S5_GROUPS, SCAN_LANES), lambda j: (0, 0, j))
    blk2 = pl.BlockSpec((S5_GROUPS, SCAN_LANES), lambda j: (0, j))
    return pl.pallas_call(
        _s5_scan_kernel,
        grid=(nl // SCAN_LANES,),
        in_specs=[blk3, blk3, blk2, blk2],
        out_specs=[blk3, blk3],
        out_shape=[jax.ShapeDtypeStruct((S5_NCH, S5_GROUPS, nl), F32)] * 2,
        compiler_params=pltpu.CompilerParams(
            dimension_semantics=("parallel",), vmem_limit_bytes=VMEM_LIMIT),
        name="s5_scan",
    )(s_re, s_im, a_re, a_im)


def _s5_out_kernel(z_ref, xre_ref, xim_ref, m_ref, wc_ref, y_ref):
    n = S5_NCH
    ys = []
    for g in range(S5_GROUPS):
        xc = jnp.concatenate([xre_ref[pl.ds(g, n, stride=S5_GROUPS), :],
                              xim_ref[pl.ds(g, n, stride=S5_GROUPS), :]], axis=1).astype(BF16)
        ys.append(jnp.dot(z_ref[g], m_ref[g], preferred_element_type=F32)
                  + jnp.dot(xc, wc_ref[g], preferred_element_type=F32))
    ts = _block_rotate(ys)
    for t in range(S5_TC):
        acc = ts[t]
        if t:
            acc = pltpu.roll(acc, S5_WIDTH - S5_GROUP_CH * t, 1)
        y_ref[0, 0, pl.ds(t, n, stride=S5_TC), :] = acc[:, :LANES]
        y_ref[0, 1, pl.ds(t, n, stride=S5_TC), :] = acc[:, LANES:]


def _s5_out(z, xc_re, xc_im, m, wc):
    wspec = pl.BlockSpec((S5_GROUPS, S5_WIDTH, S5_WIDTH), lambda b: (0, 0, 0))
    xspec = pl.BlockSpec((S5_NCH * S5_GROUPS, LANES), lambda b: (0, b))
    return pl.pallas_call(
        _s5_out_kernel,
        grid=(BATCH,),
        in_specs=[pl.BlockSpec((S5_GROUPS, S5_NCH, S5_WIDTH), lambda b: (0, b, 0)), xspec, xspec, wspec, wspec],
        out_specs=pl.BlockSpec((1, 2, SEQ, LANES), lambda b: (b, 0, 0, 0)),
        out_shape=jax.ShapeDtypeStruct((BATCH, 2, SEQ, LANES), F32),
        compiler_params=pltpu.CompilerParams(
            dimension_semantics=("parallel",), vmem_limit_bytes=VMEM_LIMIT),
        name="s5_out",
    )(z, xc_re, xc_im, m, wc)


def _mixer_kernel(x_ref, y_ref, g1_ref, win_ref, bga_ref, bgb_ref, lng_ref, lnb_ref,
                  ws_ref, bs_ref, wglu_ref, bglu_ref, wupa_ref, wupb_ref, wout_ref, g2_ref, wr_ref,
                  ewg_ref, ewu_ref, ewd_ref,
                  x1_ref, h2p_ref, pr_ref, owg_ref, owu_ref, owd_ref, bra_scr, mrg_scr):
    for src, dst in ((ewg_ref, owg_ref), (ewu_ref, owu_ref), (ewd_ref, owd_ref)):
        dst[...] = src[...].astype(BF16)

    x = x_ref[0]
    hn = _rms(x, g1_ref[...]).astype(BF16)

    def in_proj(c0):
        return jnp.dot(hn, win_ref[:, c0:c0 + MIX_COLS], preferred_element_type=F32)

    nuv = GMLP_WIDTH // MIX_COLS
    us = [_gelu(in_proj(c * MIX_COLS)) for c in range(nuv)]
    vs = [_gelu(in_proj(GMLP_WIDTH + c * MIX_COLS)) for c in range(nuv)]
    mu = sum(jnp.sum(v, axis=-1, keepdims=True) for v in vs) * (1.0 / GMLP_WIDTH)
    vcs = [v - mu for v in vs]
    var = sum(jnp.sum(vc * vc, axis=-1, keepdims=True) for vc in vcs) * (1.0 / GMLP_WIDTH)
    rstd = lax.rsqrt(var + EPS)
    vns = [(vc * rstd * lng_ref[:, c * MIX_COLS:(c + 1) * MIX_COLS]
            + lnb_ref[:, c * MIX_COLS:(c + 1) * MIX_COLS]).astype(BF16) for c, vc in enumerate(vcs)]
    low = lax.broadcasted_iota(I32, (CHUNK, LANES), 1) < GMLP_HEAD
    per = MIX_COLS // LANES
    for ch in range(TILE_M // CHUNK):
        r0 = ch * CHUNK
        for q in range(GMLP_WIDTH // LANES):
            c0 = q * LANES
            l0 = (q % per) * LANES
            vb = vns[q // per][r0:r0 + CHUNK, l0:l0 + LANES]
            z0 = jnp.dot(ws_ref[2 * q], vb, preferred_element_type=F32)
            z1 = jnp.dot(ws_ref[2 * q + 1], vb, preferred_element_type=F32)
            zz = jnp.where(low, z0, z1) + bs_ref[:, c0:c0 + LANES]
            bra_scr[r0:r0 + CHUNK, c0:c0 + LANES] = (us[q // per][r0:r0 + CHUNK, l0:l0 + LANES] * zz).astype(BF16)

    yb = _gelu(jnp.concatenate([y_ref[0, 0], y_ref[0, 1]], axis=1))
    gl = jnp.dot(yb.astype(BF16), wglu_ref[...], preferred_element_type=F32) + bglu_ref[...]
    brb = (yb * jax.nn.sigmoid(gl)).astype(BF16)

    bra = bra_scr[...]
    for c in range(D_MODEL // MIX_COLS):
        cs = slice(c * MIX_COLS, (c + 1) * MIX_COLS)
        ga = jax.nn.sigmoid(in_proj(O_GATE + c * MIX_COLS) + bga_ref[:, cs])
        gb = jax.nn.sigmoid(in_proj(O_GATE + D_MODEL + c * MIX_COLS) + bgb_ref[:, cs])
        mrg_scr[:, cs] = (ga * jnp.dot(bra, wupa_ref[:, cs], preferred_element_type=F32)
                          + gb * jnp.dot(brb, wupb_ref[:, cs], preferred_element_type=F32)).astype(BF16)
    mrg = mrg_scr[...]
    ssq = jnp.zeros((TILE_M, 1), F32)
    for c in range(D_MODEL // MIX_COLS):
        cs = slice(c * MIX_COLS, (c + 1) * MIX_COLS)
        x1c = x[:, cs] + jnp.dot(mrg, wout_ref[:, cs], preferred_element_type=F32)
        x1_ref[0, :, cs] = x1c
        ssq = ssq + jnp.sum(x1c * x1c, axis=-1, keepdims=True)

    h2 = x1_ref[0] * lax.rsqrt(ssq * (1.0 / D_MODEL) + EPS) * g2_ref[...]
    half = D_MODEL // 2
    h2p_ref[0] = pltpu.pack_elementwise([h2[:, :half], h2[:, half:]], packed_dtype=BF16)

    lg = lax.dot_general(wr_ref[...], h2.astype(BF16), _NT, preferred_element_type=F32)
    logits = lg[:N_EXPERTS] + lg[N_EXPERTS:]
    mx = jnp.max(logits, axis=0, keepdims=True)
    ex = jnp.exp(logits - mx)
    pr_ref[0] = ex / jnp.sum(ex, axis=0, keepdims=True)


def _mixer(x, y, consts, expert_w):
    nt = SEQ // TILE_M
    nstep = BATCH * nt

    def cspec(a):
        nd = a.ndim
        return pl.BlockSpec(a.shape, lambda b, i: (0,) * nd, pipeline_mode=pl.Buffered(1))

    flat = [w.reshape(-1, w.shape[-1]) for w in expert_w]
    wspecs = [pl.BlockSpec((w.shape[0] // nstep, w.shape[1]), lambda b, i: (b * nt + i, 0)) for w in flat]
    outs = pl.pallas_call(
        _mixer_kernel,
        grid=(BATCH, nt),
        in_specs=[pl.BlockSpec((1, TILE_M, D_MODEL), lambda b, i: (b, i, 0)),
                  pl.BlockSpec((1, 2, TILE_M, LANES), lambda b, i: (b, 0, i, 0))]
                 + [cspec(a) for a in consts] + wspecs,
        out_specs=[pl.BlockSpec((1, TILE_M, D_MODEL), lambda b, i: (b, i, 0)),
                   pl.BlockSpec((1, TILE_M, D_MODEL // 2), lambda b, i: (b, i, 0)),
                   pl.BlockSpec((1, N_EXPERTS, TILE_M), lambda b, i: (b, 0, i))] + wspecs,
        out_shape=[jax.ShapeDtypeStruct((BATCH, SEQ, D_MODEL), F32),
                   jax.ShapeDtypeStruct((BATCH, SEQ, D_MODEL // 2), U32),
                   jax.ShapeDtypeStruct((BATCH, N_EXPERTS, SEQ), F32)]
                  + [jax.ShapeDtypeStruct(w.shape, BF16) for w in flat],
        scratch_shapes=[pltpu.VMEM((TILE_M, GMLP_WIDTH), BF16), pltpu.VMEM((TILE_M, D_MODEL), BF16)],
        compiler_params=pltpu.CompilerParams(
            dimension_semantics=("parallel", "parallel"), vmem_limit_bytes=VMEM_LIMIT_MIXER),
        name="mixer",
    )(x, y, *consts, *flat)
    return outs[:3], [o.reshape(w.shape) for o, w in zip(outs[3:], expert_w)]


TOPK_NBLK = SEQ // LANES
TOPK_ROWS = 4 * N_EXPERTS
TOPK_SEARCH_STEPS = 28


def _lane_sum(x):
    return jnp.sum(x, axis=1, keepdims=True)


def _topk_kernel(p_ref, tri_ref, idx_ref, gate_ref, eqc_scr, loc_scr, pj_scr, cb_scr, cbp_scr):
    p = p_ref[...]

    def count_ge(th):
        return _lane_sum(jnp.where(p >= th, 1.0, 0.0))

    def search(_, c):
        lo, hi = c
        w = hi - lo
        m1, m2, m3 = lo + 0.25 * w, lo + 0.5 * w, lo + 0.75 * w
        c1, c2, c3 = count_ge(m1) >= CAPACITY, count_ge(m2) >= CAPACITY, count_ge(m3) >= CAPACITY
        return (jnp.where(c3, m3, jnp.where(c2, m2, jnp.where(c1, m1, lo))),
                jnp.where(c1, jnp.where(c2, jnp.where(c3, hi, m3), m2), m1))

    lo, hi = lax.fori_loop(0, TOPK_SEARCH_STEPS, search,
                           (jnp.zeros((TOPK_ROWS, 1), F32), jnp.full((TOPK_ROWS, 1), 2.0, F32)))
    gt = p >= hi
    eq = (p >= lo) & (p < hi)
    need = CAPACITY - _lane_sum(jnp.where(gt, 1.0, 0.0))
    tri = tri_ref[...]
    eqf = jnp.where(eq, 1.0, 0.0)
    carry = jnp.zeros((TOPK_ROWS, 1), F32)
    for j in range(TOPK_NBLK):
        sl = slice(j * LANES, (j + 1) * LANES)
        cs = jnp.dot(eqf[:, sl].astype(BF16), tri, preferred_element_type=F32) + carry
        eqc_scr[:, sl] = cs
        carry = cs[:, LANES - 1:LANES]
    sel = gt | (eq & ((eqc_scr[...] - eqf) < need))
    self = jnp.where(sel, 1.0, 0.0)

    lane = lax.broadcasted_iota(I32, (TOPK_ROWS, LANES), 1)
    cb = jnp.full((TOPK_ROWS, LANES), float(2 * SEQ), F32)
    cbp = cb
    carry = jnp.zeros((TOPK_ROWS, 1), F32)
    for j in range(TOPK_NBLK):
        sl = slice(j * LANES, (j + 1) * LANES)
        loc = jnp.dot(self[:, sl].astype(BF16), tri, preferred_element_type=F32)
        loc_scr[j * TOPK_ROWS:(j + 1) * TOPK_ROWS, :] = loc
        pj_scr[j * TOPK_ROWS:(j + 1) * TOPK_ROWS, :] = p[:, sl]
        cbp = jnp.where(lane == j, carry, cbp)
        carry = carry + loc[:, LANES - 1:LANES]
        cb = jnp.where(lane == j, carry, cb)
    padr = jnp.zeros((LANES - TOPK_ROWS, LANES), F32)
    cb_scr[...] = jnp.concatenate([cb, padr], axis=0).T
    cbp_scr[...] = jnp.concatenate([cbp, padr], axis=0).T

    slot = (lax.broadcasted_iota(I32, (1, CAPACITY), 1) + 1).astype(F32)
    lane_r = lax.broadcasted_iota(I32, (LANES, LANES), 1)
    sub_f = lax.broadcasted_iota(I32, (LANES, 1), 0).astype(F32)
    pad = jnp.zeros((LANES - TOPK_NBLK, LANES), F32)

    def block_rows_t(ref, e):
        return jnp.concatenate([ref[pl.ds(e, TOPK_NBLK, stride=TOPK_ROWS), :], pad], axis=0).T

    def sub_sum(x):
        return jnp.sum(x, axis=0, keepdims=True)

    def per_expert(e, _):
        cb_e = _lane_sum(jnp.where(lane_r == e, cb_scr[...], 0.0))
        cbp_e = _lane_sum(jnp.where(lane_r == e, cbp_scr[...], 0.0))
        hit = (cbp_e < slot) & (slot <= cb_e)
        hitb = jnp.where(hit, 1.0, 0.0).astype(BF16)
        both = sub_sum(jnp.where(hit, cbp_e * float(LANES) + sub_f, 0.0))
        before = jnp.floor(both * (1.0 / LANES))
        jblk = both - before * float(LANES)
        rank = slot - before
        loc = jnp.dot(block_rows_t(loc_scr, e).astype(BF16), hitb, preferred_element_type=F32)
        off = sub_sum(jnp.where(loc < rank, 1.0, 0.0))
        pe = block_rows_t(pj_scr, e)
        hi = pe.astype(BF16)
        r1 = pe - hi.astype(F32)
        mid = r1.astype(BF16)
        lo = (r1 - mid.astype(F32)).astype(BF16)
        pg = (jnp.dot(hi, hitb, preferred_element_type=F32) + jnp.dot(mid, hitb, preferred_element_type=F32)
              + jnp.dot(lo, hitb, preferred_element_type=F32))
        idx_ref[pl.ds(e, 1), :] = (jblk * float(LANES) + off).astype(I32)
        gate_ref[pl.ds(e, 1), :] = sub_sum(jnp.where(sub_f == off, pg, 0.0))
        return 0

    lax.fori_loop(0, TOPK_ROWS, per_expert, 0, unroll=4)


def _topk(probs, tri):
    nrow = BATCH * N_EXPERTS
    return pl.pallas_call(
        _topk_kernel,
        grid=(nrow // TOPK_ROWS,),
        in_specs=[pl.BlockSpec((TOPK_ROWS, SEQ), lambda i: (i, 0)),
                  pl.BlockSpec((LANES, LANES), lambda i: (0, 0))],
        out_specs=[pl.BlockSpec((TOPK_ROWS, CAPACITY), lambda i: (i, 0)),
                   pl.BlockSpec((TOPK_ROWS, CAPACITY), lambda i: (i, 0))],
        out_shape=[jax.ShapeDtypeStruct((nrow, CAPACITY), I32),
                   jax.ShapeDtypeStruct((nrow, CAPACITY), F32)],
        scratch_shapes=[pltpu.VMEM((TOPK_ROWS, SEQ), F32),
                        pltpu.VMEM((TOPK_NBLK * TOPK_ROWS, LANES), F32),
                        pltpu.VMEM((TOPK_NBLK * TOPK_ROWS, LANES), F32),
                        pltpu.VMEM((LANES, LANES), F32),
                        pltpu.VMEM((LANES, LANES), F32)],
        compiler_params=pltpu.CompilerParams(
            dimension_semantics=("parallel",), vmem_limit_bytes=VMEM_LIMIT),
        name="topk",
    )(probs, tri)


MOE_GROUP = 8
MOE_FCHUNK = 512
MOE_NBUF = 1 if TILE_F == EXPERT_FF else 2


def _gather_rows(idx_ref, ibase, h2p_ref, dst_ref, dbase, nrows):
    for k in range(nrows):
        r = idx_ref[0, 0, ibase + k]
        dst_ref[pl.ds(dbase + k, 1), :] = h2p_ref[0, pl.ds(r, 1), :]


def _scatter_rows(idx_ref, gate_ref, ibase, scale, y_ref, ybase, out_ref, nrows):
    for k0 in range(0, nrows, MOE_GROUP):
        rs, vals = [], []
        for k in range(k0, k0 + MOE_GROUP):
            r = idx_ref[0, 0, ibase + k]
            g = gate_ref[0, 0, ibase + k] * scale
            rs.append(r)
            vals.append(out_ref[0, pl.ds(r, 1), :] + g * y_ref[pl.ds(ybase + k, 1), :])
        for r, v in zip(rs, vals):
            out_ref[0, pl.ds(r, 1), :] = v


def _moe_kernel(idx_ref, gate_ref, h2p_ref, x1_ref, gf_ref, wg_ref, wu_ref, wd_ref, out_ref, xg_scr, xs_scr, y_scr):
    e = pl.program_id(1)
    f = pl.program_id(2)
    nf = EXPERT_FF // TILE_F
    nx = SEQ // TILE_X
    step = e * nf + f
    rq = CAPACITY // nf
    cur = e % MOE_NBUF
    nxt = (e + 1) % MOE_NBUF
    xg_cur, xg_nxt = xg_scr.at[cur], xg_scr.at[nxt]
    y_cur, y_nxt = y_scr.at[cur], y_scr.at[nxt]

    @pl.when(step == 0)
    def _():
        out_ref[...] = jnp.zeros_like(out_ref)
        y_scr[...] = jnp.zeros_like(y_scr)

        def gather(i, _):
            base = pl.multiple_of(i * 32, 32)
            _gather_rows(idx_ref, base, h2p_ref, xg_cur, base, 32)
            return 0

        lax.fori_loop(0, CAPACITY // 32, gather, 0)

    w = xg_cur[...]
    half = D_MODEL // 2
    for i in range(2):
        xs_scr[:, i * half:(i + 1) * half] = pltpu.unpack_elementwise(
            w, index=i, packed_dtype=BF16, unpacked_dtype=F32).astype(BF16)
    xrows = pl.ds(pl.multiple_of(jnp.minimum(step, nx - 1) * TILE_X, TILE_X), TILE_X)
    out_ref[0, xrows, :] = out_ref[0, xrows, :] + jnp.where(step < nx, x1_ref[0], 0.0)
    qbase = pl.multiple_of(f * rq, rq)
    e_prev = (e + N_EXPERTS - 1) % N_EXPERTS
    e_next = (e + 1) % N_EXPERTS
    scale = jnp.where(e >= 1, 1.0, 0.0).astype(F32)
    _scatter_rows(idx_ref, gate_ref, e_prev * CAPACITY + qbase, scale, y_nxt, qbase, out_ref, rq)
    _gather_rows(idx_ref, e_next * CAPACITY + qbase, h2p_ref, xg_nxt, qbase, rq)

    xs = xs_scr[...]
    y = jnp.where(f > 0, y_cur[...], 0.0) if nf > 1 else None
    for c0 in range(0, TILE_F, MOE_FCHUNK):
        cs = slice(c0, c0 + MOE_FCHUNK)
        a = jnp.dot(xs, wg_ref[0, :, cs], preferred_element_type=F32)
        u = jnp.dot(xs, wu_ref[0, :, cs], preferred_element_type=F32)
        hm = (a * jax.nn.sigmoid(a) * u).astype(BF16)
        yc = jnp.dot(hm, wd_ref[0, cs, :], preferred_element_type=F32)
        y = yc if y is None else y + yc
    y_cur[...] = y

    @pl.when((e == N_EXPERTS - 1) & (f == nf - 1))
    def _():
        def scatter(i, _):
            base = pl.multiple_of(i * 32, 32)
            _scatter_rows(idx_ref, gate_ref, e * CAPACITY + base, jnp.float32(1.0), y_cur, base, out_ref, 32)
            return 0

        lax.fori_loop(0, CAPACITY // 32, scatter, 0)

        def norm(i, _):
            rows = pl.ds(pl.multiple_of(i * TILE_X, TILE_X), TILE_X)
            out_ref[0, rows, :] = _rms(out_ref[0, rows, :], gf_ref[...])
            return 0

        lax.fori_loop(0, SEQ // TILE_X, norm, 0)


def _moe(idx, gate, h2p, x1, gf, wg, wu, wd):
    nf = EXPERT_FF // TILE_F
    nx = SEQ // TILE_X
    assert nx <= N_EXPERTS * nf
    sspec = pl.BlockSpec((1, 1, N_EXPERTS * CAPACITY), lambda b, e, f: (b, 0, 0), memory_space=pltpu.SMEM)
    return pl.pallas_call(
        _moe_kernel,
        grid=(BATCH, N_EXPERTS, nf),
        in_specs=[sspec, sspec,
                  pl.BlockSpec((1, SEQ, D_MODEL // 2), lambda b, e, f: (b, 0, 0), pipeline_mode=pl.Buffered(1)),
                  pl.BlockSpec((1, TILE_X, D_MODEL), lambda b, e, f: (b, jnp.minimum(e * nf + f, nx - 1), 0)),
                  pl.BlockSpec((1, D_MODEL), lambda b, e, f: (0, 0)),
                  pl.BlockSpec((1, D_MODEL, TILE_F), lambda b, e, f: (e, 0, f)),
                  pl.BlockSpec((1, D_MODEL, TILE_F), lambda b, e, f: (e, 0, f)),
                  pl.BlockSpec((1, TILE_F, D_MODEL), lambda b, e, f: (e, f, 0))],
        out_specs=pl.BlockSpec((1, SEQ, D_MODEL), lambda b, e, f: (b, 0, 0), pipeline_mode=pl.Buffered(1)),
        out_shape=jax.ShapeDtypeStruct((BATCH, SEQ, D_MODEL), F32),
        scratch_shapes=[pltpu.VMEM((MOE_NBUF, CAPACITY, D_MODEL // 2), U32),
                        pltpu.VMEM((CAPACITY, D_MODEL), BF16),
                        pltpu.VMEM((MOE_NBUF, CAPACITY, D_MODEL), F32)],
        compiler_params=pltpu.CompilerParams(
            dimension_semantics=("parallel", "arbitrary", "arbitrary"), vmem_limit_bytes=VMEM_LIMIT_MOE),
        name="moe",
    )(idx.reshape(BATCH, 1, N_EXPERTS * CAPACITY), gate.reshape(BATCH, 1, N_EXPERTS * CAPACITY), h2p, x1, gf,
      wg, wu, wd)


def kernel(x, norm1_g, w_in, b_gate, gmlp_ln_g, gmlp_ln_b, gmlp_w_s, gmlp_b_s, s5_lam_re, s5_lam_im, s5_log_dt,
           s5_b_re, s5_b_im, s5_c_re, s5_c_im, s5_d, s5_w_glu, s5_b_glu, w_up_a, w_up_b, w_out, norm2_g,
           w_router, w_gate, w_up, w_down, final_g):
    assert x.shape == (BATCH, SEQ, D_MODEL) and w_in.shape[0] == 1
    l = 0
    row = lambda a: a.astype(F32).reshape(1, -1)

    w_in_b = w_in[l].astype(BF16)
    m_mat, wsum, wc, a_re, a_im = _s5_tables(s5_lam_re[l], s5_lam_im[l], s5_log_dt[l], s5_b_re[l], s5_b_im[l],
                                             s5_c_re[l], s5_c_im[l], s5_d[l])
    g1 = row(norm1_g[l])

    z, s_re, s_im = _s5_in(x, g1, w_in_b, wsum)
    shape3 = (S5_NCH, S5_GROUPS, BATCH * LANES)
    xc_re, xc_im = _s5_scan(s_re.reshape(shape3), s_im.reshape(shape3),
                            jnp.tile(a_re, (1, BATCH)), jnp.tile(a_im, (1, BATCH)))
    shape2 = (S5_NCH * S5_GROUPS, BATCH * LANES)
    y = _s5_out(z, xc_re.reshape(shape2), xc_im.reshape(shape2), m_mat, wc)

    wr_t = jnp.transpose(w_router[l].astype(F32))
    wr_hi = wr_t.astype(BF16)
    wr_lo = (wr_t - wr_hi.astype(F32)).astype(BF16)
    consts = (
        g1, w_in_b, row(b_gate[l][:D_MODEL]), row(b_gate[l][D_MODEL:]), row(gmlp_ln_g[l]), row(gmlp_ln_b[l]),
        gmlp_w_s[l].astype(BF16), jnp.repeat(jnp.transpose(gmlp_b_s[l].astype(F32)), GMLP_HEAD, axis=1),
        s5_w_glu[l].astype(BF16), row(s5_b_glu[l]), w_up_a[l].astype(BF16), w_up_b[l].astype(BF16),
        w_out[l].astype(BF16), row(norm2_g[l]), jnp.concatenate([wr_hi, wr_lo], axis=0),
    )
    (x1, h2p, probs), (wg_b, wu_b, wd_b) = _mixer(x, y, consts, (w_gate[l], w_up[l], w_down[l]))

    tri = (jnp.arange(LANES)[:, None] <= jnp.arange(LANES)[None, :]).astype(BF16)
    idx, gate = _topk(probs.reshape(BATCH * N_EXPERTS, SEQ), tri)
    return _moe(idx, gate, h2p, x1, row(final_g), wg_b, wu_b, wd_b)
```

```python
import jax
import jax.numpy as jnp
import numpy as np
from jax import lax
from jax.experimental import pallas as pl
from jax.experimental.pallas import tpu as pltpu

F32 = jnp.float32
BF16 = jnp.bfloat16
I32 = jnp.int32
U32 = jnp.uint32

D_MODEL = 1024
BATCH = 16
SEQ = 4096
GMLP_WIDTH = 512
GMLP_GROUPS = 8
GMLP_HEAD = GMLP_WIDTH // GMLP_GROUPS
CHUNK = 128
S5_WIDTH = 256
S5_GROUP_CH = 16
S5_GROUPS = S5_WIDTH // S5_GROUP_CH
S5_STATE = 64
N_EXPERTS = 16
CAPACITY = 2 * SEQ // N_EXPERTS
EXPERT_FF = 2048
EPS = 1e-6
O_S5 = 2 * GMLP_WIDTH
O_GATE = O_S5 + S5_WIDTH

LANES = 128
S5_TC = 16
S5_NCH = SEQ // S5_TC
TILE_A = 1024
TILE_M = 1024
MIX_COLS = 256
TILE_F = 2048
TILE_X = 256
SCAN_LANES = 256
VMEM_LIMIT = 56 * 1024 * 1024
VMEM_LIMIT_MIXER = 60 * 1024 * 1024
VMEM_LIMIT_MOE = 62 * 1024 * 1024

_NT = (((1,), (1,)), ((), ()))


def _gelu(x):
    return 0.5 * x * (1.0 + lax.erf(x * (2.0 ** -0.5)))


def _rms(x, g):
    return x * lax.rsqrt(jnp.mean(x * x, axis=-1, keepdims=True) + EPS) * g


def _block_rotate(arrs):
    n = len(arrs)
    blk = lax.broadcasted_iota(I32, arrs[0].shape, 1) // S5_GROUP_CH
    w = [arrs[(-i) % n] for i in range(n)]
    for s in range(n.bit_length() - 1):
        bit = ((blk >> s) & 1) == 1
        w = [jnp.where(bit, w[(i - (1 << s)) % n], w[i]) for i in range(n)]
    return w


def _cpow(er, ei, k):
    mag = jnp.exp(k * er)
    return mag * jnp.cos(k * ei), mag * jnp.sin(k * ei)


def _cmul(a, b):
    return a[0] * b[0] - a[1] * b[1], a[0] * b[1] + a[1] * b[0]


def _s5_tables(lam_re, lam_im, log_dt, b_re, b_im, c_re, c_im, d_skip):
    hp = lax.Precision.HIGH
    tc, ng, nh = S5_TC, S5_GROUPS, S5_GROUP_CH
    lr, li = lam_re.astype(F32), lam_im.astype(F32)
    dt = jnp.exp(log_dt.astype(F32))[..., None]
    er, ei = lr * dt, li * dt
    lb = _cpow(er, ei, 1.0)
    den = lr * lr + li * li
    q = (((lb[0] - 1.0) * lr + lb[1] * li) / den, (lb[1] * lr - (lb[0] - 1.0) * li) / den)
    bbar = _cmul((q[0][..., None], q[1][..., None]), (b_re.astype(F32), b_im.astype(F32)))
    cm = (c_re.astype(F32), c_im.astype(F32))

    kk = np.arange(tc, dtype=np.float32)[:, None, None, None]
    pw = _cpow(er[None], ei[None], kk)
    t = _cmul((cm[0][None], cm[1][None]), (pw[0][:, :, :, None, :], pw[1][:, :, :, None, :]))
    taps = (jnp.einsum('kdghp,dgpi->kdghi', t[0], bbar[0], precision=hp)
            - jnp.einsum('kdghp,dgpi->kdghi', t[1], bbar[1], precision=hp))

    tau = (np.arange(tc)[None, :] - np.arange(ng)[:, None]) % tc
    fw = jnp.transpose(taps[:, 0], (1, 3, 0, 2))
    rv = jnp.transpose(taps[:, 1], (1, 3, 0, 2))
    mid = fw[:, :, 0] + rv[:, :, 0] + np.eye(nh, dtype=np.float32)[None] * d_skip.astype(F32)[:, None, :]
    kcat = jnp.concatenate([rv[:, :, :0:-1], mid[:, :, None], fw[:, :, 1:]], axis=2)
    kcat = kcat.reshape(ng, nh, (2 * tc - 1) * nh)
    m = jnp.stack([
        jnp.roll(jnp.stack([kcat[g, :, (tc - 1 - t) * nh:(2 * tc - 1 - t) * nh] for t in tau[g]], axis=0),
                 nh * g, axis=-1)
        for g in range(ng)], axis=0).reshape(ng, tc * nh, tc * nh)

    tauf = tau.astype(np.float32)
    bt = (jnp.transpose(bbar[0], (0, 1, 3, 2)), jnp.transpose(bbar[1], (0, 1, 3, 2)))
    parts = []
    for d, kin in ((0, tc - 1.0 - tauf), (1, tauf)):
        p_in = _cpow(er[d][:, None, None, :], ei[d][:, None, None, :], kin[:, :, None, None])
        parts.append(_cmul(p_in, (bt[0][d][:, None], bt[1][d][:, None])))
    wsum = jnp.concatenate([parts[0][0], parts[1][0], parts[0][1], parts[1][1]], axis=-1)
    wsum = wsum.reshape(ng, tc * nh, 4 * S5_STATE)

    ct = (jnp.transpose(cm[0], (0, 1, 3, 2)), jnp.transpose(cm[1], (0, 1, 3, 2)))
    parts = []
    for d, kout in ((0, tauf + 1.0), (1, tc - tauf)):
        p_out = _cpow(er[d][:, :, None, None], ei[d][:, :, None, None], kout[:, None, :, None])
        parts.append(_cmul(p_out, (ct[0][d][:, :, None, :], ct[1][d][:, :, None, :])))
    wc = jnp.concatenate([parts[0][0], parts[1][0], -parts[0][1], -parts[1][1]], axis=1)
    wc = wc.reshape(ng, 4 * S5_STATE, tc * nh)

    a16 = _cpow(er, ei, float(tc))
    a_re = jnp.concatenate([a16[0][0], a16[0][1]], axis=-1)
    a_im = jnp.concatenate([a16[1][0], a16[1][1]], axis=-1)
    return m.astype(BF16), wsum.astype(BF16), wc.astype(BF16), a_re, a_im


def _s5_in_kernel(x_ref, g1_ref, ws5_ref, wsum_ref, z_ref, sre_ref, sim_ref, s_scr):
    n = TILE_A // S5_TC
    hn = _rms(x_ref[0], g1_ref[...]).astype(BF16)
    s = jnp.dot(hn, ws5_ref[...], preferred_element_type=F32)
    s_scr[0] = s[:, :LANES]
    s_scr[1] = s[:, LANES:]
    rows = []
    for t in range(S5_TC):
        a = jnp.concatenate([s_scr[0, pl.ds(t, n, stride=S5_TC), :],
                             s_scr[1, pl.ds(t, n, stride=S5_TC), :]], axis=1)
        rows.append(pltpu.roll(a, S5_GROUP_CH * t, 1) if t else a)
    zs = _block_rotate(rows)
    for g in range(S5_GROUPS):
        zb = zs[g].astype(BF16)
        z_ref[g] = zb
        sm = jnp.dot(zb, wsum_ref[g], preferred_element_type=F32)
        sre_ref[pl.ds(g, n, stride=S5_GROUPS), :] = sm[:, :LANES]
        sim_ref[pl.ds(g, n, stride=S5_GROUPS), :] = sm[:, LANES:]


def _s5_in(x, g1, ws5, wsum):
    nt = SEQ // TILE_A
    n = TILE_A // S5_TC
    return pl.pallas_call(
        _s5_in_kernel,
        grid=(BATCH, nt),
        in_specs=[
            pl.BlockSpec((1, TILE_A, D_MODEL), lambda b, i: (b, i, 0)),
            pl.BlockSpec((1, D_MODEL), lambda b, i: (0, 0)),
            pl.BlockSpec((D_MODEL, S5_WIDTH), lambda b, i: (0, O_S5 // S5_WIDTH)),
            pl.BlockSpec((S5_GROUPS, S5_WIDTH, S5_WIDTH), lambda b, i: (0, 0, 0)),
        ],
        out_specs=[
            pl.BlockSpec((S5_GROUPS, n, S5_WIDTH), lambda b, i: (0, b * nt + i, 0)),
            pl.BlockSpec((n * S5_GROUPS, LANES), lambda b, i: (i, b)),
            pl.BlockSpec((n * S5_GROUPS, LANES), lambda b, i: (i, b)),
        ],
        out_shape=[
            jax.ShapeDtypeStruct((S5_GROUPS, BATCH * S5_NCH, S5_WIDTH), BF16),
            jax.ShapeDtypeStruct((S5_NCH * S5_GROUPS, BATCH * LANES), F32),
            jax.ShapeDtypeStruct((S5_NCH * S5_GROUPS, BATCH * LANES), F32),
        ],
        scratch_shapes=[pltpu.VMEM((2, TILE_A, LANES), F32)],
        compiler_params=pltpu.CompilerParams(
            dimension_semantics=("parallel", "parallel"), vmem_limit_bytes=VMEM_LIMIT),
        name="s5_in",
    )(x, g1, ws5, wsum)


def _s5_scan_kernel(sre_ref, sim_ref, are_ref, aim_ref, xre_ref, xim_ref):
    ar = are_ref[...]
    ai = aim_ref[...]
    fwd = (lax.broadcasted_iota(I32, ar.shape, 1) % LANES) < S5_STATE
    zero = jnp.zeros_like(ar)

    def step(c, xr, xi):
        sr = sre_ref[c]
        si = sim_ref[c]
        return ar * xr - ai * xi + sr, ar * xi + ai * xr + si

    def fwd_body(c, carry):
        xr, xi = carry
        xre_ref[c] = xr
        xim_ref[c] = xi
        return step(c, xr, xi)

    lax.fori_loop(0, S5_NCH, fwd_body, (zero, zero))

    def rev_body(i, carry):
        xr, xi = carry
        c = S5_NCH - 1 - i
        xre_ref[c] = jnp.where(fwd, xre_ref[c], xr)
        xim_ref[c] = jnp.where(fwd, xim_ref[c], xi)
        return step(c, xr, xi)

    lax.fori_loop(0, S5_NCH, rev_body, (zero, zero))


def _s5_scan(s_re, s_im, a_re, a_im):
    nl = BATCH * LANES
    blk3 = pl.BlockSpec((S5_NCH, S5_GROUPS, SCAN_LANES), lambda j: (0, 0, j))
    blk2 = pl.BlockSpec((S5_GROUPS, SCAN_LANES), lambda j: (0, j))
    return pl.pallas_call(
        _s5_scan_kernel,
        grid=(nl // SCAN_LANES,),
        in_specs=[blk3, blk3, blk2, blk2],
        out_specs=[blk3, blk3],
        out_shape=[jax.ShapeDtypeStruct((S5_NCH, S5_GROUPS, nl), F32)] * 2,
        compiler_params=pltpu.CompilerParams(
            dimension_semantics=("parallel",), vmem_limit_bytes=VMEM_LIMIT),
        name="s5_scan",
    )(s_re, s_im, a_re, a_im)


def _s5_out_kernel(z_ref, xre_ref, xim_ref, m_ref, wc_ref, y_ref):
    n = S5_NCH
    ys = []
    for g in range(S5_GROUPS):
        xc = jnp.concatenate([xre_ref[pl.ds(g, n, stride=S5_GROUPS), :],
                              xim_ref[pl.ds(g, n, stride=S5_GROUPS), :]], axis=1).astype(BF16)
        ys.append(jnp.dot(z_ref[g], m_ref[g], preferred_element_type=F32)
                  + jnp.dot(xc, wc_ref[g], preferred_element_type=F32))
    ts = _block_rotate(ys)
    for t in range(S5_TC):
        acc = ts[t]
        if t:
            acc = pltpu.roll(acc, S5_WIDTH - S5_GROUP_CH * t, 1)
        y_ref[0, 0, pl.ds(t, n, stride=S5_TC), :] = acc[:, :LANES]
        y_ref[0, 1, pl.ds(t, n, stride=S5_TC), :] = acc[:, LANES:]


def _s5_out(z, xc_re, xc_im, m, wc):
    wspec = pl.BlockSpec((S5_GROUPS, S5_WIDTH, S5_WIDTH), lambda b: (0, 0, 0))
    xspec = pl.BlockSpec((S5_NCH * S5_GROUPS, LANES), lambda b: (0, b))
    return pl.pallas_call(
        _s5_out_kernel,
        grid=(BATCH,),
        in_specs=[pl.BlockSpec((S5_GROUPS, S5_NCH, S5_WIDTH), lambda b: (0, b, 0)), xspec, xspec, wspec, wspec],
        out_specs=pl.BlockSpec((1, 2, SEQ, LANES), lambda b: (b, 0, 0, 0)),
        out_shape=jax.ShapeDtypeStruct((BATCH, 2, SEQ, LANES), F32),
        compiler_params=pltpu.CompilerParams(
            dimension_semantics=("parallel",), vmem_limit_bytes=VMEM_LIMIT),
        name="s5_out",
    )(z, xc_re, xc_im, m, wc)


def _mixer_kernel(x_ref, y_ref, g1_ref, win_ref, bga_ref, bgb_ref, lng_ref, lnb_ref,
                  ws_ref, bs_ref, wglu_ref, bglu_ref, wupa_ref, wupb_ref, wout_ref, g2_ref, wr_ref,
                  ewg_ref, ewu_ref, ewd_ref,
                  x1_ref, h2p_ref, pr_ref, owg_ref, owu_ref, owd_ref, bra_scr, mrg_scr):
    for src, dst in ((ewg_ref, owg_ref), (ewu_ref, owu_ref), (ewd_ref, owd_ref)):
        dst[...] = src[...].astype(BF16)

    x = x_ref[0]
    hn = _rms(x, g1_ref[...]).astype(BF16)

    def in_proj(c0):
        return jnp.dot(hn, win_ref[:, c0:c0 + MIX_COLS], preferred_element_type=F32)

    nuv = GMLP_WIDTH // MIX_COLS
    us = [_gelu(in_proj(c * MIX_COLS)) for c in range(nuv)]
    vs = [_gelu(in_proj(GMLP_WIDTH + c * MIX_COLS)) for c in range(nuv)]
    mu = sum(jnp.sum(v, axis=-1, keepdims=True) for v in vs) * (1.0 / GMLP_WIDTH)
    vcs = [v - mu for v in vs]
    var = sum(jnp.sum(vc * vc, axis=-1, keepdims=True) for vc in vcs) * (1.0 / GMLP_WIDTH)
    rstd = lax.rsqrt(var + EPS)
    vns = [(vc * rstd * lng_ref[:, c * MIX_COLS:(c + 1) * MIX_COLS]
            + lnb_ref[:, c * MIX_COLS:(c + 1) * MIX_COLS]).astype(BF16) for c, vc in enumerate(vcs)]
    low = lax.broadcasted_iota(I32, (CHUNK, LANES), 1) < GMLP_HEAD
    per = MIX_COLS // LANES
    for ch in range(TILE_M // CHUNK):
        r0 = ch * CHUNK
        for q in range(GMLP_WIDTH // LANES):
            c0 = q * LANES
            l0 = (q % per) * LANES
            vb = vns[q // per][r0:r0 + CHUNK, l0:l0 + LANES]
            z0 = jnp.dot(ws_ref[2 * q], vb, preferred_element_type=F32)
            z1 = jnp.dot(ws_ref[2 * q + 1], vb, preferred_element_type=F32)
            zz = jnp.where(low, z0, z1) + bs_ref[:, c0:c0 + LANES]
            bra_scr[r0:r0 + CHUNK, c0:c0 + LANES] = (us[q // per][r0:r0 + CHUNK, l0:l0 + LANES] * zz).astype(BF16)

    yb = _gelu(jnp.concatenate([y_ref[0, 0], y_ref[0, 1]], axis=1))
    gl = jnp.dot(yb.astype(BF16), wglu_ref[...], preferred_element_type=F32) + bglu_ref[...]
    brb = (yb * jax.nn.sigmoid(gl)).astype(BF16)

    bra = bra_scr[...]
    for c in range(D_MODEL // MIX_COLS):
        cs = slice(c * MIX_COLS, (c + 1) * MIX_COLS)
        ga = jax.nn.sigmoid(in_proj(O_GATE + c * MIX_COLS) + bga_ref[:, cs])
        gb = jax.nn.sigmoid(in_proj(O_GATE + D_MODEL + c * MIX_COLS) + bgb_ref[:, cs])
        mrg_scr[:, cs] = (ga * jnp.dot(bra, wupa_ref[:, cs], preferred_element_type=F32)
                          + gb * jnp.dot(brb, wupb_ref[:, cs], preferred_element_type=F32)).astype(BF16)
    mrg = mrg_scr[...]
    ssq = jnp.zeros((TILE_M, 1), F32)
    for c in range(D_MODEL // MIX_COLS):
        cs = slice(c * MIX_COLS, (c + 1) * MIX_COLS)
        x1c = x[:, cs] + jnp.dot(mrg, wout_ref[:, cs], preferred_element_type=F32)
        x1_ref[0, :, cs] = x1c
        ssq = ssq + jnp.sum(x1c * x1c, axis=-1, keepdims=True)

    h2 = x1_ref[0] * lax.rsqrt(ssq * (1.0 / D_MODEL) + EPS) * g2_ref[...]
    half = D_MODEL // 2
    h2p_ref[0] = pltpu.pack_elementwise([h2[:, :half], h2[:, half:]], packed_dtype=BF16)

    lg = lax.dot_general(wr_ref[...], h2.astype(BF16), _NT, preferred_element_type=F32)
    logits = lg[:N_EXPERTS] + lg[N_EXPERTS:]
    mx = jnp.max(logits, axis=0, keepdims=True)
    ex = jnp.exp(logits - mx)
    pr_ref[0] = ex / jnp.sum(ex, axis=0, keepdims=True)


def _mixer(x, y, consts, expert_w):
    nt = SEQ // TILE_M
    nstep = BATCH * nt

    def cspec(a):
        nd = a.ndim
        return pl.BlockSpec(a.shape, lambda b, i: (0,) * nd, pipeline_mode=pl.Buffered(1))

    flat = [w.reshape(-1, w.shape[-1]) for w in expert_w]
    wspecs = [pl.BlockSpec((w.shape[0] // nstep, w.shape[1]), lambda b, i: (b * nt + i, 0)) for w in flat]
    outs = pl.pallas_call(
        _mixer_kernel,
        grid=(BATCH, nt),
        in_specs=[pl.BlockSpec((1, TILE_M, D_MODEL), lambda b, i: (b, i, 0)),
                  pl.BlockSpec((1, 2, TILE_M, LANES), lambda b, i: (b, 0, i, 0))]
                 + [cspec(a) for a in consts] + wspecs,
        out_specs=[pl.BlockSpec((1, TILE_M, D_MODEL), lambda b, i: (b, i, 0)),
                   pl.BlockSpec((1, TILE_M, D_MODEL // 2), lambda b, i: (b, i, 0)),
                   pl.BlockSpec((1, N_EXPERTS, TILE_M), lambda b, i: (b, 0, i))] + wspecs,
        out_shape=[jax.ShapeDtypeStruct((BATCH, SEQ, D_MODEL), F32),
                   jax.ShapeDtypeStruct((BATCH, SEQ, D_MODEL // 2), U32),
                   jax.ShapeDtypeStruct((BATCH, N_EXPERTS, SEQ), F32)]
                  + [jax.ShapeDtypeStruct(w.shape, BF16) for w in flat],
        scratch_shapes=[pltpu.VMEM((TILE_M, GMLP_WIDTH), BF16), pltpu.VMEM((TILE_M, D_MODEL), BF16)],
        compiler_params=pltpu.CompilerParams(
            dimension_semantics=("parallel", "parallel"), vmem_limit_bytes=VMEM_LIMIT_MIXER),
        name="mixer",
    )(x, y, *consts, *flat)
    return outs[:3], [o.reshape(w.shape) for o, w in zip(outs[3:], expert_w)]


TOPK_NBLK = SEQ // LANES
TOPK_ROWS = 4 * N_EXPERTS
TOPK_SEARCH_STEPS = 28


def _lane_sum(x):
    return jnp.sum(x, axis=1, keepdims=True)


def _topk_kernel(p_ref, tri_ref, idx_ref, gate_ref, eqc_scr, loc_scr, pj_scr, cb_scr, cbp_scr):
    p = p_ref[...]

    def count_ge(th):
        return _lane_sum(jnp.where(p >= th, 1.0, 0.0))

    def search(_, c):
        lo, hi = c
        w = hi - lo
        m1, m2, m3 = lo + 0.25 * w, lo + 0.5 * w, lo + 0.75 * w
        c1, c2, c3 = count_ge(m1) >= CAPACITY, count_ge(m2) >= CAPACITY, count_ge(m3) >= CAPACITY
        return (jnp.where(c3, m3, jnp.where(c2, m2, jnp.where(c1, m1, lo))),
                jnp.where(c1, jnp.where(c2, jnp.where(c3, hi, m3), m2), m1))

    lo, hi = lax.fori_loop(0, TOPK_SEARCH_STEPS, search,
                           (jnp.zeros((TOPK_ROWS, 1), F32), jnp.full((TOPK_ROWS, 1), 2.0, F32)))
    gt = p >= hi
    eq = (p >= lo) & (p < hi)
    need = CAPACITY - _lane_sum(jnp.where(gt, 1.0, 0.0))
    tri = tri_ref[...]
    eqf = jnp.where(eq, 1.0, 0.0)
    carry = jnp.zeros((TOPK_ROWS, 1), F32)
    for j in range(TOPK_NBLK):
        sl = slice(j * LANES, (j + 1) * LANES)
        cs = jnp.dot(eqf[:, sl].astype(BF16), tri, preferred_element_type=F32) + carry
        eqc_scr[:, sl] = cs
        carry = cs[:, LANES - 1:LANES]
    sel = gt | (eq & ((eqc_scr[...] - eqf) < need))
    self = jnp.where(sel, 1.0, 0.0)

    lane = lax.broadcasted_iota(I32, (TOPK_ROWS, LANES), 1)
    cb = jnp.full((TOPK_ROWS, LANES), float(2 * SEQ), F32)
    cbp = cb
    carry = jnp.zeros((TOPK_ROWS, 1), F32)
    for j in range(TOPK_NBLK):
        sl = slice(j * LANES, (j + 1) * LANES)
        loc = jnp.dot(self[:, sl].astype(BF16), tri, preferred_element_type=F32)
        loc_scr[j * TOPK_ROWS:(j + 1) * TOPK_ROWS, :] = loc
        pj_scr[j * TOPK_ROWS:(j + 1) * TOPK_ROWS, :] = p[:, sl]
        cbp = jnp.where(lane == j, carry, cbp)
        carry = carry + loc[:, LANES - 1:LANES]
        cb = jnp.where(lane == j, carry, cb)
    padr = jnp.zeros((LANES - TOPK_ROWS, LANES), F32)
    cb_scr[...] = jnp.concatenate([cb, padr], axis=0).T
    cbp_scr[...] = jnp.concatenate([cbp, padr], axis=0).T

    slot = (lax.broadcasted_iota(I32, (1, CAPACITY), 1) + 1).astype(F32)
    lane_r = lax.broadcasted_iota(I32, (LANES, LANES), 1)
    sub_f = lax.broadcasted_iota(I32, (LANES, 1), 0).astype(F32)
    pad = jnp.zeros((LANES - TOPK_NBLK, LANES), F32)

    def block_rows_t(ref, e):
        return jnp.concatenate([ref[pl.ds(e, TOPK_NBLK, stride=TOPK_ROWS), :], pad], axis=0).T

    def sub_sum(x):
        return jnp.sum(x, axis=0, keepdims=True)

    def per_expert(e, _):
        cb_e = _lane_sum(jnp.where(lane_r == e, cb_scr[...], 0.0))
        cbp_e = _lane_sum(jnp.where(lane_r == e, cbp_scr[...], 0.0))
        hit = (cbp_e < slot) & (slot <= cb_e)
        hitb = jnp.where(hit, 1.0, 0.0).astype(BF16)
        both = sub_sum(jnp.where(hit, cbp_e * float(LANES) + sub_f, 0.0))
        before = jnp.floor(both * (1.0 / LANES))
        jblk = both - before * float(LANES)
        rank = slot - before
        loc = jnp.dot(block_rows_t(loc_scr, e).astype(BF16), hitb, preferred_element_type=F32)
        off = sub_sum(jnp.where(loc < rank, 1.0, 0.0))
        pe = block_rows_t(pj_scr, e)
        hi = pe.astype(BF16)
        r1 = pe - hi.astype(F32)
        mid = r1.astype(BF16)
        lo = (r1 - mid.astype(F32)).astype(BF16)
        pg = (jnp.dot(hi, hitb, preferred_element_type=F32) + jnp.dot(mid, hitb, preferred_element_type=F32)
              + jnp.dot(lo, hitb, preferred_element_type=F32))
        idx_ref[pl.ds(e, 1), :] = (jblk * float(LANES) + off).astype(I32)
        gate_ref[pl.ds(e, 1), :] = sub_sum(jnp.where(sub_f == off, pg, 0.0))
        return 0

    lax.fori_loop(0, TOPK_ROWS, per_expert, 0, unroll=4)


def _topk(probs, tri):
    nrow = BATCH * N_EXPERTS
    return pl.pallas_call(
        _topk_kernel,
        grid=(nrow // TOPK_ROWS,),
        in_specs=[pl.BlockSpec((TOPK_ROWS, SEQ), lambda i: (i, 0)),
                  pl.BlockSpec((LANES, LANES), lambda i: (0, 0))],
        out_specs=[pl.BlockSpec((TOPK_ROWS, CAPACITY), lambda i: (i, 0)),
                   pl.BlockSpec((TOPK_ROWS, CAPACITY), lambda i: (i, 0))],
        out_shape=[jax.ShapeDtypeStruct((nrow, CAPACITY), I32),
                   jax.ShapeDtypeStruct((nrow, CAPACITY), F32)],
        scratch_shapes=[pltpu.VMEM((TOPK_ROWS, SEQ), F32),
                        pltpu.VMEM((TOPK_NBLK * TOPK_ROWS, LANES), F32),
                        pltpu.VMEM((TOPK_NBLK * TOPK_ROWS, LANES), F32),
                        pltpu.VMEM((LANES, LANES), F32),
                        pltpu.VMEM((LANES, LANES), F32)],
        compiler_params=pltpu.CompilerParams(
            dimension_semantics=("parallel",), vmem_limit_bytes=VMEM_LIMIT),
        name="topk",
    )(probs, tri)


MOE_GROUP = 8
MOE_FCHUNK = 256
MOE_NBUF = 1 if TILE_F == EXPERT_FF else 2


def _gather_rows(idx_ref, ibase, h2p_ref, dst_ref, dbase, nrows):
    for k in range(nrows):
        r = idx_ref[0, 0, ibase + k]
        dst_ref[pl.ds(dbase + k, 1), :] = h2p_ref[0, pl.ds(r, 1), :]


def _scatter_rows(idx_ref, gate_ref, ibase, y_ref, ybase, out_ref, nrows):
    for k0 in range(0, nrows, MOE_GROUP):
        rs, vals = [], []
        for k in range(k0, k0 + MOE_GROUP):
            r = idx_ref[0, 0, ibase + k]
            g = gate_ref[0, 0, ibase + k]
            rs.append(r)
            vals.append(out_ref[0, pl.ds(r, 1), :] + g * y_ref[pl.ds(ybase + k, 1), :])
        for r, v in zip(rs, vals):
            out_ref[0, pl.ds(r, 1), :] = v


def _moe_kernel(idx_ref, gate_ref, h2p_ref, x1_ref, gf_ref, wg_ref, wu_ref, wd_ref, out_ref, xg_scr, xs_scr, y_scr):
    e = pl.program_id(1)
    f = pl.program_id(2)
    nf = EXPERT_FF // TILE_F
    nx = SEQ // TILE_X
    step = e * nf + f
    rq = CAPACITY // nf
    cur = e % MOE_NBUF
    nxt = (e + 1) % MOE_NBUF
    xg_cur, xg_nxt = xg_scr.at[cur], xg_scr.at[nxt]
    y_cur, y_nxt = y_scr.at[cur], y_scr.at[nxt]

    @pl.when(step == 0)
    def _():
        out_ref[...] = jnp.zeros_like(out_ref)
        y_scr[...] = jnp.zeros_like(y_scr)

        def gather(i, _):
            base = pl.multiple_of(i * 32, 32)
            _gather_rows(idx_ref, base, h2p_ref, xg_cur, base, 32)
            return 0

        lax.fori_loop(0, CAPACITY // 32, gather, 0)

    w = xg_cur[...]
    half = D_MODEL // 2
    for i in range(2):
        xs_scr[:, i * half:(i + 1) * half] = pltpu.unpack_elementwise(
            w, index=i, packed_dtype=BF16, unpacked_dtype=F32).astype(BF16)
    xrows = pl.ds(pl.multiple_of(jnp.minimum(step, nx - 1) * TILE_X, TILE_X), TILE_X)
    out_ref[0, xrows, :] = out_ref[0, xrows, :] + jnp.where(step < nx, x1_ref[0], 0.0)
    qbase = pl.multiple_of(f * rq, rq)
    e_prev = (e + N_EXPERTS - 1) % N_EXPERTS
    e_next = (e + 1) % N_EXPERTS
    nchunk = TILE_F // MOE_FCHUNK
    rc = rq // nchunk

    xs = xs_scr[...]
    y = jnp.where(f > 0, y_cur[...], 0.0) if nf > 1 else None

    def gate_up(c0):
        cs = slice(c0, c0 + MOE_FCHUNK)
        return (jnp.dot(xs, wg_ref[0, :, cs], preferred_element_type=F32),
                jnp.dot(xs, wu_ref[0, :, cs], preferred_element_type=F32))

    a, u = gate_up(0)
    for c in range(nchunk):
        c0 = c * MOE_FCHUNK
        rbase = qbase + c * rc
        _scatter_rows(idx_ref, gate_ref, e_prev * CAPACITY + rbase, y_nxt, rbase, out_ref, rc)
        _gather_rows(idx_ref, e_next * CAPACITY + rbase, h2p_ref, xg_nxt, rbase, rc)
        nxt_au = gate_up(c0 + MOE_FCHUNK) if c + 1 < nchunk else None
        hm = (a * jax.nn.sigmoid(a) * u).astype(BF16)
        yc = jnp.dot(hm, wd_ref[0, c0:c0 + MOE_FCHUNK, :], preferred_element_type=F32)
        y = yc if y is None else y + yc
        if nxt_au is not None:
            a, u = nxt_au
    y_cur[...] = y

    @pl.when((e == N_EXPERTS - 1) & (f == nf - 1))
    def _():
        def scatter(i, _):
            base = pl.multiple_of(i * 32, 32)
            _scatter_rows(idx_ref, gate_ref, e * CAPACITY + base, y_cur, base, out_ref, 32)
            return 0

        lax.fori_loop(0, CAPACITY // 32, scatter, 0)

        def norm(i, _):
            rows = pl.ds(pl.multiple_of(i * TILE_X, TILE_X), TILE_X)
            out_ref[0, rows, :] = _rms(out_ref[0, rows, :], gf_ref[...])
            return 0

        lax.fori_loop(0, SEQ // TILE_X, norm, 0)


def _moe(idx, gate, h2p, x1, gf, wg, wu, wd):
    nf = EXPERT_FF // TILE_F
    nx = SEQ // TILE_X
    assert nx <= N_EXPERTS * nf
    sspec = pl.BlockSpec((1, 1, N_EXPERTS * CAPACITY), lambda b, e, f: (b, 0, 0), memory_space=pltpu.SMEM)
    return pl.pallas_call(
        _moe_kernel,
        grid=(BATCH, N_EXPERTS, nf),
        in_specs=[sspec, sspec,
                  pl.BlockSpec((1, SEQ, D_MODEL // 2), lambda b, e, f: (b, 0, 0), pipeline_mode=pl.Buffered(1)),
                  pl.BlockSpec((1, TILE_X, D_MODEL), lambda b, e, f: (b, jnp.minimum(e * nf + f, nx - 1), 0)),
                  pl.BlockSpec((1, D_MODEL), lambda b, e, f: (0, 0)),
                  pl.BlockSpec((1, D_MODEL, TILE_F), lambda b, e, f: (e, 0, f)),
                  pl.BlockSpec((1, D_MODEL, TILE_F), lambda b, e, f: (e, 0, f)),
                  pl.BlockSpec((1, TILE_F, D_MODEL), lambda b, e, f: (e, f, 0))],
        out_specs=pl.BlockSpec((1, SEQ, D_MODEL), lambda b, e, f: (b, 0, 0), pipeline_mode=pl.Buffered(1)),
        out_shape=jax.ShapeDtypeStruct((BATCH, SEQ, D_MODEL), F32),
        scratch_shapes=[pltpu.VMEM((MOE_NBUF, CAPACITY, D_MODEL // 2), U32),
                        pltpu.VMEM((CAPACITY, D_MODEL), BF16),
                        pltpu.VMEM((MOE_NBUF, CAPACITY, D_MODEL), F32)],
        compiler_params=pltpu.CompilerParams(
            dimension_semantics=("parallel", "arbitrary", "arbitrary"), vmem_limit_bytes=VMEM_LIMIT_MOE),
        name="moe",
    )(idx.reshape(BATCH, 1, N_EXPERTS * CAPACITY), gate.reshape(BATCH, 1, N_EXPERTS * CAPACITY), h2p, x1, gf,
      wg, wu, wd)


def kernel(x, norm1_g, w_in, b_gate, gmlp_ln_g, gmlp_ln_b, gmlp_w_s, gmlp_b_s, s5_lam_re, s5_lam_im, s5_log_dt,
           s5_b_re, s5_b_im, s5_c_re, s5_c_im, s5_d, s5_w_glu, s5_b_glu, w_up_a, w_up_b, w_out, norm2_g,
           w_router, w_gate, w_up, w_down, final_g):
    assert x.shape == (BATCH, SEQ, D_MODEL) and w_in.shape[0] == 1
    l = 0
    row = lambda a: a.astype(F32).reshape(1, -1)

    w_in_b = w_in[l].astype(BF16)
    m_mat, wsum, wc, a_re, a_im = _s5_tables(s5_lam_re[l], s5_lam_im[l], s5_log_dt[l], s5_b_re[l], s5_b_im[l],
                                             s5_c_re[l], s5_c_im[l], s5_d[l])
    g1 = row(norm1_g[l])

    z, s_re, s_im = _s5_in(x, g1, w_in_b, wsum)
    shape3 = (S5_NCH, S5_GROUPS, BATCH * LANES)
    xc_re, xc_im = _s5_scan(s_re.reshape(shape3), s_im.reshape(shape3),
                            jnp.tile(a_re, (1, BATCH)), jnp.tile(a_im, (1, BATCH)))
    shape2 = (S5_NCH * S5_GROUPS, BATCH * LANES)
    y = _s5_out(z, xc_re.reshape(shape2), xc_im.reshape(shape2), m_mat, wc)

    wr_t = jnp.transpose(w_router[l].astype(F32))
    wr_hi = wr_t.astype(BF16)
    wr_lo = (wr_t - wr_hi.astype(F32)).astype(BF16)
    consts = (
        g1, w_in_b, row(b_gate[l][:D_MODEL]), row(b_gate[l][D_MODEL:]), row(gmlp_ln_g[l]), row(gmlp_ln_b[l]),
        gmlp_w_s[l].astype(BF16), jnp.repeat(jnp.transpose(gmlp_b_s[l].astype(F32)), GMLP_HEAD, axis=1),
        s5_w_glu[l].astype(BF16), row(s5_b_glu[l]), w_up_a[l].astype(BF16), w_up_b[l].astype(BF16),
        w_out[l].astype(BF16), row(norm2_g[l]), jnp.concatenate([wr_hi, wr_lo], axis=0),
    )
    (x1, h2p, probs), (wg_b, wu_b, wd_b) = _mixer(x, y, consts, (w_gate[l], w_up[l], w_down[l]))

    tri = (jnp.arange(LANES)[:, None] <= jnp.arange(LANES)[None, :]).astype(BF16)
    idx, gate = _topk(probs.reshape(BATCH * N_EXPERTS, SEQ), tri)
    return _moe(idx, gate, h2p, x1, row(final_g), wg_b, wu_b, wd_b)
```

```python
import jax
import jax.numpy as jnp
import numpy as np
from jax import lax
from jax.experimental import pallas as pl
from jax.experimental.pallas import tpu as pltpu

F32 = jnp.float32
BF16 = jnp.bfloat16
I32 = jnp.int32
U32 = jnp.uint32

D_MODEL = 1024
BATCH = 16
SEQ = 4096
GMLP_WIDTH = 512
GMLP_GROUPS = 8
GMLP_HEAD = GMLP_WIDTH // GMLP_GROUPS
CHUNK = 128
S5_WIDTH = 256
S5_GROUP_CH = 16
S5_GROUPS = S5_WIDTH // S5_GROUP_CH
S5_STATE = 64
N_EXPERTS = 16
CAPACITY = 2 * SEQ // N_EXPERTS
EXPERT_FF = 2048
EPS = 1e-6
O_S5 = 2 * GMLP_WIDTH
O_GATE = O_S5 + S5_WIDTH

LANES = 128
S5_TC = 16
S5_NCH = SEQ // S5_TC
TILE_A = 1024
TILE_M = 1024
MIX_COLS = 256
TILE_F = 2048
TILE_X = 256
SCAN_LANES = 256
VMEM_LIMIT = 56 * 1024 * 1024
VMEM_LIMIT_MIXER = 60 * 1024 * 1024
VMEM_LIMIT_MOE = 62 * 1024 * 1024

_NT = (((1,), (1,)), ((), ()))


def _gelu(x):
    return 0.5 * x * (1.0 + lax.erf(x * (2.0 ** -0.5)))


def _rms(x, g):
    return x * lax.rsqrt(jnp.mean(x * x, axis=-1, keepdims=True) + EPS) * g


def _block_rotate(arrs):
    n = len(arrs)
    blk = lax.broadcasted_iota(I32, arrs[0].shape, 1) // S5_GROUP_CH
    w = [arrs[(-i) % n] for i in range(n)]
    for s in range(n.bit_length() - 1):
        bit = ((blk >> s) & 1) == 1
        w = [jnp.where(bit, w[(i - (1 << s)) % n], w[i]) for i in range(n)]
    return w


def _cpow(er, ei, k):
    mag = jnp.exp(k * er)
    return mag * jnp.cos(k * ei), mag * jnp.sin(k * ei)


def _cmul(a, b):
    return a[0] * b[0] - a[1] * b[1], a[0] * b[1] + a[1] * b[0]


def _s5_tables(lam_re, lam_im, log_dt, b_re, b_im, c_re, c_im, d_skip):
    hp = lax.Precision.HIGH
    tc, ng, nh = S5_TC, S5_GROUPS, S5_GROUP_CH
    lr, li = lam_re.astype(F32), lam_im.astype(F32)
    dt = jnp.exp(log_dt.astype(F32))[..., None]
    er, ei = lr * dt, li * dt
    lb = _cpow(er, ei, 1.0)
    den = lr * lr + li * li
    q = (((lb[0] - 1.0) * lr + lb[1] * li) / den, (lb[1] * lr - (lb[0] - 1.0) * li) / den)
    bbar = _cmul((q[0][..., None], q[1][..., None]), (b_re.astype(F32), b_im.astype(F32)))
    cm = (c_re.astype(F32), c_im.astype(F32))

    kk = np.arange(tc, dtype=np.float32)[:, None, None, None]
    pw = _cpow(er[None], ei[None], kk)
    t = _cmul((cm[0][None], cm[1][None]), (pw[0][:, :, :, None, :], pw[1][:, :, :, None, :]))
    taps = (jnp.einsum('kdghp,dgpi->kdghi', t[0], bbar[0], precision=hp)
            - jnp.einsum('kdghp,dgpi->kdghi', t[1], bbar[1], precision=hp))

    tau = (np.arange(tc)[None, :] - np.arange(ng)[:, None]) % tc
    fw = jnp.transpose(taps[:, 0], (1, 3, 0, 2))
    rv = jnp.transpose(taps[:, 1], (1, 3, 0, 2))
    mid = fw[:, :, 0] + rv[:, :, 0] + np.eye(nh, dtype=np.float32)[None] * d_skip.astype(F32)[:, None, :]
    kcat = jnp.concatenate([rv[:, :, :0:-1], mid[:, :, None], fw[:, :, 1:]], axis=2)
    kcat = kcat.reshape(ng, nh, (2 * tc - 1) * nh)
    m = jnp.stack([
        jnp.roll(jnp.stack([kcat[g, :, (tc - 1 - t) * nh:(2 * tc - 1 - t) * nh] for t in tau[g]], axis=0),
                 nh * g, axis=-1)
        for g in range(ng)], axis=0).reshape(ng, tc * nh, tc * nh)

    tauf = tau.astype(np.float32)
    bt = (jnp.transpose(bbar[0], (0, 1, 3, 2)), jnp.transpose(bbar[1], (0, 1, 3, 2)))
    parts = []
    for d, kin in ((0, tc - 1.0 - tauf), (1, tauf)):
        p_in = _cpow(er[d][:, None, None, :], ei[d][:, None, None, :], kin[:, :, None, None])
        parts.append(_cmul(p_in, (bt[0][d][:, None], bt[1][d][:, None])))
    wsum = jnp.concatenate([parts[0][0], parts[1][0], parts[0][1], parts[1][1]], axis=-1)
    wsum = wsum.reshape(ng, tc * nh, 4 * S5_STATE)

    ct = (jnp.transpose(cm[0], (0, 1, 3, 2)), jnp.transpose(cm[1], (0, 1, 3, 2)))
    parts = []
    for d, kout in ((0, tauf + 1.0), (1, tc - tauf)):
        p_out = _cpow(er[d][:, :, None, None], ei[d][:, :, None, None], kout[:, None, :, None])
        parts.append(_cmul(p_out, (ct[0][d][:, :, None, :], ct[1][d][:, :, None, :])))
    wc = jnp.concatenate([parts[0][0], parts[1][0], -parts[0][1], -parts[1][1]], axis=1)
    wc = wc.reshape(ng, 4 * S5_STATE, tc * nh)

    a16 = _cpow(er, ei, float(tc))
    a_re = jnp.concatenate([a16[0][0], a16[0][1]], axis=-1)
    a_im = jnp.concatenate([a16[1][0], a16[1][1]], axis=-1)
    return m.astype(BF16), wsum.astype(BF16), wc.astype(BF16), a_re, a_im


def _s5_in_kernel(x_ref, g1_ref, ws5_ref, wsum_ref, z_ref, sre_ref, sim_ref, s_scr):
    n = TILE_A // S5_TC
    hn = _rms(x_ref[0], g1_ref[...]).astype(BF16)
    s = jnp.dot(hn, ws5_ref[...], preferred_element_type=F32)
    s_scr[0] = s[:, :LANES]
    s_scr[1] = s[:, LANES:]
    rows = []
    for t in range(S5_TC):
        a = jnp.concatenate([s_scr[0, pl.ds(t, n, stride=S5_TC), :],
                             s_scr[1, pl.ds(t, n, stride=S5_TC), :]], axis=1)
        rows.append(pltpu.roll(a, S5_GROUP_CH * t, 1) if t else a)
    zs = _block_rotate(rows)
    for g in range(S5_GROUPS):
        zb = zs[g].astype(BF16)
        z_ref[g] = zb
        sm = jnp.dot(zb, wsum_ref[g], preferred_element_type=F32)
        sre_ref[pl.ds(g, n, stride=S5_GROUPS), :] = sm[:, :LANES]
        sim_ref[pl.ds(g, n, stride=S5_GROUPS), :] = sm[:, LANES:]


def _s5_in(x, g1, ws5, wsum):
    nt = SEQ // TILE_A
    n = TILE_A // S5_TC
    return pl.pallas_call(
        _s5_in_kernel,
        grid=(BATCH, nt),
        in_specs=[
            pl.BlockSpec((1, TILE_A, D_MODEL), lambda b, i: (b, i, 0)),
            pl.BlockSpec((1, D_MODEL), lambda b, i: (0, 0)),
            pl.BlockSpec((D_MODEL, S5_WIDTH), lambda b, i: (0, O_S5 // S5_WIDTH)),
            pl.BlockSpec((S5_GROUPS, S5_WIDTH, S5_WIDTH), lambda b, i: (0, 0, 0)),
        ],
        out_specs=[
            pl.BlockSpec((S5_GROUPS, n, S5_WIDTH), lambda b, i: (0, b * nt + i, 0)),
            pl.BlockSpec((n * S5_GROUPS, LANES), lambda b, i: (i, b)),
            pl.BlockSpec((n * S5_GROUPS, LANES), lambda b, i: (i, b)),
        ],
        out_shape=[
            jax.ShapeDtypeStruct((S5_GROUPS, BATCH * S5_NCH, S5_WIDTH), BF16),
            jax.ShapeDtypeStruct((S5_NCH * S5_GROUPS, BATCH * LANES), F32),
            jax.ShapeDtypeStruct((S5_NCH * S5_GROUPS, BATCH * LANES), F32),
        ],
        scratch_shapes=[pltpu.VMEM((2, TILE_A, LANES), F32)],
        compiler_params=pltpu.CompilerParams(
            dimension_semantics=("parallel", "parallel"), vmem_limit_bytes=VMEM_LIMIT),
        name="s5_in",
    )(x, g1, ws5, wsum)


def _s5_scan_kernel(sre_ref, sim_ref, are_ref, aim_ref, xre_ref, xim_ref):
    ar = are_ref[...]
    ai = aim_ref[...]
    fwd = (lax.broadcasted_iota(I32, ar.shape, 1) % LANES) < S5_STATE
    zero = jnp.zeros_like(ar)

    def step(c, xr, xi):
        sr = sre_ref[c]
        si = sim_ref[c]
        return ar * xr - ai * xi + sr, ar * xi + ai * xr + si

    def fwd_body(c, carry):
        xr, xi = carry
        xre_ref[c] = xr
        xim_ref[c] = xi
        return step(c, xr, xi)

    lax.fori_loop(0, S5_NCH, fwd_body, (zero, zero))

    def rev_body(i, carry):
        xr, xi = carry
        c = S5_NCH - 1 - i
        xre_ref[c] = jnp.where(fwd, xre_ref[c], xr)
        xim_ref[c] = jnp.where(fwd, xim_ref[c], xi)
        return step(c, xr, xi)

    lax.fori_loop(0, S5_NCH, rev_body, (zero, zero))


def _s5_scan(s_re, s_im, a_re, a_im):
    nl = BATCH * LANES
    blk3 = pl.BlockSpec((S5_NCH, S5_GROUPS, SCAN_LANES), lambda j: (0, 0, j))
    blk2 = pl.BlockSpec((S5_GROUPS, SCAN_LANES), lambda j: (0, j))
    return pl.pallas_call(
        _s5_scan_kernel,
        grid=(nl // SCAN_LANES,),
        in_specs=[blk3, blk3, blk2, blk2],
        out_specs=[blk3, blk3],
        out_shape=[jax.ShapeDtypeStruct((S5_NCH, S5_GROUPS, nl), F32)] * 2,
        compiler_params=pltpu.CompilerParams(
            dimension_semantics=("parallel",), vmem_limit_bytes=VMEM_LIMIT),
        name="s5_scan",
    )(s_re, s_im, a_re, a_im)


def _s5_out_kernel(z_ref, xre_ref, xim_ref, m_ref, wc_ref, y_ref):
    n = S5_NCH
    ys = []
    for g in range(S5_GROUPS):
        xc = jnp.concatenate([xre_ref[pl.ds(g, n, stride=S5_GROUPS), :],
                              xim_ref[pl.ds(g, n, stride=S5_GROUPS), :]], axis=1).astype(BF16)
        ys.append(jnp.dot(z_ref[g], m_ref[g], preferred_element_type=F32)
                  + jnp.dot(xc, wc_ref[g], preferred_element_type=F32))
    ts = _block_rotate(ys)
    for t in range(S5_TC):
        acc = ts[t]
        if t:
            acc = pltpu.roll(acc, S5_WIDTH - S5_GROUP_CH * t, 1)
        y_ref[0, 0, pl.ds(t, n, stride=S5_TC), :] = acc[:, :LANES]
        y_ref[0, 1, pl.ds(t, n, stride=S5_TC), :] = acc[:, LANES:]


def _s5_out(z, xc_re, xc_im, m, wc):
    wspec = pl.BlockSpec((S5_GROUPS, S5_WIDTH, S5_WIDTH), lambda b: (0, 0, 0))
    xspec = pl.BlockSpec((S5_NCH * S5_GROUPS, LANES), lambda b: (0, b))
    return pl.pallas_call(
        _s5_out_kernel,
        grid=(BATCH,),
        in_specs=[pl.BlockSpec((S5_GROUPS, S5_NCH, S5_WIDTH), lambda b: (0, b, 0)), xspec, xspec, wspec, wspec],
        out_specs=pl.BlockSpec((1, 2, SEQ, LANES), lambda b: (b, 0, 0, 0)),
        out_shape=jax.ShapeDtypeStruct((BATCH, 2, SEQ, LANES), F32),
        compiler_params=pltpu.CompilerParams(
            dimension_semantics=("parallel",), vmem_limit_bytes=VMEM_LIMIT),
        name="s5_out",
    )(z, xc_re, xc_im, m, wc)


def _mixer_kernel(x_ref, y_ref, g1_ref, win_ref, bga_ref, bgb_ref, lng_ref, lnb_ref,
                  ws_ref, bs_ref, wglu_ref, bglu_ref, wupa_ref, wupb_ref, wout_ref, g2_ref, wr_ref,
                  ewg_ref, ewu_ref, ewd_ref,
                  x1_ref, h2p_ref, pr_ref, owg_ref, owu_ref, owd_ref, bra_scr, mrg_scr):
    for src, dst in ((ewg_ref, owg_ref), (ewu_ref, owu_ref), (ewd_ref, owd_ref)):
        dst[...] = src[...].astype(BF16)

    x = x_ref[0]
    hn = _rms(x, g1_ref[...]).astype(BF16)

    def in_proj(c0):
        return jnp.dot(hn, win_ref[:, c0:c0 + MIX_COLS], preferred_element_type=F32)

    nuv = GMLP_WIDTH // MIX_COLS
    us = [_gelu(in_proj(c * MIX_COLS)) for c in range(nuv)]
    vs = [_gelu(in_proj(GMLP_WIDTH + c * MIX_COLS)) for c in range(nuv)]
    mu = sum(jnp.sum(v, axis=-1, keepdims=True) for v in vs) * (1.0 / GMLP_WIDTH)
    vcs = [v - mu for v in vs]
    var = sum(jnp.sum(vc * vc, axis=-1, keepdims=True) for vc in vcs) * (1.0 / GMLP_WIDTH)
    rstd = lax.rsqrt(var + EPS)
    vns = [(vc * rstd * lng_ref[:, c * MIX_COLS:(c + 1) * MIX_COLS]
            + lnb_ref[:, c * MIX_COLS:(c + 1) * MIX_COLS]).astype(BF16) for c, vc in enumerate(vcs)]
    low = lax.broadcasted_iota(I32, (CHUNK, LANES), 1) < GMLP_HEAD
    per = MIX_COLS // LANES
    for ch in range(TILE_M // CHUNK):
        r0 = ch * CHUNK
        for q in range(GMLP_WIDTH // LANES):
            c0 = q * LANES
            l0 = (q % per) * LANES
            vb = vns[q // per][r0:r0 + CHUNK, l0:l0 + LANES]
            z0 = jnp.dot(ws_ref[2 * q], vb, preferred_element_type=F32)
            z1 = jnp.dot(ws_ref[2 * q + 1], vb, preferred_element_type=F32)
            zz = jnp.where(low, z0, z1) + bs_ref[:, c0:c0 + LANES]
            bra_scr[r0:r0 + CHUNK, c0:c0 + LANES] = (us[q // per][r0:r0 + CHUNK, l0:l0 + LANES] * zz).astype(BF16)

    yb = _gelu(jnp.concatenate([y_ref[0, 0], y_ref[0, 1]], axis=1))
    gl = jnp.dot(yb.astype(BF16), wglu_ref[...], preferred_element_type=F32) + bglu_ref[...]
    brb = (yb * jax.nn.sigmoid(gl)).astype(BF16)

    bra = bra_scr[...]
    for c in range(D_MODEL // MIX_COLS):
        cs = slice(c * MIX_COLS, (c + 1) * MIX_COLS)
        ga = jax.nn.sigmoid(in_proj(O_GATE + c * MIX_COLS) + bga_ref[:, cs])
        gb = jax.nn.sigmoid(in_proj(O_GATE + D_MODEL + c * MIX_COLS) + bgb_ref[:, cs])
        mrg_scr[:, cs] = (ga * jnp.dot(bra, wupa_ref[:, cs], preferred_element_type=F32)
                          + gb * jnp.dot(brb, wupb_ref[:, cs], preferred_element_type=F32)).astype(BF16)
    mrg = mrg_scr[...]
    ssq = jnp.zeros((TILE_M, 1), F32)
    for c in range(D_MODEL // MIX_COLS):
        cs = slice(c * MIX_COLS, (c + 1) * MIX_COLS)
        x1c = x[:, cs] + jnp.dot(mrg, wout_ref[:, cs], preferred_element_type=F32)
        x1_ref[0, :, cs] = x1c
        ssq = ssq + jnp.sum(x1c * x1c, axis=-1, keepdims=True)

    h2 = x1_ref[0] * lax.rsqrt(ssq * (1.0 / D_MODEL) + EPS) * g2_ref[...]
    half = D_MODEL // 2
    h2p_ref[0] = pltpu.pack_elementwise([h2[:, :half], h2[:, half:]], packed_dtype=BF16)

    lg = lax.dot_general(wr_ref[...], h2.astype(BF16), _NT, preferred_element_type=F32)
    logits = lg[:N_EXPERTS] + lg[N_EXPERTS:]
    mx = jnp.max(logits, axis=0, keepdims=True)
    ex = jnp.exp(logits - mx)
    pr_ref[0] = ex / jnp.sum(ex, axis=0, keepdims=True)


def _mixer(x, y, consts, expert_w):
    nt = SEQ // TILE_M
    nstep = BATCH * nt

    def cspec(a):
        nd = a.ndim
        return pl.BlockSpec(a.shape, lambda b, i: (0,) * nd, pipeline_mode=pl.Buffered(1))

    flat = [w.reshape(-1, w.shape[-1]) for w in expert_w]
    wspecs = [pl.BlockSpec((w.shape[0] // nstep, w.shape[1]), lambda b, i: (b * nt + i, 0)) for w in flat]
    outs = pl.pallas_call(
        _mixer_kernel,
        grid=(BATCH, nt),
        in_specs=[pl.BlockSpec((1, TILE_M, D_MODEL), lambda b, i: (b, i, 0)),
                  pl.BlockSpec((1, 2, TILE_M, LANES), lambda b, i: (b, 0, i, 0))]
                 + [cspec(a) for a in consts] + wspecs,
        out_specs=[pl.BlockSpec((1, TILE_M, D_MODEL), lambda b, i: (b, i, 0)),
                   pl.BlockSpec((1, TILE_M, D_MODEL // 2), lambda b, i: (b, i, 0)),
                   pl.BlockSpec((1, N_EXPERTS, TILE_M), lambda b, i: (b, 0, i))] + wspecs,
        out_shape=[jax.ShapeDtypeStruct((BATCH, SEQ, D_MODEL), F32),
                   jax.ShapeDtypeStruct((BATCH, SEQ, D_MODEL // 2), U32),
                   jax.ShapeDtypeStruct((BATCH, N_EXPERTS, SEQ), F32)]
                  + [jax.ShapeDtypeStruct(w.shape, BF16) for w in flat],
        scratch_shapes=[pltpu.VMEM((TILE_M, GMLP_WIDTH), BF16), pltpu.VMEM((TILE_M, D_MODEL), BF16)],
        compiler_params=pltpu.CompilerParams(
            dimension_semantics=("parallel", "parallel"), vmem_limit_bytes=VMEM_LIMIT_MIXER),
        name="mixer",
    )(x, y, *consts, *flat)
    return outs[:3], [o.reshape(w.shape) for o, w in zip(outs[3:], expert_w)]


TOPK_NBLK = SEQ // LANES
TOPK_ROWS = 4 * N_EXPERTS
TOPK_SEARCH_STEPS = 28


def _lane_sum(x):
    return jnp.sum(x, axis=1, keepdims=True)


def _topk_kernel(p_ref, tri_ref, idx_ref, gate_ref, eqc_scr, loc_scr, pj_scr, cb_scr, cbp_scr):
    p = p_ref[...]

    def count_ge(th):
        return _lane_sum(jnp.where(p >= th, 1.0, 0.0))

    def search(_, c):
        lo, hi = c
        w = hi - lo
        m1, m2, m3 = lo + 0.25 * w, lo + 0.5 * w, lo + 0.75 * w
        c1, c2, c3 = count_ge(m1) >= CAPACITY, count_ge(m2) >= CAPACITY, count_ge(m3) >= CAPACITY
        return (jnp.where(c3, m3, jnp.where(c2, m2, jnp.where(c1, m1, lo))),
                jnp.where(c1, jnp.where(c2, jnp.where(c3, hi, m3), m2), m1))

    lo, hi = lax.fori_loop(0, TOPK_SEARCH_STEPS, search,
                           (jnp.zeros((TOPK_ROWS, 1), F32), jnp.full((TOPK_ROWS, 1), 2.0, F32)))
    gt = p >= hi
    eq = (p >= lo) & (p < hi)
    need = CAPACITY - _lane_sum(jnp.where(gt, 1.0, 0.0))
    tri = tri_ref[...]
    eqf = jnp.where(eq, 1.0, 0.0)
    carry = jnp.zeros((TOPK_ROWS, 1), F32)
    for j in range(TOPK_NBLK):
        sl = slice(j * LANES, (j + 1) * LANES)
        cs = jnp.dot(eqf[:, sl].astype(BF16), tri, preferred_element_type=F32) + carry
        eqc_scr[:, sl] = cs
        carry = cs[:, LANES - 1:LANES]
    sel = gt | (eq & ((eqc_scr[...] - eqf) < need))
    self = jnp.where(sel, 1.0, 0.0)

    lane = lax.broadcasted_iota(I32, (TOPK_ROWS, LANES), 1)
    cb = jnp.full((TOPK_ROWS, LANES), float(2 * SEQ), F32)
    cbp = cb
    carry = jnp.zeros((TOPK_ROWS, 1), F32)
    for j in range(TOPK_NBLK):
        sl = slice(j * LANES, (j + 1) * LANES)
        loc = jnp.dot(self[:, sl].astype(BF16), tri, preferred_element_type=F32)
        loc_scr[j * TOPK_ROWS:(j + 1) * TOPK_ROWS, :] = loc
        pj_scr[j * TOPK_ROWS:(j + 1) * TOPK_ROWS, :] = p[:, sl]
        cbp = jnp.where(lane == j, carry, cbp)
        carry = carry + loc[:, LANES - 1:LANES]
        cb = jnp.where(lane == j, carry, cb)
    padr = jnp.zeros((LANES - TOPK_ROWS, LANES), F32)
    cb_scr[...] = jnp.concatenate([cb, padr], axis=0).T
    cbp_scr[...] = jnp.concatenate([cbp, padr], axis=0).T

    slot = (lax.broadcasted_iota(I32, (1, CAPACITY), 1) + 1).astype(F32)
    lane_r = lax.broadcasted_iota(I32, (LANES, LANES), 1)
    sub_f = lax.broadcasted_iota(I32, (LANES, 1), 0).astype(F32)
    pad = jnp.zeros((LANES - TOPK_NBLK, LANES), F32)

    def block_rows_t(ref, e):
        return jnp.concatenate([ref[pl.ds(e, TOPK_NBLK, stride=TOPK_ROWS), :], pad], axis=0).T

    def sub_sum(x):
        return jnp.sum(x, axis=0, keepdims=True)

    def per_expert(e, _):
        cb_e = _lane_sum(jnp.where(lane_r == e, cb_scr[...], 0.0))
        cbp_e = _lane_sum(jnp.where(lane_r == e, cbp_scr[...], 0.0))
        hit = (cbp_e < slot) & (slot <= cb_e)
        hitb = jnp.where(hit, 1.0, 0.0).astype(BF16)
        both = sub_sum(jnp.where(hit, cbp_e * float(LANES) + sub_f, 0.0))
        before = jnp.floor(both * (1.0 / LANES))
        jblk = both - before * float(LANES)
        rank = slot - before
        loc = jnp.dot(block_rows_t(loc_scr, e).astype(BF16), hitb, preferred_element_type=F32)
        off = sub_sum(jnp.where(loc < rank, 1.0, 0.0))
        pe = block_rows_t(pj_scr, e)
        hi = pe.astype(BF16)
        r1 = pe - hi.astype(F32)
        mid = r1.astype(BF16)
        lo = (r1 - mid.astype(F32)).astype(BF16)
        pg = (jnp.dot(hi, hitb, preferred_element_type=F32) + jnp.dot(mid, hitb, preferred_element_type=F32)
              + jnp.dot(lo, hitb, preferred_element_type=F32))
        idx_ref[pl.ds(e, 1), :] = (jblk * float(LANES) + off).astype(I32)
        gate_ref[pl.ds(e, 1), :] = sub_sum(jnp.where(sub_f == off, pg, 0.0))
        return 0

    lax.fori_loop(0, TOPK_ROWS, per_expert, 0, unroll=4)


def _topk(probs, tri):
    nrow = BATCH * N_EXPERTS
    return pl.pallas_call(
        _topk_kernel,
        grid=(nrow // TOPK_ROWS,),
        in_specs=[pl.BlockSpec((TOPK_ROWS, SEQ), lambda i: (i, 0)),
                  pl.BlockSpec((LANES, LANES), lambda i: (0, 0))],
        out_specs=[pl.BlockSpec((TOPK_ROWS, CAPACITY), lambda i: (i, 0)),
                   pl.BlockSpec((TOPK_ROWS, CAPACITY), lambda i: (i, 0))],
        out_shape=[jax.ShapeDtypeStruct((nrow, CAPACITY), I32),
                   jax.ShapeDtypeStruct((nrow, CAPACITY), F32)],
        scratch_shapes=[pltpu.VMEM((TOPK_ROWS, SEQ), F32),
                        pltpu.VMEM((TOPK_NBLK * TOPK_ROWS, LANES), F32),
                        pltpu.VMEM((TOPK_NBLK * TOPK_ROWS, LANES), F32),
                        pltpu.VMEM((LANES, LANES), F32),
                        pltpu.VMEM((LANES, LANES), F32)],
        compiler_params=pltpu.CompilerParams(
            dimension_semantics=("parallel",), vmem_limit_bytes=VMEM_LIMIT),
        name="topk",
    )(probs, tri)


MOE_GROUP = 8
MOE_FCHUNK = 256
MOE_NBUF = 1 if TILE_F == EXPERT_FF else 2
MOE_ANCHOR_LAG = 3


def _gather_rows(idx_ref, ibase, h2p_ref, dst_ref, dbase, nrows):
    for k in range(nrows):
        r = idx_ref[0, 0, ibase + k]
        dst_ref[pl.ds(dbase + k, 1), :] = h2p_ref[0, pl.ds(r, 1), :]


def _scatter_rows(idx_ref, gate_ref, ibase, y_ref, ybase, out_ref, nrows):
    for k0 in range(0, nrows, MOE_GROUP):
        rs, vals = [], []
        for k in range(k0, k0 + MOE_GROUP):
            r = idx_ref[0, 0, ibase + k]
            g = gate_ref[0, 0, ibase + k]
            rs.append(r)
            vals.append(out_ref[0, pl.ds(r, 1), :] + g * y_ref[pl.ds(ybase + k, 1), :])
        for r, v in zip(rs, vals):
            out_ref[0, pl.ds(r, 1), :] = v
    return rs[-1]


def _moe_kernel(idx_ref, gate_ref, h2p_ref, x1_ref, gf_ref, wg_ref, wu_ref, wd_ref, out_ref, xg_scr, xs_scr, y_scr):
    e = pl.program_id(1)
    f = pl.program_id(2)
    nf = EXPERT_FF // TILE_F
    nx = SEQ // TILE_X
    step = e * nf + f
    rq = CAPACITY // nf
    cur = e % MOE_NBUF
    nxt = (e + 1) % MOE_NBUF
    xg_cur, xg_nxt = xg_scr.at[cur], xg_scr.at[nxt]
    y_cur, y_nxt = y_scr.at[cur], y_scr.at[nxt]

    @pl.when(step == 0)
    def _():
        out_ref[...] = jnp.zeros_like(out_ref)
        y_scr[...] = jnp.zeros_like(y_scr)

        def gather(i, _):
            base = pl.multiple_of(i * 32, 32)
            _gather_rows(idx_ref, base, h2p_ref, xg_cur, base, 32)
            return 0

        lax.fori_loop(0, CAPACITY // 32, gather, 0)

    w = xg_cur[...]
    half = D_MODEL // 2
    for i in range(2):
        xs_scr[:, i * half:(i + 1) * half] = pltpu.unpack_elementwise(
            w, index=i, packed_dtype=BF16, unpacked_dtype=F32).astype(BF16)
    xrows = pl.ds(pl.multiple_of(jnp.minimum(step, nx - 1) * TILE_X, TILE_X), TILE_X)
    out_ref[0, xrows, :] = out_ref[0, xrows, :] + jnp.where(step < nx, x1_ref[0], 0.0)
    qbase = pl.multiple_of(f * rq, rq)
    e_prev = (e + N_EXPERTS - 1) % N_EXPERTS
    e_next = (e + 1) % N_EXPERTS
    nchunk = TILE_F // MOE_FCHUNK
    rc = rq // nchunk

    xs = xs_scr[...]
    y = jnp.where(f > 0, y_cur[...], 0.0) if nf > 1 else None

    def gate_up(c0):
        cs = slice(c0, c0 + MOE_FCHUNK)
        return (jnp.dot(xs, wg_ref[0, :, cs], preferred_element_type=F32),
                jnp.dot(xs, wu_ref[0, :, cs], preferred_element_type=F32))

    a, u = gate_up(0)
    last_rows = []
    for c in range(nchunk):
        c0 = c * MOE_FCHUNK
        rbase = qbase + c * rc
        last_rows.append(_scatter_rows(idx_ref, gate_ref, e_prev * CAPACITY + rbase, y_nxt, rbase, out_ref, rc))
        _gather_rows(idx_ref, e_next * CAPACITY + rbase, h2p_ref, xg_nxt, rbase, rc)
        nxt_au = gate_up(c0 + MOE_FCHUNK) if c + 1 < nchunk else None
        hm = a * jax.nn.sigmoid(a) * u
        if c >= MOE_ANCHOR_LAG:
            hm = hm + out_ref[0, pl.ds(last_rows[c - MOE_ANCHOR_LAG], 1), 0:MOE_FCHUNK] * 0.0
        hm = hm.astype(BF16)
        yc = jnp.dot(hm, wd_ref[0, c0:c0 + MOE_FCHUNK, :], preferred_element_type=F32)
        y = yc if y is None else y + yc
        if nxt_au is not None:
            a, u = nxt_au
    y_cur[...] = y

    @pl.when((e == N_EXPERTS - 1) & (f == nf - 1))
    def _():
        def scatter(i, _):
            base = pl.multiple_of(i * 32, 32)
            _scatter_rows(idx_ref, gate_ref, e * CAPACITY + base, y_cur, base, out_ref, 32)
            return 0

        lax.fori_loop(0, CAPACITY // 32, scatter, 0)

        def norm(i, _):
            rows = pl.ds(pl.multiple_of(i * TILE_X, TILE_X), TILE_X)
            out_ref[0, rows, :] = _rms(out_ref[0, rows, :], gf_ref[...])
            return 0

        lax.fori_loop(0, SEQ // TILE_X, norm, 0)


def _moe(idx, gate, h2p, x1, gf, wg, wu, wd):
    nf = EXPERT_FF // TILE_F
    nx = SEQ // TILE_X
    assert nx <= N_EXPERTS * nf
    sspec = pl.BlockSpec((1, 1, N_EXPERTS * CAPACITY), lambda b, e, f: (b, 0, 0), memory_space=pltpu.SMEM)
    return pl.pallas_call(
        _moe_kernel,
        grid=(BATCH, N_EXPERTS, nf),
        in_specs=[sspec, sspec,
                  pl.BlockSpec((1, SEQ, D_MODEL // 2), lambda b, e, f: (b, 0, 0), pipeline_mode=pl.Buffered(1)),
                  pl.BlockSpec((1, TILE_X, D_MODEL), lambda b, e, f: (b, jnp.minimum(e * nf + f, nx - 1), 0)),
                  pl.BlockSpec((1, D_MODEL), lambda b, e, f: (0, 0)),
                  pl.BlockSpec((1, D_MODEL, TILE_F), lambda b, e, f: (e, 0, f)),
                  pl.BlockSpec((1, D_MODEL, TILE_F), lambda b, e, f: (e, 0, f)),
                  pl.BlockSpec((1, TILE_F, D_MODEL), lambda b, e, f: (e, f, 0))],
        out_specs=pl.BlockSpec((1, SEQ, D_MODEL), lambda b, e, f: (b, 0, 0), pipeline_mode=pl.Buffered(1)),
        out_shape=jax.ShapeDtypeStruct((BATCH, SEQ, D_MODEL), F32),
        scratch_shapes=[pltpu.VMEM((MOE_NBUF, CAPACITY, D_MODEL // 2), U32),
                        pltpu.VMEM((CAPACITY, D_MODEL), BF16),
                        pltpu.VMEM((MOE_NBUF, CAPACITY, D_MODEL), F32)],
        compiler_params=pltpu.CompilerParams(
            dimension_semantics=("parallel", "arbitrary", "arbitrary"), vmem_limit_bytes=VMEM_LIMIT_MOE),
        name="moe",
    )(idx.reshape(BATCH, 1, N_EXPERTS * CAPACITY), gate.reshape(BATCH, 1, N_EXPERTS * CAPACITY), h2p, x1, gf,
      wg, wu, wd)


def kernel(x, norm1_g, w_in, b_gate, gmlp_ln_g, gmlp_ln_b, gmlp_w_s, gmlp_b_s, s5_lam_re, s5_lam_im, s5_log_dt,
           s5_b_re, s5_b_im, s5_c_re, s5_c_im, s5_d, s5_w_glu, s5_b_glu, w_up_a, w_up_b, w_out, norm2_g,
           w_router, w_gate, w_up, w_down, final_g):
    assert x.shape == (BATCH, SEQ, D_MODEL) and w_in.shape[0] == 1
    l = 0
    row = lambda a: a.astype(F32).reshape(1, -1)

    w_in_b = w_in[l].astype(BF16)
    m_mat, wsum, wc, a_re, a_im = _s5_tables(s5_lam_re[l], s5_lam_im[l], s5_log_dt[l], s5_b_re[l], s5_b_im[l],
                                             s5_c_re[l], s5_c_im[l], s5_d[l])
    g1 = row(norm1_g[l])

    z, s_re, s_im = _s5_in(x, g1, w_in_b, wsum)
    shape3 = (S5_NCH, S5_GROUPS, BATCH * LANES)
    xc_re, xc_im = _s5_scan(s_re.reshape(shape3), s_im.reshape(shape3),
                            jnp.tile(a_re, (1, BATCH)), jnp.tile(a_im, (1, BATCH)))
    shape2 = (S5_NCH * S5_GROUPS, BATCH * LANES)
    y = _s5_out(z, xc_re.reshape(shape2), xc_im.reshape(shape2), m_mat, wc)

    wr_t = jnp.transpose(w_router[l].astype(F32))
    wr_hi = wr_t.astype(BF16)
    wr_lo = (wr_t - wr_hi.astype(F32)).astype(BF16)
    consts = (
        g1, w_in_b, row(b_gate[l][:D_MODEL]), row(b_gate[l][D_MODEL:]), row(gmlp_ln_g[l]), row(gmlp_ln_b[l]),
        gmlp_w_s[l].astype(BF16), jnp.repeat(jnp.transpose(gmlp_b_s[l].astype(F32)), GMLP_HEAD, axis=1),
        s5_w_glu[l].astype(BF16), row(s5_b_glu[l]), w_up_a[l].astype(BF16), w_up_b[l].astype(BF16),
        w_out[l].astype(BF16), row(norm2_g[l]), jnp.concatenate([wr_hi, wr_lo], axis=0),
    )
    (x1, h2p, probs), (wg_b, wu_b, wd_b) = _mixer(x, y, consts, (w_gate[l], w_up[l], w_down[l]))

    tri = (jnp.arange(LANES)[:, None] <= jnp.arange(LANES)[None, :]).astype(BF16)
    idx, gate = _topk(probs.reshape(BATCH * N_EXPERTS, SEQ), tri)
    return _moe(idx, gate, h2p, x1, row(final_g), wg_b, wu_b, wd_b)
```

```python
import jax
import jax.numpy as jnp
import numpy as np
from jax import lax
from jax.experimental import pallas as pl
from jax.experimental.pallas import tpu as pltpu

F32 = jnp.float32
BF16 = jnp.bfloat16
I32 = jnp.int32
U32 = jnp.uint32

D_MODEL = 1024
BATCH = 16
SEQ = 4096
GMLP_WIDTH = 512
GMLP_GROUPS = 8
GMLP_HEAD = GMLP_WIDTH // GMLP_GROUPS
CHUNK = 128
S5_WIDTH = 256
S5_GROUP_CH = 16
S5_GROUPS = S5_WIDTH // S5_GROUP_CH
S5_STATE = 64
N_EXPERTS = 16
CAPACITY = 2 * SEQ // N_EXPERTS
EXPERT_FF = 2048
EPS = 1e-6
O_S5 = 2 * GMLP_WIDTH
O_GATE = O_S5 + S5_WIDTH

LANES = 128
S5_TC = 16
S5_NCH = SEQ // S5_TC
TILE_A = 2048
TILE_M = 1024
MIX_COLS = 256
TILE_F = 2048
TILE_X = 256
SCAN_LANES = 256
VMEM_LIMIT = 56 * 1024 * 1024
VMEM_LIMIT_MIXER = 60 * 1024 * 1024
VMEM_LIMIT_MOE = 62 * 1024 * 1024

_NT = (((1,), (1,)), ((), ()))


def _gelu(x):
    return 0.5 * x * (1.0 + lax.erf(x * (2.0 ** -0.5)))


def _rms(x, g):
    return x * lax.rsqrt(jnp.mean(x * x, axis=-1, keepdims=True) + EPS) * g


def _block_rotate(arrs):
    n = len(arrs)
    blk = lax.broadcasted_iota(I32, arrs[0].shape, 1) // S5_GROUP_CH
    w = [arrs[(-i) % n] for i in range(n)]
    for s in range(n.bit_length() - 1):
        bit = ((blk >> s) & 1) == 1
        w = [jnp.where(bit, w[(i - (1 << s)) % n], w[i]) for i in range(n)]
    return w


def _cpow(er, ei, k):
    mag = jnp.exp(k * er)
    return mag * jnp.cos(k * ei), mag * jnp.sin(k * ei)


def _cmul(a, b):
    return a[0] * b[0] - a[1] * b[1], a[0] * b[1] + a[1] * b[0]


def _s5_tables(lam_re, lam_im, log_dt, b_re, b_im, c_re, c_im, d_skip):
    hp = lax.Precision.HIGH
    tc, ng, nh = S5_TC, S5_GROUPS, S5_GROUP_CH
    lr, li = lam_re.astype(F32), lam_im.astype(F32)
    dt = jnp.exp(log_dt.astype(F32))[..., None]
    er, ei = lr * dt, li * dt
    lb = _cpow(er, ei, 1.0)
    den = lr * lr + li * li
    q = (((lb[0] - 1.0) * lr + lb[1] * li) / den, (lb[1] * lr - (lb[0] - 1.0) * li) / den)
    bbar = _cmul((q[0][..., None], q[1][..., None]), (b_re.astype(F32), b_im.astype(F32)))
    cm = (c_re.astype(F32), c_im.astype(F32))

    kk = np.arange(tc, dtype=np.float32)[:, None, None, None]
    pw = _cpow(er[None], ei[None], kk)
    t = _cmul((cm[0][None], cm[1][None]), (pw[0][:, :, :, None, :], pw[1][:, :, :, None, :]))
    taps = (jnp.einsum('kdghp,dgpi->kdghi', t[0], bbar[0], precision=hp)
            - jnp.einsum('kdghp,dgpi->kdghi', t[1], bbar[1], precision=hp))

    tau = (np.arange(tc)[None, :] - np.arange(ng)[:, None]) % tc
    fw = jnp.transpose(taps[:, 0], (1, 3, 0, 2))
    rv = jnp.transpose(taps[:, 1], (1, 3, 0, 2))
    mid = fw[:, :, 0] + rv[:, :, 0] + np.eye(nh, dtype=np.float32)[None] * d_skip.astype(F32)[:, None, :]
    kcat = jnp.concatenate([rv[:, :, :0:-1], mid[:, :, None], fw[:, :, 1:]], axis=2)
    kcat = kcat.reshape(ng, nh, (2 * tc - 1) * nh)
    m = jnp.stack([
        jnp.roll(jnp.stack([kcat[g, :, (tc - 1 - t) * nh:(2 * tc - 1 - t) * nh] for t in tau[g]], axis=0),
                 nh * g, axis=-1)
        for g in range(ng)], axis=0).reshape(ng, tc * nh, tc * nh)

    tauf = tau.astype(np.float32)
    bt = (jnp.transpose(bbar[0], (0, 1, 3, 2)), jnp.transpose(bbar[1], (0, 1, 3, 2)))
    parts = []
    for d, kin in ((0, tc - 1.0 - tauf), (1, tauf)):
        p_in = _cpow(er[d][:, None, None, :], ei[d][:, None, None, :], kin[:, :, None, None])
        parts.append(_cmul(p_in, (bt[0][d][:, None], bt[1][d][:, None])))
    wsum = jnp.concatenate([parts[0][0], parts[1][0], parts[0][1], parts[1][1]], axis=-1)
    wsum = wsum.reshape(ng, tc * nh, 4 * S5_STATE)

    ct = (jnp.transpose(cm[0], (0, 1, 3, 2)), jnp.transpose(cm[1], (0, 1, 3, 2)))
    parts = []
    for d, kout in ((0, tauf + 1.0), (1, tc - tauf)):
        p_out = _cpow(er[d][:, :, None, None], ei[d][:, :, None, None], kout[:, None, :, None])
        parts.append(_cmul(p_out, (ct[0][d][:, :, None, :], ct[1][d][:, :, None, :])))
    wc = jnp.concatenate([parts[0][0], parts[1][0], -parts[0][1], -parts[1][1]], axis=1)
    wc = wc.reshape(ng, 4 * S5_STATE, tc * nh)

    a16 = _cpow(er, ei, float(tc))
    a_re = jnp.concatenate([a16[0][0], a16[0][1]], axis=-1)
    a_im = jnp.concatenate([a16[1][0], a16[1][1]], axis=-1)
    return m.astype(BF16), wsum.astype(BF16), wc.astype(BF16), a_re, a_im


def _s5_in_kernel(x_ref, g1_ref, ws5_ref, wsum_ref, z_ref, sre_ref, sim_ref, s_scr):
    n = TILE_A // S5_TC
    hn = _rms(x_ref[0], g1_ref[...]).astype(BF16)
    s = jnp.dot(hn, ws5_ref[...], preferred_element_type=F32)
    s_scr[0] = s[:, :LANES]
    s_scr[1] = s[:, LANES:]
    rows = []
    for t in range(S5_TC):
        a = jnp.concatenate([s_scr[0, pl.ds(t, n, stride=S5_TC), :],
                             s_scr[1, pl.ds(t, n, stride=S5_TC), :]], axis=1)
        rows.append(pltpu.roll(a, S5_GROUP_CH * t, 1) if t else a)
    zs = _block_rotate(rows)
    for g in range(S5_GROUPS):
        zb = zs[g].astype(BF16)
        z_ref[g] = zb
        sm = jnp.dot(zb, wsum_ref[g], preferred_element_type=F32)
        sre_ref[pl.ds(g, n, stride=S5_GROUPS), :] = sm[:, :LANES]
        sim_ref[pl.ds(g, n, stride=S5_GROUPS), :] = sm[:, LANES:]


def _s5_in(x, g1, ws5, wsum):
    nt = SEQ // TILE_A
    n = TILE_A // S5_TC
    return pl.pallas_call(
        _s5_in_kernel,
        grid=(BATCH, nt),
        in_specs=[
            pl.BlockSpec((1, TILE_A, D_MODEL), lambda b, i: (b, i, 0)),
            pl.BlockSpec((1, D_MODEL), lambda b, i: (0, 0)),
            pl.BlockSpec((D_MODEL, S5_WIDTH), lambda b, i: (0, O_S5 // S5_WIDTH)),
            pl.BlockSpec((S5_GROUPS, S5_WIDTH, S5_WIDTH), lambda b, i: (0, 0, 0)),
        ],
        out_specs=[
            pl.BlockSpec((S5_GROUPS, n, S5_WIDTH), lambda b, i: (0, b * nt + i, 0)),
            pl.BlockSpec((n * S5_GROUPS, LANES), lambda b, i: (i, b)),
            pl.BlockSpec((n * S5_GROUPS, LANES), lambda b, i: (i, b)),
        ],
        out_shape=[
            jax.ShapeDtypeStruct((S5_GROUPS, BATCH * S5_NCH, S5_WIDTH), BF16),
            jax.ShapeDtypeStruct((S5_NCH * S5_GROUPS, BATCH * LANES), F32),
            jax.ShapeDtypeStruct((S5_NCH * S5_GROUPS, BATCH * LANES), F32),
        ],
        scratch_shapes=[pltpu.VMEM((2, TILE_A, LANES), F32)],
        compiler_params=pltpu.CompilerParams(
            dimension_semantics=("parallel", "parallel"), vmem_limit_bytes=VMEM_LIMIT),
        name="s5_in",
    )(x, g1, ws5, wsum)


def _s5_scan_kernel(sre_ref, sim_ref, are_ref, aim_ref, xre_ref, xim_ref):
    ar = are_ref[...]
    ai = aim_ref[...]
    fwd = (lax.broadcasted_iota(I32, ar.shape, 1) % LANES) < S5_STATE
    zero = jnp.zeros_like(ar)

    def step(c, xr, xi):
        sr = sre_ref[c]
        si = sim_ref[c]
        return ar * xr - ai * xi + sr, ar * xi + ai * xr + si

    def fwd_body(c, carry):
        xr, xi = carry
        xre_ref[c] = xr
        xim_ref[c] = xi
        return step(c, xr, xi)

    lax.fori_loop(0, S5_NCH, fwd_body, (zero, zero))

    def rev_body(i, carry):
        xr, xi = carry
        c = S5_NCH - 1 - i
        xre_ref[c] = jnp.where(fwd, xre_ref[c], xr)
        xim_ref[c] = jnp.where(fwd, xim_ref[c], xi)
        return step(c, xr, xi)

    lax.fori_loop(0, S5_NCH, rev_body, (zero, zero))


def _s5_scan(s_re, s_im, a_re, a_im):
    nl = BATCH * LANES
    blk3 = pl.BlockSpec((S5_NCH, S5_GROUPS, SCAN_LANES), lambda j: (0, 0, j))
    blk2 = pl.BlockSpec((S5_GROUPS, SCAN_LANES), lambda j: (0, j))
    return pl.pallas_call(
        _s5_scan_kernel,
        grid=(nl // SCAN_LANES,),
        in_specs=[blk3, blk3, blk2, blk2],
        out_specs=[blk3, blk3],
        out_shape=[jax.ShapeDtypeStruct((S5_NCH, S5_GROUPS, nl), F32)] * 2,
        compiler_params=pltpu.CompilerParams(
            dimension_semantics=("parallel",), vmem_limit_bytes=VMEM_LIMIT),
        name="s5_scan",
    )(s_re, s_im, a_re, a_im)


def _s5_out_kernel(z_ref, xre_ref, xim_ref, m_ref, wc_ref, y_ref):
    n = S5_NCH
    ys = []
    for g in range(S5_GROUPS):
        xc = jnp.concatenate([xre_ref[pl.ds(g, n, stride=S5_GROUPS), :],
                              xim_ref[pl.ds(g, n, stride=S5_GROUPS), :]], axis=1).astype(BF16)
        ys.append(jnp.dot(z_ref[g], m_ref[g], preferred_element_type=F32)
                  + jnp.dot(xc, wc_ref[g], preferred_element_type=F32))
    ts = _block_rotate(ys)
    for t in range(S5_TC):
        acc = ts[t]
        if t:
            acc = pltpu.roll(acc, S5_WIDTH - S5_GROUP_CH * t, 1)
        y_ref[0, 0, pl.ds(t, n, stride=S5_TC), :] = acc[:, :LANES]
        y_ref[0, 1, pl.ds(t, n, stride=S5_TC), :] = acc[:, LANES:]


def _s5_out(z, xc_re, xc_im, m, wc):
    wspec = pl.BlockSpec((S5_GROUPS, S5_WIDTH, S5_WIDTH), lambda b: (0, 0, 0))
    xspec = pl.BlockSpec((S5_NCH * S5_GROUPS, LANES), lambda b: (0, b))
    return pl.pallas_call(
        _s5_out_kernel,
        grid=(BATCH,),
        in_specs=[pl.BlockSpec((S5_GROUPS, S5_NCH, S5_WIDTH), lambda b: (0, b, 0)), xspec, xspec, wspec, wspec],
        out_specs=pl.BlockSpec((1, 2, SEQ, LANES), lambda b: (b, 0, 0, 0)),
        out_shape=jax.ShapeDtypeStruct((BATCH, 2, SEQ, LANES), F32),
        compiler_params=pltpu.CompilerParams(
            dimension_semantics=("parallel",), vmem_limit_bytes=VMEM_LIMIT),
        name="s5_out",
    )(z, xc_re, xc_im, m, wc)


def _mixer_kernel(x_ref, y_ref, g1_ref, win_ref, bga_ref, bgb_ref, lng_ref, lnb_ref,
                  ws_ref, bs_ref, wglu_ref, bglu_ref, wupa_ref, wupb_ref, wout_ref, g2_ref, wr_ref,
                  ewg_ref, ewu_ref, ewd_ref,
                  x1_ref, h2p_ref, pr_ref, owg_ref, owu_ref, owd_ref, bra_scr, mrg_scr):
    for src, dst in ((ewg_ref, owg_ref), (ewu_ref, owu_ref), (ewd_ref, owd_ref)):
        dst[...] = src[...].astype(BF16)

    x = x_ref[0]
    hn = _rms(x, g1_ref[...]).astype(BF16)

    def in_proj(c0):
        return jnp.dot(hn, win_ref[:, c0:c0 + MIX_COLS], preferred_element_type=F32)

    nuv = GMLP_WIDTH // MIX_COLS
    us = [_gelu(in_proj(c * MIX_COLS)) for c in range(nuv)]
    vs = [_gelu(in_proj(GMLP_WIDTH + c * MIX_COLS)) for c in range(nuv)]
    mu = sum(jnp.sum(v, axis=-1, keepdims=True) for v in vs) * (1.0 / GMLP_WIDTH)
    vcs = [v - mu for v in vs]
    var = sum(jnp.sum(vc * vc, axis=-1, keepdims=True) for vc in vcs) * (1.0 / GMLP_WIDTH)
    rstd = lax.rsqrt(var + EPS)
    vns = [(vc * rstd * lng_ref[:, c * MIX_COLS:(c + 1) * MIX_COLS]
            + lnb_ref[:, c * MIX_COLS:(c + 1) * MIX_COLS]).astype(BF16) for c, vc in enumerate(vcs)]
    low = lax.broadcasted_iota(I32, (CHUNK, LANES), 1) < GMLP_HEAD
    per = MIX_COLS // LANES
    for ch in range(TILE_M // CHUNK):
        r0 = ch * CHUNK
        for q in range(GMLP_WIDTH // LANES):
            c0 = q * LANES
            l0 = (q % per) * LANES
            vb = vns[q // per][r0:r0 + CHUNK, l0:l0 + LANES]
            z0 = jnp.dot(ws_ref[2 * q], vb, preferred_element_type=F32)
            z1 = jnp.dot(ws_ref[2 * q + 1], vb, preferred_element_type=F32)
            zz = jnp.where(low, z0, z1) + bs_ref[:, c0:c0 + LANES]
            bra_scr[r0:r0 + CHUNK, c0:c0 + LANES] = (us[q // per][r0:r0 + CHUNK, l0:l0 + LANES] * zz).astype(BF16)

    yb = _gelu(jnp.concatenate([y_ref[0, 0], y_ref[0, 1]], axis=1))
    gl = jnp.dot(yb.astype(BF16), wglu_ref[...], preferred_element_type=F32) + bglu_ref[...]
    brb = (yb * jax.nn.sigmoid(gl)).astype(BF16)

    bra = bra_scr[...]
    for c in range(D_MODEL // MIX_COLS):
        cs = slice(c * MIX_COLS, (c + 1) * MIX_COLS)
        ga = jax.nn.sigmoid(in_proj(O_GATE + c * MIX_COLS) + bga_ref[:, cs])
        gb = jax.nn.sigmoid(in_proj(O_GATE + D_MODEL + c * MIX_COLS) + bgb_ref[:, cs])
        mrg_scr[:, cs] = (ga * jnp.dot(bra, wupa_ref[:, cs], preferred_element_type=F32)
                          + gb * jnp.dot(brb, wupb_ref[:, cs], preferred_element_type=F32)).astype(BF16)
    mrg = mrg_scr[...]
    ssq = jnp.zeros((TILE_M, 1), F32)
    for c in range(D_MODEL // MIX_COLS):
        cs = slice(c * MIX_COLS, (c + 1) * MIX_COLS)
        x1c = x[:, cs] + jnp.dot(mrg, wout_ref[:, cs], preferred_element_type=F32)
        x1_ref[0, :, cs] = x1c
        ssq = ssq + jnp.sum(x1c * x1c, axis=-1, keepdims=True)

    h2 = x1_ref[0] * lax.rsqrt(ssq * (1.0 / D_MODEL) + EPS) * g2_ref[...]
    half = D_MODEL // 2
    h2p_ref[0] = pltpu.pack_elementwise([h2[:, :half], h2[:, half:]], packed_dtype=BF16)

    lg = lax.dot_general(wr_ref[...], h2.astype(BF16), _NT, preferred_element_type=F32)
    logits = lg[:N_EXPERTS] + lg[N_EXPERTS:]
    mx = jnp.max(logits, axis=0, keepdims=True)
    ex = jnp.exp(logits - mx)
    pr_ref[0] = ex / jnp.sum(ex, axis=0, keepdims=True)


def _mixer(x, y, consts, expert_w):
    nt = SEQ // TILE_M
    nstep = BATCH * nt

    def cspec(a):
        nd = a.ndim
        return pl.BlockSpec(a.shape, lambda b, i: (0,) * nd, pipeline_mode=pl.Buffered(1))

    flat = [w.reshape(-1, w.shape[-1]) for w in expert_w]
    wspecs = [pl.BlockSpec((w.shape[0] // nstep, w.shape[1]), lambda b, i: (b * nt + i, 0)) for w in flat]
    outs = pl.pallas_call(
        _mixer_kernel,
        grid=(BATCH, nt),
        in_specs=[pl.BlockSpec((1, TILE_M, D_MODEL), lambda b, i: (b, i, 0)),
                  pl.BlockSpec((1, 2, TILE_M, LANES), lambda b, i: (b, 0, i, 0))]
                 + [cspec(a) for a in consts] + wspecs,
        out_specs=[pl.BlockSpec((1, TILE_M, D_MODEL), lambda b, i: (b, i, 0)),
                   pl.BlockSpec((1, TILE_M, D_MODEL // 2), lambda b, i: (b, i, 0)),
                   pl.BlockSpec((1, N_EXPERTS, TILE_M), lambda b, i: (b, 0, i))] + wspecs,
        out_shape=[jax.ShapeDtypeStruct((BATCH, SEQ, D_MODEL), F32),
                   jax.ShapeDtypeStruct((BATCH, SEQ, D_MODEL // 2), U32),
                   jax.ShapeDtypeStruct((BATCH, N_EXPERTS, SEQ), F32)]
                  + [jax.ShapeDtypeStruct(w.shape, BF16) for w in flat],
        scratch_shapes=[pltpu.VMEM((TILE_M, GMLP_WIDTH), BF16), pltpu.VMEM((TILE_M, D_MODEL), BF16)],
        compiler_params=pltpu.CompilerParams(
            dimension_semantics=("parallel", "parallel"), vmem_limit_bytes=VMEM_LIMIT_MIXER),
        name="mixer",
    )(x, y, *consts, *flat)
    return outs[:3], [o.reshape(w.shape) for o, w in zip(outs[3:], expert_w)]


TOPK_NBLK = SEQ // LANES
TOPK_ROWS = 4 * N_EXPERTS
TOPK_SEARCH_STEPS = 28


def _lane_sum(x):
    return jnp.sum(x, axis=1, keepdims=True)


def _topk_kernel(p_ref, tri_ref, idx_ref, gate_ref, eqc_scr, loc_scr, pj_scr, cb_scr, cbp_scr):
    p = p_ref[...]

    def count_ge(th):
        return _lane_sum(jnp.where(p >= th, 1.0, 0.0))

    def search(_, c):
        lo, hi = c
        w = hi - lo
        m1, m2, m3 = lo + 0.25 * w, lo + 0.5 * w, lo + 0.75 * w
        c1, c2, c3 = count_ge(m1) >= CAPACITY, count_ge(m2) >= CAPACITY, count_ge(m3) >= CAPACITY
        return (jnp.where(c3, m3, jnp.where(c2, m2, jnp.where(c1, m1, lo))),
                jnp.where(c1, jnp.where(c2, jnp.where(c3, hi, m3), m2), m1))

    lo, hi = lax.fori_loop(0, TOPK_SEARCH_STEPS, search,
                           (jnp.zeros((TOPK_ROWS, 1), F32), jnp.full((TOPK_ROWS, 1), 2.0, F32)))
    gt = p >= hi
    eq = (p >= lo) & (p < hi)
    need = CAPACITY - _lane_sum(jnp.where(gt, 1.0, 0.0))
    tri = tri_ref[...]
    eqf = jnp.where(eq, 1.0, 0.0)
    carry = jnp.zeros((TOPK_ROWS, 1), F32)
    for j in range(TOPK_NBLK):
        sl = slice(j * LANES, (j + 1) * LANES)
        cs = jnp.dot(eqf[:, sl].astype(BF16), tri, preferred_element_type=F32) + carry
        eqc_scr[:, sl] = cs
        carry = cs[:, LANES - 1:LANES]
    sel = gt | (eq & ((eqc_scr[...] - eqf) < need))
    self = jnp.where(sel, 1.0, 0.0)

    lane = lax.broadcasted_iota(I32, (TOPK_ROWS, LANES), 1)
    cb = jnp.full((TOPK_ROWS, LANES), float(2 * SEQ), F32)
    cbp = cb
    carry = jnp.zeros((TOPK_ROWS, 1), F32)
    for j in range(TOPK_NBLK):
        sl = slice(j * LANES, (j + 1) * LANES)
        loc = jnp.dot(self[:, sl].astype(BF16), tri, preferred_element_type=F32)
        loc_scr[j * TOPK_ROWS:(j + 1) * TOPK_ROWS, :] = loc
        pj_scr[j * TOPK_ROWS:(j + 1) * TOPK_ROWS, :] = p[:, sl]
        cbp = jnp.where(lane == j, carry, cbp)
        carry = carry + loc[:, LANES - 1:LANES]
        cb = jnp.where(lane == j, carry, cb)
    padr = jnp.zeros((LANES - TOPK_ROWS, LANES), F32)
    cb_scr[...] = jnp.concatenate([cb, padr], axis=0).T
    cbp_scr[...] = jnp.concatenate([cbp, padr], axis=0).T

    slot = (lax.broadcasted_iota(I32, (1, CAPACITY), 1) + 1).astype(F32)
    lane_r = lax.broadcasted_iota(I32, (LANES, LANES), 1)
    sub_f = lax.broadcasted_iota(I32, (LANES, 1), 0).astype(F32)
    pad = jnp.zeros((LANES - TOPK_NBLK, LANES), F32)

    def block_rows_t(ref, e):
        return jnp.concatenate([ref[pl.ds(e, TOPK_NBLK, stride=TOPK_ROWS), :], pad], axis=0).T

    def sub_sum(x):
        return jnp.sum(x, axis=0, keepdims=True)

    def per_expert(e, _):
        cb_e = _lane_sum(jnp.where(lane_r == e, cb_scr[...], 0.0))
        cbp_e = _lane_sum(jnp.where(lane_r == e, cbp_scr[...], 0.0))
        hit = (cbp_e < slot) & (slot <= cb_e)
        hitb = jnp.where(hit, 1.0, 0.0).astype(BF16)
        both = sub_sum(jnp.where(hit, cbp_e * float(LANES) + sub_f, 0.0))
        before = jnp.floor(both * (1.0 / LANES))
        jblk = both - before * float(LANES)
        rank = slot - before
        loc = jnp.dot(block_rows_t(loc_scr, e).astype(BF16), hitb, preferred_element_type=F32)
        off = sub_sum(jnp.where(loc < rank, 1.0, 0.0))
        pe = block_rows_t(pj_scr, e)
        hi = pe.astype(BF16)
        r1 = pe - hi.astype(F32)
        mid = r1.astype(BF16)
        lo = (r1 - mid.astype(F32)).astype(BF16)
        pg = (jnp.dot(hi, hitb, preferred_element_type=F32) + jnp.dot(mid, hitb, preferred_element_type=F32)
              + jnp.dot(lo, hitb, preferred_element_type=F32))
        idx_ref[pl.ds(e, 1), :] = (jblk * float(LANES) + off).astype(I32)
        gate_ref[pl.ds(e, 1), :] = sub_sum(jnp.where(sub_f == off, pg, 0.0))
        return 0

    lax.fori_loop(0, TOPK_ROWS, per_expert, 0, unroll=4)


def _topk(probs, tri):
    nrow = BATCH * N_EXPERTS
    return pl.pallas_call(
        _topk_kernel,
        grid=(nrow // TOPK_ROWS,),
        in_specs=[pl.BlockSpec((TOPK_ROWS, SEQ), lambda i: (i, 0)),
                  pl.BlockSpec((LANES, LANES), lambda i: (0, 0))],
        out_specs=[pl.BlockSpec((TOPK_ROWS, CAPACITY), lambda i: (i, 0)),
                   pl.BlockSpec((TOPK_ROWS, CAPACITY), lambda i: (i, 0))],
        out_shape=[jax.ShapeDtypeStruct((nrow, CAPACITY), I32),
                   jax.ShapeDtypeStruct((nrow, CAPACITY), F32)],
        scratch_shapes=[pltpu.VMEM((TOPK_ROWS, SEQ), F32),
                        pltpu.VMEM((TOPK_NBLK * TOPK_ROWS, LANES), F32),
                        pltpu.VMEM((TOPK_NBLK * TOPK_ROWS, LANES), F32),
                        pltpu.VMEM((LANES, LANES), F32),
                        pltpu.VMEM((LANES, LANES), F32)],
        compiler_params=pltpu.CompilerParams(
            dimension_semantics=("parallel",), vmem_limit_bytes=VMEM_LIMIT),
        name="topk",
    )(probs, tri)


MOE_GROUP = 8
MOE_FCHUNK = 256
MOE_NBUF = 1 if TILE_F == EXPERT_FF else 2


def _gather_rows(idx_ref, ibase, h2p_ref, dst_ref, dbase, nrows):
    for k in range(nrows):
        r = idx_ref[0, 0, ibase + k]
        dst_ref[pl.ds(dbase + k, 1), :] = h2p_ref[0, pl.ds(r, 1), :]


def _scatter_rows(idx_ref, gate_ref, ibase, y_ref, ybase, out_ref, nrows):
    for k0 in range(0, nrows, MOE_GROUP):
        rs, vals = [], []
        for k in range(k0, k0 + MOE_GROUP):
            r = idx_ref[0, 0, ibase + k]
            g = gate_ref[0, 0, ibase + k]
            rs.append(r)
            vals.append(out_ref[0, pl.ds(r, 1), :] + g * y_ref[pl.ds(ybase + k, 1), :])
        for r, v in zip(rs, vals):
            out_ref[0, pl.ds(r, 1), :] = v


def _moe_kernel(idx_ref, gate_ref, h2p_ref, x1_ref, gf_ref, wg_ref, wu_ref, wd_ref, out_ref, xg_scr, xs_scr, y_scr):
    e = pl.program_id(1)
    f = pl.program_id(2)
    nf = EXPERT_FF // TILE_F
    nx = SEQ // TILE_X
    step = e * nf + f
    rq = CAPACITY // nf
    cur = e % MOE_NBUF
    nxt = (e + 1) % MOE_NBUF
    xg_cur, xg_nxt = xg_scr.at[cur], xg_scr.at[nxt]
    y_cur, y_nxt = y_scr.at[cur], y_scr.at[nxt]

    @pl.when(step == 0)
    def _():
        out_ref[...] = jnp.zeros_like(out_ref)
        y_scr[...] = jnp.zeros_like(y_scr)

        def gather(i, _):
            base = pl.multiple_of(i * 32, 32)
            _gather_rows(idx_ref, base, h2p_ref, xg_cur, base, 32)
            return 0

        lax.fori_loop(0, CAPACITY // 32, gather, 0)

    w = xg_cur[...]
    half = D_MODEL // 2
    for i in range(2):
        xs_scr[:, i * half:(i + 1) * half] = pltpu.unpack_elementwise(
            w, index=i, packed_dtype=BF16, unpacked_dtype=F32).astype(BF16)
    xrows = pl.ds(pl.multiple_of(jnp.minimum(step, nx - 1) * TILE_X, TILE_X), TILE_X)
    out_ref[0, xrows, :] = out_ref[0, xrows, :] + jnp.where(step < nx, x1_ref[0], 0.0)
    qbase = pl.multiple_of(f * rq, rq)
    e_prev = (e + N_EXPERTS - 1) % N_EXPERTS
    e_next = (e + 1) % N_EXPERTS
    nchunk = TILE_F // MOE_FCHUNK
    rc = rq // nchunk

    xs = xs_scr[...]
    y = jnp.where(f > 0, y_cur[...], 0.0) if nf > 1 else None

    def gate_up(c0):
        cs = slice(c0, c0 + MOE_FCHUNK)
        return (jnp.dot(xs, wg_ref[0, :, cs], preferred_element_type=F32),
                jnp.dot(xs, wu_ref[0, :, cs], preferred_element_type=F32))

    a, u = gate_up(0)
    for c in range(nchunk):
        c0 = c * MOE_FCHUNK
        rbase = qbase + c * rc
        _scatter_rows(idx_ref, gate_ref, e_prev * CAPACITY + rbase, y_nxt, rbase, out_ref, rc)
        _gather_rows(idx_ref, e_next * CAPACITY + rbase, h2p_ref, xg_nxt, rbase, rc)
        nxt_au = gate_up(c0 + MOE_FCHUNK) if c + 1 < nchunk else None
        hm = (a * jax.nn.sigmoid(a) * u).astype(BF16)
        yc = jnp.dot(hm, wd_ref[0, c0:c0 + MOE_FCHUNK, :], preferred_element_type=F32)
        y = yc if y is None else y + yc
        if nxt_au is not None:
            a, u = nxt_au
    y_cur[...] = y

    @pl.when((e == N_EXPERTS - 1) & (f == nf - 1))
    def _():
        def scatter(i, _):
            base = pl.multiple_of(i * 32, 32)
            _scatter_rows(idx_ref, gate_ref, e * CAPACITY + base, y_cur, base, out_ref, 32)
            return 0

        lax.fori_loop(0, CAPACITY // 32, scatter, 0)

        def norm(i, _):
            rows = pl.ds(pl.multiple_of(i * TILE_X, TILE_X), TILE_X)
            out_ref[0, rows, :] = _rms(out_ref[0, rows, :], gf_ref[...])
            return 0

        lax.fori_loop(0, SEQ // TILE_X, norm, 0)


def _moe(idx, gate, h2p, x1, gf, wg, wu, wd):
    nf = EXPERT_FF // TILE_F
    nx = SEQ // TILE_X
    assert nx <= N_EXPERTS * nf
    sspec = pl.BlockSpec((1, 1, N_EXPERTS * CAPACITY), lambda b, e, f: (b, 0, 0), memory_space=pltpu.SMEM)
    return pl.pallas_call(
        _moe_kernel,
        grid=(BATCH, N_EXPERTS, nf),
        in_specs=[sspec, sspec,
                  pl.BlockSpec((1, SEQ, D_MODEL // 2), lambda b, e, f: (b, 0, 0), pipeline_mode=pl.Buffered(1)),
                  pl.BlockSpec((1, TILE_X, D_MODEL), lambda b, e, f: (b, jnp.minimum(e * nf + f, nx - 1), 0)),
                  pl.BlockSpec((1, D_MODEL), lambda b, e, f: (0, 0)),
                  pl.BlockSpec((1, D_MODEL, TILE_F), lambda b, e, f: (e, 0, f)),
                  pl.BlockSpec((1, D_MODEL, TILE_F), lambda b, e, f: (e, 0, f)),
                  pl.BlockSpec((1, TILE_F, D_MODEL), lambda b, e, f: (e, f, 0))],
        out_specs=pl.BlockSpec((1, SEQ, D_MODEL), lambda b, e, f: (b, 0, 0), pipeline_mode=pl.Buffered(1)),
        out_shape=jax.ShapeDtypeStruct((BATCH, SEQ, D_MODEL), F32),
        scratch_shapes=[pltpu.VMEM((MOE_NBUF, CAPACITY, D_MODEL // 2), U32),
                        pltpu.VMEM((CAPACITY, D_MODEL), BF16),
                        pltpu.VMEM((MOE_NBUF, CAPACITY, D_MODEL), F32)],
        compiler_params=pltpu.CompilerParams(
            dimension_semantics=("parallel", "arbitrary", "arbitrary"), vmem_limit_bytes=VMEM_LIMIT_MOE),
        name="moe",
    )(idx.reshape(BATCH, 1, N_EXPERTS * CAPACITY), gate.reshape(BATCH, 1, N_EXPERTS * CAPACITY), h2p, x1, gf,
      wg, wu, wd)


def kernel(x, norm1_g, w_in, b_gate, gmlp_ln_g, gmlp_ln_b, gmlp_w_s, gmlp_b_s, s5_lam_re, s5_lam_im, s5_log_dt,
           s5_b_re, s5_b_im, s5_c_re, s5_c_im, s5_d, s5_w_glu, s5_b_glu, w_up_a, w_up_b, w_out, norm2_g,
           w_router, w_gate, w_up, w_down, final_g):
    assert x.shape == (BATCH, SEQ, D_MODEL) and w_in.shape[0] == 1
    l = 0
    row = lambda a: a.astype(F32).reshape(1, -1)

    w_in_b = w_in[l].astype(BF16)
    m_mat, wsum, wc, a_re, a_im = _s5_tables(s5_lam_re[l], s5_lam_im[l], s5_log_dt[l], s5_b_re[l], s5_b_im[l],
                                             s5_c_re[l], s5_c_im[l], s5_d[l])
    g1 = row(norm1_g[l])

    z, s_re, s_im = _s5_in(x, g1, w_in_b, wsum)
    shape3 = (S5_NCH, S5_GROUPS, BATCH * LANES)
    xc_re, xc_im = _s5_scan(s_re.reshape(shape3), s_im.reshape(shape3),
                            jnp.tile(a_re, (1, BATCH)), jnp.tile(a_im, (1, BATCH)))
    shape2 = (S5_NCH * S5_GROUPS, BATCH * LANES)
    y = _s5_out(z, xc_re.reshape(shape2), xc_im.reshape(shape2), m_mat, wc)

    wr_t = jnp.transpose(w_router[l].astype(F32))
    wr_hi = wr_t.astype(BF16)
    wr_lo = (wr_t - wr_hi.astype(F32)).astype(BF16)
    consts = (
        g1, w_in_b, row(b_gate[l][:D_MODEL]), row(b_gate[l][D_MODEL:]), row(gmlp_ln_g[l]), row(gmlp_ln_b[l]),
        gmlp_w_s[l].astype(BF16), jnp.repeat(jnp.transpose(gmlp_b_s[l].astype(F32)), GMLP_HEAD, axis=1),
        s5_w_glu[l].astype(BF16), row(s5_b_glu[l]), w_up_a[l].astype(BF16), w_up_b[l].astype(BF16),
        w_out[l].astype(BF16), row(norm2_g[l]), jnp.concatenate([wr_hi, wr_lo], axis=0),
    )
    (x1, h2p, probs), (wg_b, wu_b, wd_b) = _mixer(x, y, consts, (w_gate[l], w_up[l], w_down[l]))

    tri = (jnp.arange(LANES)[:, None] <= jnp.arange(LANES)[None, :]).astype(BF16)
    idx, gate = _topk(probs.reshape(BATCH * N_EXPERTS, SEQ), tri)
    return _moe(idx, gate, h2p, x1, row(final_g), wg_b, wu_b, wd_b)
```

```python
import jax
import jax.numpy as jnp
import numpy as np
from jax import lax
from jax.experimental import pallas as pl
from jax.experimental.pallas import tpu as pltpu

F32 = jnp.float32
BF16 = jnp.bfloat16
I32 = jnp.int32
U32 = jnp.uint32

D_MODEL = 1024
BATCH = 16
SEQ = 4096
GMLP_WIDTH = 512
GMLP_GROUPS = 8
GMLP_HEAD = GMLP_WIDTH // GMLP_GROUPS
CHUNK = 128
S5_WIDTH = 256
S5_GROUP_CH = 16
S5_GROUPS = S5_WIDTH // S5_GROUP_CH
S5_STATE = 64
N_EXPERTS = 16
CAPACITY = 2 * SEQ // N_EXPERTS
EXPERT_FF = 2048
EPS = 1e-6
O_S5 = 2 * GMLP_WIDTH
O_GATE = O_S5 + S5_WIDTH

LANES = 128
SUBLANES = 8
S5_TC = 16
S5_NCH = SEQ // S5_TC
TILE_A = 2048
TILE_M = 1024
MIX_COLS = 256
TILE_F = 2048
TILE_X = 256
SCAN_LANES = 256
VMEM_LIMIT = 56 * 1024 * 1024
VMEM_LIMIT_MIXER = 60 * 1024 * 1024
VMEM_LIMIT_MOE = 62 * 1024 * 1024

_NT = (((1,), (1,)), ((), ()))


def _gelu(x):
    return 0.5 * x * (1.0 + lax.erf(x * (2.0 ** -0.5)))


def _rms(x, g):
    return x * lax.rsqrt(jnp.mean(x * x, axis=-1, keepdims=True) + EPS) * g


def _block_rotate(arrs):
    n = len(arrs)
    blk = lax.broadcasted_iota(I32, arrs[0].shape, 1) // S5_GROUP_CH
    w = [arrs[(-i) % n] for i in range(n)]
    for s in range(n.bit_length() - 1):
        bit = ((blk >> s) & 1) == 1
        w = [jnp.where(bit, w[(i - (1 << s)) % n], w[i]) for i in range(n)]
    return w


def _cpow(er, ei, k):
    mag = jnp.exp(k * er)
    return mag * jnp.cos(k * ei), mag * jnp.sin(k * ei)


def _cmul(a, b):
    return a[0] * b[0] - a[1] * b[1], a[0] * b[1] + a[1] * b[0]


def _s5_tables(lam_re, lam_im, log_dt, b_re, b_im, c_re, c_im, d_skip):
    hp = lax.Precision.HIGH
    tc, ng, nh = S5_TC, S5_GROUPS, S5_GROUP_CH
    lr, li = lam_re.astype(F32), lam_im.astype(F32)
    dt = jnp.exp(log_dt.astype(F32))[..., None]
    er, ei = lr * dt, li * dt
    lb = _cpow(er, ei, 1.0)
    den = lr * lr + li * li
    q = (((lb[0] - 1.0) * lr + lb[1] * li) / den, (lb[1] * lr - (lb[0] - 1.0) * li) / den)
    bbar = _cmul((q[0][..., None], q[1][..., None]), (b_re.astype(F32), b_im.astype(F32)))
    cm = (c_re.astype(F32), c_im.astype(F32))

    kk = np.arange(tc, dtype=np.float32)[:, None, None, None]
    pw = _cpow(er[None], ei[None], kk)
    t = _cmul((cm[0][None], cm[1][None]), (pw[0][:, :, :, None, :], pw[1][:, :, :, None, :]))
    taps = (jnp.einsum('kdghp,dgpi->kdghi', t[0], bbar[0], precision=hp)
            - jnp.einsum('kdghp,dgpi->kdghi', t[1], bbar[1], precision=hp))

    tau = (np.arange(tc)[None, :] - np.arange(ng)[:, None]) % tc
    fw = jnp.transpose(taps[:, 0], (1, 3, 0, 2))
    rv = jnp.transpose(taps[:, 1], (1, 3, 0, 2))
    mid = fw[:, :, 0] + rv[:, :, 0] + np.eye(nh, dtype=np.float32)[None] * d_skip.astype(F32)[:, None, :]
    kcat = jnp.concatenate([rv[:, :, :0:-1], mid[:, :, None], fw[:, :, 1:]], axis=2)
    kcat = kcat.reshape(ng, nh, (2 * tc - 1) * nh)
    m = jnp.stack([
        jnp.roll(jnp.stack([kcat[g, :, (tc - 1 - t) * nh:(2 * tc - 1 - t) * nh] for t in tau[g]], axis=0),
                 nh * g, axis=-1)
        for g in range(ng)], axis=0).reshape(ng, tc * nh, tc * nh)

    tauf = tau.astype(np.float32)
    bt = (jnp.transpose(bbar[0], (0, 1, 3, 2)), jnp.transpose(bbar[1], (0, 1, 3, 2)))
    parts = []
    for d, kin in ((0, tc - 1.0 - tauf), (1, tauf)):
        p_in = _cpow(er[d][:, None, None, :], ei[d][:, None, None, :], kin[:, :, None, None])
        parts.append(_cmul(p_in, (bt[0][d][:, None], bt[1][d][:, None])))
    wsum = jnp.concatenate([parts[0][0], parts[1][0], parts[0][1], parts[1][1]], axis=-1)
    wsum = wsum.reshape(ng, tc * nh, 4 * S5_STATE)

    ct = (jnp.transpose(cm[0], (0, 1, 3, 2)), jnp.transpose(cm[1], (0, 1, 3, 2)))
    parts = []
    for d, kout in ((0, tauf + 1.0), (1, tc - tauf)):
        p_out = _cpow(er[d][:, :, None, None], ei[d][:, :, None, None], kout[:, None, :, None])
        parts.append(_cmul(p_out, (ct[0][d][:, :, None, :], ct[1][d][:, :, None, :])))
    wc = jnp.concatenate([parts[0][0], parts[1][0], -parts[0][1], -parts[1][1]], axis=1)
    wc = wc.reshape(ng, 4 * S5_STATE, tc * nh)

    a16 = _cpow(er, ei, float(tc))
    a_re = jnp.concatenate([a16[0][0], a16[0][1]], axis=-1)
    a_im = jnp.concatenate([a16[1][0], a16[1][1]], axis=-1)
    return m.astype(BF16), wsum.astype(BF16), wc.astype(BF16), a_re, a_im


def _s5_in_kernel(x_ref, g1_ref, ws5_ref, wsum_ref, z_ref, sre_ref, sim_ref, s_scr):
    n = TILE_A // S5_TC
    hn = _rms(x_ref[0], g1_ref[...]).astype(BF16)
    s = jnp.dot(hn, ws5_ref[...], preferred_element_type=F32)
    s_scr[0] = s[:, :LANES]
    s_scr[1] = s[:, LANES:]
    rows = []
    for t in range(S5_TC):
        a = jnp.concatenate([s_scr[0, pl.ds(t, n, stride=S5_TC), :],
                             s_scr[1, pl.ds(t, n, stride=S5_TC), :]], axis=1)
        rows.append(pltpu.roll(a, S5_GROUP_CH * t, 1) if t else a)
    zs = _block_rotate(rows)
    for g in range(S5_GROUPS):
        zb = zs[g].astype(BF16)
        z_ref[g] = zb
        sm = jnp.dot(zb, wsum_ref[g], preferred_element_type=F32)
        sre_ref[pl.ds(g, n, stride=S5_GROUPS), :] = sm[:, :LANES]
        sim_ref[pl.ds(g, n, stride=S5_GROUPS), :] = sm[:, LANES:]


def _s5_in(x, g1, ws5, wsum):
    nt = SEQ // TILE_A
    n = TILE_A // S5_TC
    return pl.pallas_call(
        _s5_in_kernel,
        grid=(BATCH, nt),
        in_specs=[
            pl.BlockSpec((1, TILE_A, D_MODEL), lambda b, i: (b, i, 0)),
            pl.BlockSpec((1, D_MODEL), lambda b, i: (0, 0)),
            pl.BlockSpec((D_MODEL, S5_WIDTH), lambda b, i: (0, O_S5 // S5_WIDTH)),
            pl.BlockSpec((S5_GROUPS, S5_WIDTH, S5_WIDTH), lambda b, i: (0, 0, 0)),
        ],
        out_specs=[
            pl.BlockSpec((S5_GROUPS, n, S5_WIDTH), lambda b, i: (0, b * nt + i, 0)),
            pl.BlockSpec((n * S5_GROUPS, LANES), lambda b, i: (i, b)),
            pl.BlockSpec((n * S5_GROUPS, LANES), lambda b, i: (i, b)),
        ],
        out_shape=[
            jax.ShapeDtypeStruct((S5_GROUPS, BATCH * S5_NCH, S5_WIDTH), BF16),
            jax.ShapeDtypeStruct((S5_NCH * S5_GROUPS, BATCH * LANES), F32),
            jax.ShapeDtypeStruct((S5_NCH * S5_GROUPS, BATCH * LANES), F32),
        ],
        scratch_shapes=[pltpu.VMEM((2, TILE_A, LANES), F32)],
        compiler_params=pltpu.CompilerParams(
            dimension_semantics=("parallel", "parallel"), vmem_limit_bytes=VMEM_LIMIT),
        name="s5_in",
    )(x, g1, ws5, wsum)


def _s5_scan_kernel(sre_ref, sim_ref, are_ref, aim_ref, xre_ref, xim_ref):
    ar = are_ref[...]
    ai = aim_ref[...]
    fwd = (lax.broadcasted_iota(I32, ar.shape, 1) % LANES) < S5_STATE
    zero = jnp.zeros_like(ar)

    def step(c, xr, xi):
        sr = sre_ref[c]
        si = sim_ref[c]
        return ar * xr - ai * xi + sr, ar * xi + ai * xr + si

    def fwd_body(c, carry):
        xr, xi = carry
        xre_ref[c] = xr
        xim_ref[c] = xi
        return step(c, xr, xi)

    lax.fori_loop(0, S5_NCH, fwd_body, (zero, zero))

    def rev_body(i, carry):
        xr, xi = carry
        c = S5_NCH - 1 - i
        xre_ref[c] = jnp.where(fwd, xre_ref[c], xr)
        xim_ref[c] = jnp.where(fwd, xim_ref[c], xi)
        return step(c, xr, xi)

    lax.fori_loop(0, S5_NCH, rev_body, (zero, zero))


def _s5_scan(s_re, s_im, a_re, a_im):
    nl = BATCH * LANES
    blk3 = pl.BlockSpec((S5_NCH, S5_GROUPS, SCAN_LANES), lambda j: (0, 0, j))
    blk2 = pl.BlockSpec((S5_GROUPS, SCAN_LANES), lambda j: (0, j))
    return pl.pallas_call(
        _s5_scan_kernel,
        grid=(nl // SCAN_LANES,),
        in_specs=[blk3, blk3, blk2, blk2],
        out_specs=[blk3, blk3],
        out_shape=[jax.ShapeDtypeStruct((S5_NCH, S5_GROUPS, nl), F32)] * 2,
        compiler_params=pltpu.CompilerParams(
            dimension_semantics=("parallel",), vmem_limit_bytes=VMEM_LIMIT),
        name="s5_scan",
    )(s_re, s_im, a_re, a_im)


def _s5_out_kernel(z_ref, xre_ref, xim_ref, m_ref, wc_ref, y_ref):
    n = S5_NCH
    ys = []
    for g in range(S5_GROUPS):
        xc = jnp.concatenate([xre_ref[pl.ds(g, n, stride=S5_GROUPS), :],
                              xim_ref[pl.ds(g, n, stride=S5_GROUPS), :]], axis=1).astype(BF16)
        ys.append(jnp.dot(z_ref[g], m_ref[g], preferred_element_type=F32)
                  + jnp.dot(xc, wc_ref[g], preferred_element_type=F32))
    ts = _block_rotate(ys)
    for t in range(S5_TC):
        acc = ts[t]
        if t:
            acc = pltpu.roll(acc, S5_WIDTH - S5_GROUP_CH * t, 1)
        y_ref[0, 0, pl.ds(t, n, stride=S5_TC), :] = acc[:, :LANES]
        y_ref[0, 1, pl.ds(t, n, stride=S5_TC), :] = acc[:, LANES:]


def _s5_out(z, xc_re, xc_im, m, wc):
    wspec = pl.BlockSpec((S5_GROUPS, S5_WIDTH, S5_WIDTH), lambda b: (0, 0, 0))
    xspec = pl.BlockSpec((S5_NCH * S5_GROUPS, LANES), lambda b: (0, b))
    return pl.pallas_call(
        _s5_out_kernel,
        grid=(BATCH,),
        in_specs=[pl.BlockSpec((S5_GROUPS, S5_NCH, S5_WIDTH), lambda b: (0, b, 0)), xspec, xspec, wspec, wspec],
        out_specs=pl.BlockSpec((1, 2, SEQ, LANES), lambda b: (b, 0, 0, 0)),
        out_shape=jax.ShapeDtypeStruct((BATCH, 2, SEQ, LANES), F32),
        compiler_params=pltpu.CompilerParams(
            dimension_semantics=("parallel",), vmem_limit_bytes=VMEM_LIMIT),
        name="s5_out",
    )(z, xc_re, xc_im, m, wc)


def _mixer_kernel(x_ref, y_ref, g1_ref, win_ref, bga_ref, bgb_ref, lng_ref, lnb_ref,
                  ws_ref, bs_ref, wglu_ref, bglu_ref, wupa_ref, wupb_ref, wout_ref, g2_ref, wr_ref,
                  ewg_ref, ewu_ref, ewd_ref,
                  x1_ref, h2p_ref, pr_ref, owg_ref, owu_ref, owd_ref, bra_scr, mrg_scr):
    for src, dst in ((ewg_ref, owg_ref), (ewu_ref, owu_ref), (ewd_ref, owd_ref)):
        dst[...] = src[...].astype(BF16)

    x = x_ref[0]
    hn = _rms(x, g1_ref[...]).astype(BF16)

    def in_proj(c0):
        return jnp.dot(hn, win_ref[:, c0:c0 + MIX_COLS], preferred_element_type=F32)

    nuv = GMLP_WIDTH // MIX_COLS
    us = [_gelu(in_proj(c * MIX_COLS)) for c in range(nuv)]
    vs = [_gelu(in_proj(GMLP_WIDTH + c * MIX_COLS)) for c in range(nuv)]
    mu = sum(jnp.sum(v, axis=-1, keepdims=True) for v in vs) * (1.0 / GMLP_WIDTH)
    vcs = [v - mu for v in vs]
    var = sum(jnp.sum(vc * vc, axis=-1, keepdims=True) for vc in vcs) * (1.0 / GMLP_WIDTH)
    rstd = lax.rsqrt(var + EPS)
    vns = [(vc * rstd * lng_ref[:, c * MIX_COLS:(c + 1) * MIX_COLS]
            + lnb_ref[:, c * MIX_COLS:(c + 1) * MIX_COLS]).astype(BF16) for c, vc in enumerate(vcs)]
    low = lax.broadcasted_iota(I32, (CHUNK, LANES), 1) < GMLP_HEAD
    per = MIX_COLS // LANES
    for ch in range(TILE_M // CHUNK):
        r0 = ch * CHUNK
        for q in range(GMLP_WIDTH // LANES):
            c0 = q * LANES
            l0 = (q % per) * LANES
            vb = vns[q // per][r0:r0 + CHUNK, l0:l0 + LANES]
            z0 = jnp.dot(ws_ref[2 * q], vb, preferred_element_type=F32)
            z1 = jnp.dot(ws_ref[2 * q + 1], vb, preferred_element_type=F32)
            zz = jnp.where(low, z0, z1) + bs_ref[:, c0:c0 + LANES]
            bra_scr[r0:r0 + CHUNK, c0:c0 + LANES] = (us[q // per][r0:r0 + CHUNK, l0:l0 + LANES] * zz).astype(BF16)

    yb = _gelu(jnp.concatenate([y_ref[0, 0], y_ref[0, 1]], axis=1))
    gl = jnp.dot(yb.astype(BF16), wglu_ref[...], preferred_element_type=F32) + bglu_ref[...]
    brb = (yb * jax.nn.sigmoid(gl)).astype(BF16)

    bra = bra_scr[...]
    for c in range(D_MODEL // MIX_COLS):
        cs = slice(c * MIX_COLS, (c + 1) * MIX_COLS)
        ga = jax.nn.sigmoid(in_proj(O_GATE + c * MIX_COLS) + bga_ref[:, cs])
        gb = jax.nn.sigmoid(in_proj(O_GATE + D_MODEL + c * MIX_COLS) + bgb_ref[:, cs])
        mrg_scr[:, cs] = (ga * jnp.dot(bra, wupa_ref[:, cs], preferred_element_type=F32)
                          + gb * jnp.dot(brb, wupb_ref[:, cs], preferred_element_type=F32)).astype(BF16)
    mrg = mrg_scr[...]
    ssq = jnp.zeros((TILE_M, 1), F32)
    for c in range(D_MODEL // MIX_COLS):
        cs = slice(c * MIX_COLS, (c + 1) * MIX_COLS)
        x1c = x[:, cs] + jnp.dot(mrg, wout_ref[:, cs], preferred_element_type=F32)
        x1_ref[0, :, cs] = x1c
        ssq = ssq + jnp.sum(x1c * x1c, axis=-1, keepdims=True)

    h2 = x1_ref[0] * lax.rsqrt(ssq * (1.0 / D_MODEL) + EPS) * g2_ref[...]
    half = D_MODEL // 2
    h2p_ref[0] = pltpu.pack_elementwise([h2[:, :half], h2[:, half:]], packed_dtype=BF16)

    lg = lax.dot_general(wr_ref[...], h2.astype(BF16), _NT, preferred_element_type=F32)
    logits = lg[:N_EXPERTS] + lg[N_EXPERTS:]
    mx = jnp.max(logits, axis=0, keepdims=True)
    ex = jnp.exp(logits - mx)
    pr_ref[0] = ex / jnp.sum(ex, axis=0, keepdims=True)


def _mixer(x, y, consts, expert_w):
    nt = SEQ // TILE_M
    nstep = BATCH * nt

    def cspec(a):
        nd = a.ndim
        return pl.BlockSpec(a.shape, lambda b, i: (0,) * nd, pipeline_mode=pl.Buffered(1))

    flat = [w.reshape(-1, w.shape[-1]) for w in expert_w]
    wspecs = [pl.BlockSpec((w.shape[0] // nstep, w.shape[1]), lambda b, i: (b * nt + i, 0)) for w in flat]
    outs = pl.pallas_call(
        _mixer_kernel,
        grid=(BATCH, nt),
        in_specs=[pl.BlockSpec((1, TILE_M, D_MODEL), lambda b, i: (b, i, 0)),
                  pl.BlockSpec((1, 2, TILE_M, LANES), lambda b, i: (b, 0, i, 0))]
                 + [cspec(a) for a in consts] + wspecs,
        out_specs=[pl.BlockSpec((1, TILE_M, D_MODEL), lambda b, i: (b, i, 0)),
                   pl.BlockSpec((1, TILE_M, D_MODEL // 2), lambda b, i: (b, i, 0)),
                   pl.BlockSpec((1, N_EXPERTS, TILE_M), lambda b, i: (b, 0, i))] + wspecs,
        out_shape=[jax.ShapeDtypeStruct((BATCH, SEQ, D_MODEL), F32),
                   jax.ShapeDtypeStruct((BATCH, SEQ, D_MODEL // 2), U32),
                   jax.ShapeDtypeStruct((BATCH, N_EXPERTS, SEQ), F32)]
                  + [jax.ShapeDtypeStruct(w.shape, BF16) for w in flat],
        scratch_shapes=[pltpu.VMEM((TILE_M, GMLP_WIDTH), BF16), pltpu.VMEM((TILE_M, D_MODEL), BF16)],
        compiler_params=pltpu.CompilerParams(
            dimension_semantics=("parallel", "parallel"), vmem_limit_bytes=VMEM_LIMIT_MIXER),
        name="mixer",
    )(x, y, *consts, *flat)
    return outs[:3], [o.reshape(w.shape) for o, w in zip(outs[3:], expert_w)]


TOPK_NBLK = SEQ // LANES
TOPK_ROWS = 4 * N_EXPERTS
TOPK_SEARCH_STEPS = 28


def _lane_sum(x):
    return jnp.sum(x, axis=1, keepdims=True)


def _topk_kernel(p_ref, tri_ref, idx_ref, ilo_ref, gate_ref, eqc_scr, loc_scr, pj_scr, cb_scr, cbp_scr):
    p = p_ref[...]

    def count_ge(th):
        return _lane_sum(jnp.where(p >= th, 1.0, 0.0))

    def search(_, c):
        lo, hi = c
        w = hi - lo
        m1, m2, m3 = lo + 0.25 * w, lo + 0.5 * w, lo + 0.75 * w
        c1, c2, c3 = count_ge(m1) >= CAPACITY, count_ge(m2) >= CAPACITY, count_ge(m3) >= CAPACITY
        return (jnp.where(c3, m3, jnp.where(c2, m2, jnp.where(c1, m1, lo))),
                jnp.where(c1, jnp.where(c2, jnp.where(c3, hi, m3), m2), m1))

    lo, hi = lax.fori_loop(0, TOPK_SEARCH_STEPS, search,
                           (jnp.zeros((TOPK_ROWS, 1), F32), jnp.full((TOPK_ROWS, 1), 2.0, F32)))
    gt = p >= hi
    eq = (p >= lo) & (p < hi)
    need = CAPACITY - _lane_sum(jnp.where(gt, 1.0, 0.0))
    tri = tri_ref[...]
    eqf = jnp.where(eq, 1.0, 0.0)
    carry = jnp.zeros((TOPK_ROWS, 1), F32)
    for j in range(TOPK_NBLK):
        sl = slice(j * LANES, (j + 1) * LANES)
        cs = jnp.dot(eqf[:, sl].astype(BF16), tri, preferred_element_type=F32) + carry
        eqc_scr[:, sl] = cs
        carry = cs[:, LANES - 1:LANES]
    sel = gt | (eq & ((eqc_scr[...] - eqf) < need))
    self = jnp.where(sel, 1.0, 0.0)

    lane = lax.broadcasted_iota(I32, (TOPK_ROWS, LANES), 1)
    cb = jnp.full((TOPK_ROWS, LANES), float(2 * SEQ), F32)
    cbp = cb
    carry = jnp.zeros((TOPK_ROWS, 1), F32)
    for j in range(TOPK_NBLK):
        sl = slice(j * LANES, (j + 1) * LANES)
        loc = jnp.dot(self[:, sl].astype(BF16), tri, preferred_element_type=F32)
        loc_scr[j * TOPK_ROWS:(j + 1) * TOPK_ROWS, :] = loc
        pj_scr[j * TOPK_ROWS:(j + 1) * TOPK_ROWS, :] = p[:, sl]
        cbp = jnp.where(lane == j, carry, cbp)
        carry = carry + loc[:, LANES - 1:LANES]
        cb = jnp.where(lane == j, carry, cb)
    padr = jnp.zeros((LANES - TOPK_ROWS, LANES), F32)
    cb_scr[...] = jnp.concatenate([cb, padr], axis=0).T
    cbp_scr[...] = jnp.concatenate([cbp, padr], axis=0).T

    slot = (lax.broadcasted_iota(I32, (1, CAPACITY), 1) + 1).astype(F32)
    lane_r = lax.broadcasted_iota(I32, (LANES, LANES), 1)
    sub_f = lax.broadcasted_iota(I32, (LANES, 1), 0).astype(F32)
    pad = jnp.zeros((LANES - TOPK_NBLK, LANES), F32)

    def block_rows_t(ref, e):
        return jnp.concatenate([ref[pl.ds(e, TOPK_NBLK, stride=TOPK_ROWS), :], pad], axis=0).T

    def sub_sum(x):
        return jnp.sum(x, axis=0, keepdims=True)

    def per_expert(e, _):
        cb_e = _lane_sum(jnp.where(lane_r == e, cb_scr[...], 0.0))
        cbp_e = _lane_sum(jnp.where(lane_r == e, cbp_scr[...], 0.0))
        hit = (cbp_e < slot) & (slot <= cb_e)
        hitb = jnp.where(hit, 1.0, 0.0).astype(BF16)
        both = sub_sum(jnp.where(hit, cbp_e * float(LANES) + sub_f, 0.0))
        before = jnp.floor(both * (1.0 / LANES))
        jblk = both - before * float(LANES)
        rank = slot - before
        loc = jnp.dot(block_rows_t(loc_scr, e).astype(BF16), hitb, preferred_element_type=F32)
        off = sub_sum(jnp.where(loc < rank, 1.0, 0.0))
        pe = block_rows_t(pj_scr, e)
        hi = pe.astype(BF16)
        r1 = pe - hi.astype(F32)
        mid = r1.astype(BF16)
        lo = (r1 - mid.astype(F32)).astype(BF16)
        pg = (jnp.dot(hi, hitb, preferred_element_type=F32) + jnp.dot(mid, hitb, preferred_element_type=F32)
              + jnp.dot(lo, hitb, preferred_element_type=F32))
        tok = jblk * float(LANES) + off
        tile = jnp.floor(tok * 0.125)
        idx_ref[pl.ds(e, 1), :] = tile.astype(I32)
        ilo_ref[pl.ds(e, 1), :] = (tok - tile * 8.0).astype(I32)
        gate_ref[pl.ds(e, 1), :] = sub_sum(jnp.where(sub_f == off, pg, 0.0))
        return 0

    lax.fori_loop(0, TOPK_ROWS, per_expert, 0, unroll=4)


def _topk(probs, tri):
    nrow = BATCH * N_EXPERTS
    return pl.pallas_call(
        _topk_kernel,
        grid=(nrow // TOPK_ROWS,),
        in_specs=[pl.BlockSpec((TOPK_ROWS, SEQ), lambda i: (i, 0)),
                  pl.BlockSpec((LANES, LANES), lambda i: (0, 0))],
        out_specs=[pl.BlockSpec((TOPK_ROWS, CAPACITY), lambda i: (i, 0)),
                   pl.BlockSpec((TOPK_ROWS, CAPACITY), lambda i: (i, 0)),
                   pl.BlockSpec((TOPK_ROWS, CAPACITY), lambda i: (i, 0))],
        out_shape=[jax.ShapeDtypeStruct((nrow, CAPACITY), I32),
                   jax.ShapeDtypeStruct((nrow, CAPACITY), I32),
                   jax.ShapeDtypeStruct((nrow, CAPACITY), F32)],
        scratch_shapes=[pltpu.VMEM((TOPK_ROWS, SEQ), F32),
                        pltpu.VMEM((TOPK_NBLK * TOPK_ROWS, LANES), F32),
                        pltpu.VMEM((TOPK_NBLK * TOPK_ROWS, LANES), F32),
                        pltpu.VMEM((LANES, LANES), F32),
                        pltpu.VMEM((LANES, LANES), F32)],
        compiler_params=pltpu.CompilerParams(
            dimension_semantics=("parallel",), vmem_limit_bytes=VMEM_LIMIT),
        name="topk",
    )(probs, tri)


MOE_GROUP = 8
MOE_FCHUNK = 256
MOE_NBUF = 1 if TILE_F == EXPERT_FF else 2


def _gather_rows(idx_ref, ibase, h2p_ref, dst_ref, dbase, nrows):
    for k in range(nrows):
        hi_ref, lo_ref = idx_ref
        dst_ref[pl.ds(dbase + k, 1), :] = h2p_ref[0, hi_ref[0, 0, ibase + k], pl.ds(lo_ref[0, 0, ibase + k], 1), :]


def _scatter_rows(idx_ref, gate_ref, ibase, y_ref, ybase, out_ref, nrows):
    for k0 in range(0, nrows, MOE_GROUP):
        rs, vals = [], []
        for k in range(k0, k0 + MOE_GROUP):
            hi_ref, lo_ref = idx_ref
            r = (hi_ref[0, 0, ibase + k], lo_ref[0, 0, ibase + k])
            g = gate_ref[0, 0, ibase + k]
            rs.append(r)
            vals.append(out_ref[0, r[0], pl.ds(r[1], 1), :] + g * y_ref[pl.ds(ybase + k, 1), :])
        for r, v in zip(rs, vals):
            out_ref[0, r[0], pl.ds(r[1], 1), :] = v


def _moe_kernel(ihi_ref, ilo_ref, gate_ref, h2p_ref, x1_ref, gf_ref, wg_ref, wu_ref, wd_ref, out_ref,
                xg_scr, xs_scr, y_scr):
    idx_ref = (ihi_ref, ilo_ref)
    e = pl.program_id(1)
    f = pl.program_id(2)
    nf = EXPERT_FF // TILE_F
    nx = SEQ // TILE_X
    step = e * nf + f
    rq = CAPACITY // nf
    cur = e % MOE_NBUF
    nxt = (e + 1) % MOE_NBUF
    xg_cur, xg_nxt = xg_scr.at[cur], xg_scr.at[nxt]
    y_cur, y_nxt = y_scr.at[cur], y_scr.at[nxt]

    @pl.when(step == 0)
    def _():
        out_ref[...] = jnp.zeros_like(out_ref)
        y_scr[...] = jnp.zeros_like(y_scr)

        def gather(i, _):
            base = pl.multiple_of(i * 32, 32)
            _gather_rows(idx_ref, base, h2p_ref, xg_cur, base, 32)
            return 0

        lax.fori_loop(0, CAPACITY // 32, gather, 0)

    w = xg_cur[...]
    half = D_MODEL // 2
    for i in range(2):
        xs_scr[:, i * half:(i + 1) * half] = pltpu.unpack_elementwise(
            w, index=i, packed_dtype=BF16, unpacked_dtype=F32).astype(BF16)
    xrows = pl.ds(jnp.minimum(step, nx - 1) * (TILE_X // SUBLANES), TILE_X // SUBLANES)
    out_ref[0, xrows] = out_ref[0, xrows] + jnp.where(step < nx, x1_ref[0], 0.0)
    qbase = pl.multiple_of(f * rq, rq)
    e_prev = (e + N_EXPERTS - 1) % N_EXPERTS
    e_next = (e + 1) % N_EXPERTS
    nchunk = TILE_F // MOE_FCHUNK
    rc = rq // nchunk

    xs = xs_scr[...]
    y = jnp.where(f > 0, y_cur[...], 0.0) if nf > 1 else None

    def gate_up(c0):
        cs = slice(c0, c0 + MOE_FCHUNK)
        return (jnp.dot(xs, wg_ref[0, :, cs], preferred_element_type=F32),
                jnp.dot(xs, wu_ref[0, :, cs], preferred_element_type=F32))

    a, u = gate_up(0)
    for c in range(nchunk):
        c0 = c * MOE_FCHUNK
        rbase = qbase + c * rc
        _scatter_rows(idx_ref, gate_ref, e_prev * CAPACITY + rbase, y_nxt, rbase, out_ref, rc)
        _gather_rows(idx_ref, e_next * CAPACITY + rbase, h2p_ref, xg_nxt, rbase, rc)
        nxt_au = gate_up(c0 + MOE_FCHUNK) if c + 1 < nchunk else None
        hm = (a * jax.nn.sigmoid(a) * u).astype(BF16)
        yc = jnp.dot(hm, wd_ref[0, c0:c0 + MOE_FCHUNK, :], preferred_element_type=F32)
        y = yc if y is None else y + yc
        if nxt_au is not None:
            a, u = nxt_au
    y_cur[...] = y

    @pl.when((e == N_EXPERTS - 1) & (f == nf - 1))
    def _():
        def scatter(i, _):
            base = pl.multiple_of(i * 32, 32)
            _scatter_rows(idx_ref, gate_ref, e * CAPACITY + base, y_cur, base, out_ref, 32)
            return 0

        lax.fori_loop(0, CAPACITY // 32, scatter, 0)

        def norm(i, _):
            rows = pl.ds(i * (TILE_X // SUBLANES), TILE_X // SUBLANES)
            out_ref[0, rows] = _rms(out_ref[0, rows], gf_ref[...])
            return 0

        lax.fori_loop(0, SEQ // TILE_X, norm, 0)


def _moe(idx, gate, h2p, x1, gf, wg, wu, wd):
    nf = EXPERT_FF // TILE_F
    nx = SEQ // TILE_X
    nt = SEQ // SUBLANES
    assert nx <= N_EXPERTS * nf
    flat = lambda a: a.reshape(BATCH, 1, N_EXPERTS * CAPACITY)
    sspec = pl.BlockSpec((1, 1, N_EXPERTS * CAPACITY), lambda b, e, f: (b, 0, 0), memory_space=pltpu.SMEM)
    out = pl.pallas_call(
        _moe_kernel,
        grid=(BATCH, N_EXPERTS, nf),
        in_specs=[sspec, sspec, sspec,
                  pl.BlockSpec((1, nt, SUBLANES, D_MODEL // 2), lambda b, e, f: (b, 0, 0, 0),
                               pipeline_mode=pl.Buffered(1)),
                  pl.BlockSpec((1, TILE_X // SUBLANES, SUBLANES, D_MODEL),
                               lambda b, e, f: (b, jnp.minimum(e * nf + f, nx - 1), 0, 0)),
                  pl.BlockSpec((1, D_MODEL), lambda b, e, f: (0, 0)),
                  pl.BlockSpec((1, D_MODEL, TILE_F), lambda b, e, f: (e, 0, f)),
                  pl.BlockSpec((1, D_MODEL, TILE_F), lambda b, e, f: (e, 0, f)),
                  pl.BlockSpec((1, TILE_F, D_MODEL), lambda b, e, f: (e, f, 0))],
        out_specs=pl.BlockSpec((1, nt, SUBLANES, D_MODEL), lambda b, e, f: (b, 0, 0, 0),
                               pipeline_mode=pl.Buffered(1)),
        out_shape=jax.ShapeDtypeStruct((BATCH, nt, SUBLANES, D_MODEL), F32),
        scratch_shapes=[pltpu.VMEM((MOE_NBUF, CAPACITY, D_MODEL // 2), U32),
                        pltpu.VMEM((CAPACITY, D_MODEL), BF16),
                        pltpu.VMEM((MOE_NBUF, CAPACITY, D_MODEL), F32)],
        compiler_params=pltpu.CompilerParams(
            dimension_semantics=("parallel", "arbitrary", "arbitrary"), vmem_limit_bytes=VMEM_LIMIT_MOE),
        name="moe",
    )(flat(idx[0]), flat(idx[1]), flat(gate), h2p.reshape(BATCH, nt, SUBLANES, D_MODEL // 2),
      x1.reshape(BATCH, nt, SUBLANES, D_MODEL), gf, wg, wu, wd)
    return out.reshape(BATCH, SEQ, D_MODEL)


def kernel(x, norm1_g, w_in, b_gate, gmlp_ln_g, gmlp_ln_b, gmlp_w_s, gmlp_b_s, s5_lam_re, s5_lam_im, s5_log_dt,
           s5_b_re, s5_b_im, s5_c_re, s5_c_im, s5_d, s5_w_glu, s5_b_glu, w_up_a, w_up_b, w_out, norm2_g,
           w_router, w_gate, w_up, w_down, final_g):
    assert x.shape == (BATCH, SEQ, D_MODEL) and w_in.shape[0] == 1
    l = 0
    row = lambda a: a.astype(F32).reshape(1, -1)

    w_in_b = w_in[l].astype(BF16)
    m_mat, wsum, wc, a_re, a_im = _s5_tables(s5_lam_re[l], s5_lam_im[l], s5_log_dt[l], s5_b_re[l], s5_b_im[l],
                                             s5_c_re[l], s5_c_im[l], s5_d[l])
    g1 = row(norm1_g[l])

    z, s_re, s_im = _s5_in(x, g1, w_in_b, wsum)
    shape3 = (S5_NCH, S5_GROUPS, BATCH * LANES)
    xc_re, xc_im = _s5_scan(s_re.reshape(shape3), s_im.reshape(shape3),
                            jnp.tile(a_re, (1, BATCH)), jnp.tile(a_im, (1, BATCH)))
    shape2 = (S5_NCH * S5_GROUPS, BATCH * LANES)
    y = _s5_out(z, xc_re.reshape(shape2), xc_im.reshape(shape2), m_mat, wc)

    wr_t = jnp.transpose(w_router[l].astype(F32))
    wr_hi = wr_t.astype(BF16)
    wr_lo = (wr_t - wr_hi.astype(F32)).astype(BF16)
    consts = (
        g1, w_in_b, row(b_gate[l][:D_MODEL]), row(b_gate[l][D_MODEL:]), row(gmlp_ln_g[l]), row(gmlp_ln_b[l]),
        gmlp_w_s[l].astype(BF16), jnp.repeat(jnp.transpose(gmlp_b_s[l].astype(F32)), GMLP_HEAD, axis=1),
        s5_w_glu[l].astype(BF16), row(s5_b_glu[l]), w_up_a[l].astype(BF16), w_up_b[l].astype(BF16),
        w_out[l].astype(BF16), row(norm2_g[l]), jnp.concatenate([wr_hi, wr_lo], axis=0),
    )
    (x1, h2p, probs), (wg_b, wu_b, wd_b) = _mixer(x, y, consts, (w_gate[l], w_up[l], w_down[l]))

    tri = (jnp.arange(LANES)[:, None] <= jnp.arange(LANES)[None, :]).astype(BF16)
    itile, isub, gate = _topk(probs.reshape(BATCH * N_EXPERTS, SEQ), tri)
    return _moe((itile, isub), gate, h2p, x1, row(final_g), wg_b, wu_b, wd_b)
```
